```python
import math
import jax, jax.numpy as jnp
from jax import lax
import numpy as np

D_MODEL = 2048
BATCH = 16
SEQ = 2048
DEPTH = 2

SSD_HEADS = 32
SSD_HEAD_DIM = 64
SSD_INNER = SSD_HEADS * SSD_HEAD_DIM
SSD_GROUPS = 8
SSD_STATE = 128
SSD_CHUNK = 128
CONV_WIDTH = 5
XBC_WIDTH = SSD_INNER + 2 * SSD_GROUPS * SSD_STATE
DT_MIN = 0.001
DT_MAX = 0.1
HEAD_DIM = 128
ROPE_DIM = HEAD_DIM // 4
ROPE_THETA = 500000.0
DIL_PATTERNS = ((128, 1), (512, 4), (2048, 16))
DIL_GROUPS = len(DIL_PATTERNS)
DIL_HEADS = 8
DIL_WIDTH = DIL_HEADS * HEAD_DIM
WIN_Q_HEADS = 16
WIN_KV_HEADS = 4
WIN_HALF = 128
D_FF = 4 * D_MODEL
N_BRANCH = 3
EPS = 1e-6
NEG_INF = -1e30

SPLITS = (SSD_INNER,
          XBC_WIDTH,
          2 * SSD_HEADS,
          3 * DIL_GROUPS * DIL_WIDTH,
          WIN_Q_HEADS * HEAD_DIM,
          WIN_KV_HEADS * HEAD_DIM,
          WIN_KV_HEADS * HEAD_DIM,
          N_BRANCH * D_MODEL)
N_IN = sum(SPLITS)
SPLIT_POINTS = tuple(int(v) for v in np.cumsum(SPLITS)[:-1])

kernel_name = "gated_parallel_ssd_dilated_window_hybrid"


def rms_norm(x, g):
    xf = x.astype(jnp.float32)
    y = xf * lax.rsqrt(jnp.mean(xf * xf, axis=-1, keepdims=True) + EPS)
    return (y * g.astype(jnp.float32)).astype(x.dtype)


def rope_tables(seq):
    inv = ROPE_THETA ** (-jnp.arange(0, ROPE_DIM, 2, dtype=jnp.float32) / ROPE_DIM)
    ang = jnp.arange(seq, dtype=jnp.float32)[:, None] * inv[None, :]
    return jnp.cos(ang), jnp.sin(ang)


def partial_rope(t, cos, sin):
    shp = (1, cos.shape[0]) + (1,) * (t.ndim - 3) + (cos.shape[1],)
    c, s = cos.reshape(shp), sin.reshape(shp)
    t1, t2 = jnp.split(t[..., :ROPE_DIM].astype(jnp.float32), 2, axis=-1)
    rot = jnp.concatenate([t1 * c - t2 * s, t1 * s + t2 * c], axis=-1).astype(t.dtype)
    return jnp.concatenate([rot, t[..., ROPE_DIM:]], axis=-1)


def banded_attention(q, k, v, half_window, sink_logits=None):
    b, l, hq, dh = q.shape
    hkv = k.shape[2]
    rep = hq // hkv
    blk = half_window
    nb = -(-l // blk)
    lp = nb * blk
    qb = jnp.pad(q, ((0, 0), (0, lp - l), (0, 0), (0, 0))).reshape(b, nb, blk, hkv, rep, dh)

    def windows(t):
        tp = jnp.pad(t, ((0, 0), (blk, lp - l + blk), (0, 0), (0, 0))).reshape(b, nb + 2, blk, hkv, dh)
        return jnp.concatenate([tp[:, :-2], tp[:, 1:-1], tp[:, 2:]], axis=2)

    kw, vw = windows(k), windows(v)
    scores = jnp.einsum('bnqgrd,bnkgd->bngrqk', qb, kw,
                        preferred_element_type=jnp.float32) * (dh ** -0.5)
    qpos = jnp.arange(lp).reshape(nb, blk)
    kpos = jnp.arange(nb)[:, None] * blk + jnp.arange(3 * blk)[None, :] - blk
    valid = ((jnp.abs(qpos[:, :, None] - kpos[:, None, :]) <= half_window)
             & (kpos[:, None, :] >= 0) & (kpos[:, None, :] < l))
    scores = jnp.where(valid[None, :, None, None], scores, NEG_INF)
    lse = jax.nn.logsumexp(scores, axis=-1)
    if sink_logits is not None:
        lse = jnp.logaddexp(lse, sink_logits.astype(jnp.float32).reshape(1, 1, hkv, rep, 1))
    p = jnp.exp(scores - lse[..., None])
    out = jnp.einsum('bngrqk,bnkgd->bnqgrd', p.astype(v.dtype), vw).reshape(b, lp, hq, dh)[:, :l]
    lse = lse.transpose(0, 1, 4, 2, 3).reshape(b, lp, hq)[:, :l]
    return out, lse


def dilated_mixture_attention(q, k, v):
    b, s = q.shape[:2]
    outs, lses = [], []
    for gi, (window, dil) in enumerate(DIL_PATTERNS):
        n_sub = s // dil

        def by_stride(t):
            t = t.reshape(b, n_sub, dil, DIL_HEADS, HEAD_DIM).transpose(0, 2, 1, 3, 4)
            return t.reshape(b * dil, n_sub, DIL_HEADS, HEAD_DIM)

        o, lse = banded_attention(by_stride(q[:, :, gi]), by_stride(k[:, :, gi]),
                                  by_stride(v[:, :, gi]), window // (2 * dil))
        outs.append(o.reshape(b, dil, n_sub, DIL_HEADS, HEAD_DIM)
                    .transpose(0, 2, 1, 3, 4).reshape(b, s, DIL_HEADS, HEAD_DIM))
        lses.append(lse.reshape(b, dil, n_sub, DIL_HEADS).transpose(0, 2, 1, 3).reshape(b, s, DIL_HEADS))
    weights = jax.nn.softmax(jnp.stack(lses, axis=0), axis=0)
    out = jnp.einsum('gbsh,gbshd->bshd', weights, jnp.stack(outs, axis=0).astype(jnp.float32))
    return out.astype(q.dtype).reshape(b, s, DIL_WIDTH)


def ssd_scan(x, dt, a, bm, cm):
    b, l, h, p = x.shape
    g, n = bm.shape[2], bm.shape[3]
    hg = h // g
    t = SSD_CHUNK
    c = l // t
    x = x.reshape(b, c, t, g, hg, p)
    dt = dt.reshape(b, c, t, g, hg)
    bm = bm.reshape(b, c, t, g, n)
    cm = cm.reshape(b, c, t, g, n)
    cs = jnp.cumsum(dt * a.reshape(g, hg), axis=2)
    xdt = x * dt[..., None]
    lower = jnp.tril(jnp.ones((t, t), dtype=bool))[None, None, :, :, None, None]
    seg = cs[:, :, :, None] - cs[:, :, None, :]
    decay = jnp.exp(jnp.where(lower, seg, -jnp.inf))
    cb = jnp.einsum('bclgn,bcsgn->bclsg', cm, bm)
    y_diag = jnp.einsum('bclsg,bclsgh,bcsghp->bclghp', cb, decay, xdt)
    decay_states = jnp.exp(cs[:, :, -1:] - cs)
    states = jnp.einsum('bctgn,bctgh,bctghp->bcghpn', bm, decay_states, xdt)
    chunk_decay = jnp.exp(cs[:, :, -1])

    def step(carry, inp):
        st, dec = inp
        return carry * dec[..., None, None] + st, carry

    init = jnp.zeros((b, g, hg, p, n), dtype=x.dtype)
    _, prev = lax.scan(step, init, (states.transpose(1, 0, 2, 3, 4, 5), chunk_decay.transpose(1, 0, 2, 3)))
    prev = prev.transpose(1, 0, 2, 3, 4, 5)
    y_off = jnp.einsum('bctgn,bcghpn,bctgh->bctghp', cm, prev, jnp.exp(cs))
    return (y_diag + y_off).reshape(b, l, h, p)


def ssd_mixer(z, xbc, dt_raw, conv_w, conv_b, dt_bias, a_log, d_skip, ssd_norm):
    b, s, _ = xbc.shape
    pad = (CONV_WIDTH - 1) // 2
    xbc = lax.conv_general_dilated(xbc, conv_w[:, None, :].astype(xbc.dtype), window_strides=(1,),
                                   padding=[(pad, pad)], dimension_numbers=('NWC', 'WIO', 'NWC'),
                                   feature_group_count=XBC_WIDTH) + conv_b
    xbc = jax.nn.silu(xbc).astype(jnp.float32)
    xs, bm, cm = jnp.split(xbc, [SSD_INNER, SSD_INNER + SSD_GROUPS * SSD_STATE], axis=-1)
    xs = xs.reshape(b, s, SSD_HEADS, SSD_HEAD_DIM)
    bm = bm.reshape(b, s, SSD_GROUPS, SSD_STATE)
    cm = cm.reshape(b, s, SSD_GROUPS, SSD_STATE)
    a = -jnp.exp(a_log.astype(jnp.float32))
    dt = jax.nn.softplus(dt_raw.astype(jnp.float32).reshape(b, s, 2, SSD_HEADS)
                         + dt_bias.astype(jnp.float32))
    flip = lambda t: jnp.flip(t, axis=1)
    y_fwd = ssd_scan(xs, dt[:, :, 0], a[0], bm, cm)
    y_bwd = flip(ssd_scan(flip(xs), flip(dt[:, :, 1]), a[1], flip(bm), flip(cm)))
    y = y_fwd + y_bwd + d_skip.astype(jnp.float32)[:, None] * xs
    y = y.reshape(b, s, SSD_INNER) * jax.nn.silu(z.astype(jnp.float32))
    return rms_norm(y, ssd_norm).astype(z.dtype)


def hybrid_layer(x, cos, sin, g_mix, w_in, conv_w, conv_b, dt_bias, a_log, d_skip, ssd_norm,
                 w_a, w_b, w_c, sink, w_out, g_mlp, w_up, w_down):
    b, s, _ = x.shape
    h = rms_norm(x, g_mix)
    proj = h @ w_in
    z, xbc, dt_raw, qkv_d, q_w, k_w, v_w, gate_logits = jnp.split(proj, SPLIT_POINTS, axis=-1)
    y_a = ssd_mixer(z, xbc, dt_raw, conv_w, conv_b, dt_bias, a_log, d_skip, ssd_norm)
    qkv_d = partial_rope(qkv_d.reshape(b, s, 3 * DIL_GROUPS * DIL_HEADS, HEAD_DIM), cos, sin)
    qkv_d = qkv_d.reshape(b, s, 3, DIL_GROUPS, DIL_HEADS, HEAD_DIM)
    y_b = dilated_mixture_attention(qkv_d[:, :, 0], qkv_d[:, :, 1], qkv_d[:, :, 2])
    q_w = partial_rope(q_w.reshape(b, s, WIN_Q_HEADS, HEAD_DIM), cos, sin)
    k_w = partial_rope(k_w.reshape(b, s, WIN_KV_HEADS, HEAD_DIM), cos, sin)
    v_w = v_w.reshape(b, s, WIN_KV_HEADS, HEAD_DIM)
    y_c, _ = banded_attention(q_w, k_w, v_w, WIN_HALF, sink)
    y_c = y_c.reshape(b, s, WIN_Q_HEADS * HEAD_DIM)
    gates = jax.nn.sigmoid(gate_logits.astype(jnp.float32)).astype(x.dtype).reshape(b, s, N_BRANCH, D_MODEL)
    merged = gates[:, :, 0] * (y_a @ w_a) + gates[:, :, 1] * (y_b @ w_b) + gates[:, :, 2] * (y_c @ w_c)
    x = x + merged @ w_out
    hm = rms_norm(x, g_mlp)
    return x + jnp.square(jax.nn.relu(hm @ w_up)) @ w_down


def setup_inputs(seed: int = 0) -> dict:
    key = jax.random.key(seed)
    ks = jax.random.split(key, 18)
    f32 = jnp.float32

    def normal(k, shape, scale):
        return jax.random.normal(k, shape, f32) * scale

    x = normal(ks[0], (BATCH, SEQ, D_MODEL), 1.0)
    g_mix = 1.0 + normal(ks[1], (DEPTH, D_MODEL), 0.02)
    w_in = normal(ks[2], (DEPTH, D_MODEL, N_IN), D_MODEL ** -0.5)
    conv_w = normal(ks[3], (DEPTH, CONV_WIDTH, XBC_WIDTH), CONV_WIDTH ** -0.5)
    conv_b = normal(ks[4], (DEPTH, XBC_WIDTH), 0.01)
    dt0 = jnp.exp(jax.random.uniform(ks[5], (DEPTH, 2, SSD_HEADS), f32, math.log(DT_MIN), math.log(DT_MAX)))
    dt_bias = dt0 + jnp.log(-jnp.expm1(-dt0))
    a_log = jnp.log(jax.random.uniform(ks[6], (DEPTH, 2, SSD_HEADS), f32, 1.0, 16.0))
    d_skip = 1.0 + normal(ks[7], (DEPTH, SSD_HEADS), 0.1)
    ssd_norm = 1.0 + normal(ks[8], (DEPTH, SSD_INNER), 0.02)
    w_a = normal(ks[9], (DEPTH, SSD_INNER, D_MODEL), SSD_INNER ** -0.5)
    w_b = normal(ks[10], (DEPTH, DIL_WIDTH, D_MODEL), DIL_WIDTH ** -0.5)
    w_c = normal(ks[11], (DEPTH, WIN_Q_HEADS * HEAD_DIM, D_MODEL), (WIN_Q_HEADS * HEAD_DIM) ** -0.5)
    sink = normal(ks[12], (DEPTH, WIN_Q_HEADS), 0.5)
    w_out = normal(ks[13], (DEPTH, D_MODEL, D_MODEL), D_MODEL ** -0.5)
    g_mlp = 1.0 + normal(ks[14], (DEPTH, D_MODEL), 0.02)
    w_up = normal(ks[15], (DEPTH, D_MODEL, D_FF), D_MODEL ** -0.5)
    w_down = normal(ks[16], (DEPTH, D_FF, D_MODEL), D_FF ** -0.5)
    g_final = 1.0 + normal(ks[17], (D_MODEL,), 0.02)
    return {"x": x, "g_mix": g_mix, "w_in": w_in, "conv_w": conv_w, "conv_b": conv_b,
            "dt_bias": dt_bias, "a_log": a_log, "d_skip": d_skip, "ssd_norm": ssd_norm,
            "w_a": w_a, "w_b": w_b, "w_c": w_c, "sink": sink, "w_out": w_out,
            "g_mlp": g_mlp, "w_up": w_up, "w_down": w_down, "g_final": g_final}


def reference(x, g_mix, w_in, conv_w, conv_b, dt_bias, a_log, d_skip, ssd_norm,
              w_a, w_b, w_c, sink, w_out, g_mlp, w_up, w_down, g_final):
    cos, sin = rope_tables(x.shape[1])
    for i in range(DEPTH):
        x = hybrid_layer(x, cos, sin, g_mix[i], w_in[i], conv_w[i], conv_b[i], dt_bias[i], a_log[i],
                         d_skip[i], ssd_norm[i], w_a[i], w_b[i], w_c[i], sink[i], w_out[i],
                         g_mlp[i], w_up[i], w_down[i])
    return rms_norm(x, g_final)
```

```python
import functools

import jax
import jax.numpy as jnp
from jax import lax
from jax.experimental import pallas as pl
from jax.experimental.pallas import tpu as pltpu

F32 = jnp.float32
BF16 = jnp.bfloat16

D_MODEL = 2048
DEPTH = 2
SSD_HEADS = 32
SSD_HEAD_DIM = 64
SSD_INNER = SSD_HEADS * SSD_HEAD_DIM
SSD_GROUPS = 8
SSD_STATE = 128
SSD_CHUNK = 128
HEADS_PER_GROUP = SSD_HEADS // SSD_GROUPS
GROUP_WIDTH = HEADS_PER_GROUP * SSD_HEAD_DIM
CONV_WIDTH = 5
CONV_PAD = (CONV_WIDTH - 1) // 2
XBC_WIDTH = SSD_INNER + 2 * SSD_GROUPS * SSD_STATE
HEAD_DIM = 128
ROPE_DIM = HEAD_DIM // 4
ROPE_HALF = ROPE_DIM // 2
ROPE_THETA = 500000.0
DIL_PATTERNS = ((128, 1), (512, 4), (2048, 16))
DIL_GROUPS = len(DIL_PATTERNS)
DIL_HEADS = 8
DIL_WIDTH = DIL_HEADS * HEAD_DIM
DIL_HALF = 64
WIN_Q_HEADS = 16
WIN_KV_HEADS = 4
WIN_REP = WIN_Q_HEADS // WIN_KV_HEADS
WIN_HALF = 128
D_FF = 4 * D_MODEL
N_BRANCH = 3
EPS = 1e-6
NEG_INF = -1e30

OFF_Z = 0
OFF_XBC = OFF_Z + SSD_INNER
OFF_QKVD = OFF_XBC + XBC_WIDTH
OFF_QW = OFF_QKVD + 3 * DIL_GROUPS * DIL_WIDTH
OFF_KW = OFF_QW + WIN_Q_HEADS * HEAD_DIM
OFF_VW = OFF_KW + WIN_KV_HEADS * HEAD_DIM
OFF_GATE = OFF_VW + WIN_KV_HEADS * HEAD_DIM
N_MAIN = OFF_GATE + N_BRANCH * D_MODEL
DT_COLS = 2 * SSD_HEADS
DT_IN_OFF = SSD_INNER + XBC_WIDTH
DT_LANES = 128

LANE = 128
SUBLANE = 8
VMEM_LIMIT = 56 * 1024 * 1024


def _split3(x):
    hi = x.astype(BF16)
    r1 = x - hi.astype(F32)
    mid = r1.astype(BF16)
    r2 = r1 - mid.astype(F32)
    return hi, mid, r2.astype(BF16)


def _dot(a, b):
    return jnp.dot(a, b, preferred_element_type=F32)


def _dot_nt(a, b):
    return lax.dot_general(a, b, (((1,), (1,)), ((), ())), preferred_element_type=F32)


def _exact_left(mat, x):
    hi, mid, lo = _split3(x)
    return _dot(mat, hi) + _dot(mat, mid) + _dot(mat, lo)


def _exact_right(x, mat):
    hi, mid, lo = _split3(x)
    return _dot(hi, mat) + _dot(mid, mat) + _dot(lo, mat)


def _softplus(x):
    return jnp.maximum(x, 0.0) + jnp.log1p(jnp.exp(-jnp.abs(x)))


def _norm_rows(x, g):
    ms = jnp.mean(x * x, axis=-1, keepdims=True)
    return x * lax.rsqrt(ms + EPS) * g


def _rope_tile(a, ct, sa, sb):
    return a * ct + pltpu.roll(a, ROPE_HALF, 1) * sa + pltpu.roll(a, HEAD_DIM - ROPE_HALF, 1) * sb


def _proj_kernel(x_ref, g_ref, w_ref, wdt_ref, wdtt_ref, bdt_ref, bdtt_ref, ct_ref, sa_ref, sb_ref,
                 o_ref, dt_ref, dtt_ref, h_ref, *, tn, rope_lo, rope_hi, gate_lo):
    j = pl.program_id(1)

    @pl.when(j == 0)
    def _():
        hb = _norm_rows(x_ref[...], g_ref[...]).astype(BF16)
        h_ref[...] = hb
        dt_ref[...] = _softplus(_dot(hb, wdt_ref[...]) + bdt_ref[...])
        dtt_ref[...] = _softplus(_dot_nt(wdtt_ref[...], hb) + bdtt_ref[...])

    acc = _dot(h_ref[...], w_ref[...])
    is_rope = jnp.logical_and(j >= rope_lo, j < rope_hi)
    is_gate = j >= gate_lo

    @pl.when(is_rope)
    def _():
        ct, sa, sb = ct_ref[...], sa_ref[...], sb_ref[...]
        for c in range(tn // HEAD_DIM):
            sl = slice(c * HEAD_DIM, (c + 1) * HEAD_DIM)
            o_ref[:, sl] = _rope_tile(acc[:, sl], ct, sa, sb).astype(o_ref.dtype)

    @pl.when(is_gate)
    def _():
        o_ref[...] = (1.0 / (1.0 + jnp.exp(-acc))).astype(o_ref.dtype)

    @pl.when(jnp.logical_not(jnp.logical_or(is_rope, is_gate)))
    def _():
        o_ref[...] = acc.astype(o_ref.dtype)


def _proj(x2d, g, w_main, w_dt, w_dtt, b_dt, b_dtt, ct, sa, sb, seq, *, tm, tn):
    m = x2d.shape[0]
    assert seq % tm == 0 and m % tm == 0 and N_MAIN % tn == 0
    assert OFF_QKVD % tn == 0 and OFF_VW % tn == 0 and OFF_GATE % tn == 0
    seq_tiles = seq // tm
    kern = functools.partial(_proj_kernel, tn=tn, rope_lo=OFF_QKVD // tn, rope_hi=OFF_VW // tn,
                             gate_lo=OFF_GATE // tn)
    return pl.pallas_call(
        kern,
        grid=(m // tm, N_MAIN // tn),
        in_specs=[
            pl.BlockSpec((tm, D_MODEL), lambda i, j: (i, 0)),
            pl.BlockSpec((1, D_MODEL), lambda i, j: (0, 0)),
            pl.BlockSpec((D_MODEL, tn), lambda i, j: (0, j)),
            pl.BlockSpec((D_MODEL, DT_LANES), lambda i, j: (0, 0)),
            pl.BlockSpec((DT_LANES, D_MODEL), lambda i, j: (0, 0)),
            pl.BlockSpec((1, DT_LANES), lambda i, j: (0, 0)),
            pl.BlockSpec((DT_LANES, 1), lambda i, j: (0, 0)),
            pl.BlockSpec((tm, HEAD_DIM), lambda i, j: (i % seq_tiles, 0)),
            pl.BlockSpec((tm, HEAD_DIM), lambda i, j: (i % seq_tiles, 0)),
            pl.BlockSpec((tm, HEAD_DIM), lambda i, j: (i % seq_tiles, 0)),
        ],
        out_specs=[
            pl.BlockSpec((tm, tn), lambda i, j: (i, j)),
            pl.BlockSpec((tm, DT_LANES), lambda i, j: (i, 0)),
            pl.BlockSpec((DT_LANES, tm), lambda i, j: (0, i)),
        ],
        out_shape=[
            jax.ShapeDtypeStruct((m, N_MAIN), BF16),
            jax.ShapeDtypeStruct((m, DT_LANES), F32),
            jax.ShapeDtypeStruct((DT_LANES, m), F32),
        ],
        scratch_shapes=[pltpu.VMEM((tm, D_MODEL), BF16)],
        compiler_params=pltpu.CompilerParams(
            dimension_semantics=("parallel", "arbitrary"), vmem_limit_bytes=VMEM_LIMIT),
        name="proj",
    )(x2d, g, w_main, w_dt, w_dtt, b_dt, b_dtt, ct, sa, sb)


def _up_kernel(x_ref, g_ref, w_ref, o_ref, h_ref):
    @pl.when(pl.program_id(1) == 0)
    def _():
        h_ref[...] = _norm_rows(x_ref[...], g_ref[...]).astype(BF16)

    acc = jnp.maximum(_dot(h_ref[...], w_ref[...]), 0.0)
    o_ref[...] = (acc * acc).astype(o_ref.dtype)


def _mlp_up(x2d, g, w, *, tm, tn):
    m, n = x2d.shape[0], w.shape[1]
    return pl.pallas_call(
        _up_kernel,
        grid=(m // tm, n // tn),
        in_specs=[
            pl.BlockSpec((tm, D_MODEL), lambda i, j: (i, 0)),
            pl.BlockSpec((1, D_MODEL), lambda i, j: (0, 0)),
            pl.BlockSpec((D_MODEL, tn), lambda i, j: (0, j)),
        ],
        out_specs=pl.BlockSpec((tm, tn), lambda i, j: (i, j)),
        out_shape=jax.ShapeDtypeStruct((m, n), BF16),
        scratch_shapes=[pltpu.VMEM((tm, D_MODEL), BF16)],
        compiler_params=pltpu.CompilerParams(
            dimension_semantics=("parallel", "arbitrary"), vmem_limit_bytes=VMEM_LIMIT),
        name="mlp_up",
    )(x2d, g, w)


def _matmul_res_kernel(a_ref, w_ref, r_ref, o_ref, acc_ref):
    k = pl.program_id(2)

    @pl.when(k == 0)
    def _():
        acc_ref[...] = jnp.zeros_like(acc_ref)

    acc_ref[...] += _dot(a_ref[...], w_ref[...])

    @pl.when(k == pl.num_programs(2) - 1)
    def _():
        o_ref[...] = r_ref[...] + acc_ref[...]


def _matmul_res(a, w, res, *, tm, tn, tk):
    m, kdim = a.shape
    n = w.shape[1]
    return pl.pallas_call(
        _matmul_res_kernel,
        grid=(m // tm, n // tn, kdim // tk),
        in_specs=[
            pl.BlockSpec((tm, tk), lambda i, j, k: (i, k)),
            pl.BlockSpec((tk, tn), lambda i, j, k: (k, j)),
            pl.BlockSpec((tm, tn), lambda i, j, k: (i, j)),
        ],
        out_specs=pl.BlockSpec((tm, tn), lambda i, j, k: (i, j)),
        out_shape=jax.ShapeDtypeStruct((m, n), F32),
        scratch_shapes=[pltpu.VMEM((tm, tn), F32)],
        compiler_params=pltpu.CompilerParams(
            dimension_semantics=("parallel", "parallel", "arbitrary"), vmem_limit_bytes=VMEM_LIMIT),
        name="matmul_res",
    )(a, w, res)


def _merge_kernel(ys_ref, z_ref, nw_ref, yb_ref, yc_ref, g0_ref, g1_ref, g2_ref,
                  wa_ref, wb_ref, wc_ref, o_ref, ya_ref):
    @pl.when(pl.program_id(1) == 0)
    def _():
        z = z_ref[...].astype(F32)
        y = ys_ref[...].astype(F32) * (z / (1.0 + jnp.exp(-z)))
        ya_ref[...] = _norm_rows(y, nw_ref[...]).astype(BF16)

    acc = g0_ref[...].astype(F32) * _dot(ya_ref[...], wa_ref[...])
    acc += g1_ref[...].astype(F32) * _dot(yb_ref[...], wb_ref[...])
    acc += g2_ref[...].astype(F32) * _dot(yc_ref[...], wc_ref[...])
    o_ref[...] = acc.astype(o_ref.dtype)


def _merge(y_ssd, proj, norm_w, y_b, y_c, w_a, w_b, w_c, *, tm, tn):
    m = y_ssd.shape[0]
    gate_blk = OFF_GATE // tn
    per_gate = D_MODEL // tn
    return pl.pallas_call(
        _merge_kernel,
        grid=(m // tm, D_MODEL // tn),
        in_specs=[
            pl.BlockSpec((tm, SSD_INNER), lambda i, j: (i, 0)),
            pl.BlockSpec((tm, SSD_INNER), lambda i, j: (i, OFF_Z // SSD_INNER)),
            pl.BlockSpec((1, SSD_INNER), lambda i, j: (0, 0)),
            pl.BlockSpec((tm, DIL_WIDTH), lambda i, j: (i, 0)),
            pl.BlockSpec((tm, WIN_Q_HEADS * HEAD_DIM), lambda i, j: (i, 0)),
            pl.BlockSpec((tm, tn), lambda i, j: (i, gate_blk + j)),
            pl.BlockSpec((tm, tn), lambda i, j: (i, gate_blk + per_gate + j)),
            pl.BlockSpec((tm, tn), lambda i, j: (i, gate_blk + 2 * per_gate + j)),
            pl.BlockSpec((SSD_INNER, tn), lambda i, j: (0, j)),
            pl.BlockSpec((DIL_WIDTH, tn), lambda i, j: (0, j)),
            pl.BlockSpec((WIN_Q_HEADS * HEAD_DIM, tn), lambda i, j: (0, j)),
        ],
        out_specs=pl.BlockSpec((tm, tn), lambda i, j: (i, j)),
        out_shape=jax.ShapeDtypeStruct((m, D_MODEL), BF16),
        scratch_shapes=[pltpu.VMEM((tm, SSD_INNER), BF16)],
        compiler_params=pltpu.CompilerParams(
            dimension_semantics=("parallel", "arbitrary"), vmem_limit_bytes=VMEM_LIMIT),
        name="merge",
    )(y_ssd, proj, norm_w, y_b, y_c, proj, proj, proj, w_a, w_b, w_c)


def _final_norm_kernel(x_ref, g_ref, o_ref):
    o_ref[...] = _norm_rows(x_ref[...], g_ref[...])


def _final_norm(x2d, g, *, tm):
    m = x2d.shape[0]
    return pl.pallas_call(
        _final_norm_kernel,
        grid=(m // tm,),
        in_specs=[pl.BlockSpec((tm, D_MODEL), lambda i: (i, 0)),
                  pl.BlockSpec((1, D_MODEL), lambda i: (0, 0))],
        out_specs=pl.BlockSpec((tm, D_MODEL), lambda i: (i, 0)),
        out_shape=jax.ShapeDtypeStruct((m, D_MODEL), F32),
        compiler_params=pltpu.CompilerParams(dimension_semantics=("parallel",)),
        name="final_norm",
    )(x2d, g)


def _ssd_kernel(xs_ref, bm_ref, cm_ref, cwx_ref, cwb_ref, cwc_ref, cbx_ref, cbb_ref, cbc_ref,
                dt_ref, dtt_ref, alt_ref, pcol_ref, p64_ref,
                y_ref,
                padx_ref, padb_ref, padc_ref, xs_s, bm_s, cm_s, bmt_s, dtr_s, y_s, st_s, sel_s, tri_s,
                *, seq, conv_rows):
    t = SSD_CHUNK
    n_chunks = seq // t
    g = pl.program_id(1)
    for c in range(n_chunks):
        dtr_s[c] = dtt_ref[:, c * t:(c + 1) * t]

    halo = SUBLANE
    for pad_ref, src_ref in ((padx_ref, xs_ref), (padb_ref, bm_ref), (padc_ref, cm_ref)):
        width = pad_ref.shape[1]
        pad_ref[0:halo, :] = jnp.zeros((halo, width), F32)
        pad_ref[halo + seq:2 * halo + seq, :] = jnp.zeros((halo, width), F32)
        pad_ref[halo:halo + seq, :] = src_ref[0].astype(F32)

    def conv_silu(pad_ref, w_ref, b_ref, r0):
        acc = b_ref[...] + w_ref[0:1, :] * pad_ref[halo - CONV_PAD + r0:halo - CONV_PAD + r0 + conv_rows, :]
        for k in range(1, CONV_WIDTH):
            lo = halo - CONV_PAD + k + r0
            acc = acc + w_ref[k:k + 1, :] * pad_ref[lo:lo + conv_rows, :]
        return acc / (1.0 + jnp.exp(-acc))

    dskip = p64_ref[0, 0:1, :]
    for r in range(seq // conv_rows):
        r0 = r * conv_rows
        rows = slice(r0, r0 + conv_rows)
        xv = conv_silu(padx_ref, cwx_ref, cbx_ref, r0)
        xs_s[rows, :] = xv.astype(BF16)
        y_s[rows, :] = xv * dskip
        bv = conv_silu(padb_ref, cwb_ref, cbb_ref, r0)
        bm_s[rows, :] = bv.astype(BF16)
        for cc in range(conv_rows // t):
            bmt_s[r0 // t + cc] = bv[cc * t:(cc + 1) * t, :].T.astype(BF16)
        cm_s[rows, :] = conv_silu(padc_ref, cwc_ref, cbc_ref, r0).astype(BF16)

    row_i = lax.broadcasted_iota(jnp.int32, (DT_LANES, HEADS_PER_GROUP * LANE), 0)
    col_h = lax.broadcasted_iota(jnp.int32, (DT_LANES, HEADS_PER_GROUP * LANE), 1) // LANE
    base = g * (2 * HEADS_PER_GROUP)
    for d in range(2):
        sel_s[d] = (row_i == base + d * HEADS_PER_GROUP + col_h).astype(BF16)
    ri = lax.broadcasted_iota(jnp.int32, (t, t), 0)
    ci = lax.broadcasted_iota(jnp.int32, (t, t), 1)
    tri_s[0] = (ri >= ci).astype(BF16)
    tri_s[1] = (ri <= ci).astype(BF16)
    st_s[...] = jnp.zeros_like(st_s)

    a_rows = -jnp.exp(alt_ref[...])
    head_of_lane = lax.broadcasted_iota(jnp.int32, (1, GROUP_WIDTH), 1) // SSD_HEAD_DIM

    def one_direction(d, c):
        off = pl.multiple_of(c * t, t)
        rows = pl.ds(off, t)
        lower, upper = tri_s[0], tri_s[1]
        col_tri, row_tri = (lower, upper) if d == 0 else (upper, lower)
        valid = (ri >= ci) if d == 0 else (ri <= ci)
        end = t - 1 if d == 0 else 0

        cum_cols = _exact_left(col_tri, dt_ref[0, rows, :])
        a_cols = -jnp.exp(pcol_ref[0, d:d + 1, :])
        cs_col = _exact_right(cum_cols, sel_s[d]) * a_cols
        dt_rows = dtr_s[c]
        cs_rows = _exact_right(dt_rows, row_tri) * a_rows

        xs_c = xs_s[rows, :]
        cm_c = cm_s[rows, :]
        cb = _dot_nt(cm_c, bm_s[rows, :])
        bmt_c = bmt_s[c].astype(F32)
        st = st_s[d]
        st_b = st.astype(BF16)

        y = jnp.zeros((t, GROUP_WIDTH), F32)
        new_st = jnp.zeros((SSD_STATE, GROUP_WIDTH), F32)
        scale_row = jnp.zeros((1, GROUP_WIDTH), F32)
        for h in range(HEADS_PER_GROUP):
            r = d * HEADS_PER_GROUP + h
            col = cs_col[:, h * LANE:(h + 1) * LANE]
            row = cs_rows[r:r + 1, :]
            dt_row = dt_rows[r:r + 1, :]
            decay = jnp.exp(jnp.where(valid, col - row, NEG_INF))
            l_mat = (cb * decay * dt_row).astype(BF16)
            c_in = (cm_c.astype(F32) * jnp.exp(col)).astype(BF16)
            head = head_of_lane == h
            xs_h = jnp.where(head, xs_c, jnp.zeros_like(xs_c))
            st_h = jnp.where(head, st_b, jnp.zeros_like(st_b))
            y = y + _dot(jnp.concatenate([l_mat, c_in], axis=1),
                         jnp.concatenate([xs_h, st_h], axis=0))
            tot = row[:, end:end + 1]
            w_row = dt_row * jnp.exp(tot - row)
            new_st = new_st + _dot((bmt_c * w_row).astype(BF16), xs_h)
            scale_row = scale_row + jnp.where(head, jnp.exp(tot), 0.0)
        y_s[rows, :] += y
        st_s[d] = st * scale_row + new_st

    def body(i, carry):
        one_direction(0, i)
        one_direction(1, n_chunks - 1 - i)
        return carry

    lax.fori_loop(0, n_chunks, body, 0)
    y_ref[0] = y_s[...].astype(y_ref.dtype)


def _ssd(proj3, conv_w, conv_b, dt, dtt, alog_col, pcol, p64, *, conv_rows=256):
    b, seq, _ = proj3.shape
    gw, ns = GROUP_WIDTH, SSD_STATE
    xs_blk = OFF_XBC // gw
    bm_blk = (OFF_XBC + SSD_INNER) // ns
    cm_blk = (OFF_XBC + SSD_INNER + SSD_GROUPS * ns) // ns
    cw_b_blk = SSD_INNER // ns
    cw_c_blk = (SSD_INNER + SSD_GROUPS * ns) // ns
    kern = functools.partial(_ssd_kernel, seq=seq, conv_rows=conv_rows)
    return pl.pallas_call(
        kern,
        grid=(b, SSD_GROUPS),
        in_specs=[
            pl.BlockSpec((1, seq, gw), lambda i, g: (i, 0, xs_blk + g)),
            pl.BlockSpec((1, seq, ns), lambda i, g: (i, 0, bm_blk + g)),
            pl.BlockSpec((1, seq, ns), lambda i, g: (i, 0, cm_blk + g)),
            pl.BlockSpec((CONV_WIDTH, gw), lambda i, g: (0, g)),
            pl.BlockSpec((CONV_WIDTH, ns), lambda i, g: (0, cw_b_blk + g)),
            pl.BlockSpec((CONV_WIDTH, ns), lambda i, g: (0, cw_c_blk + g)),
            pl.BlockSpec((1, gw), lambda i, g: (0, g)),
            pl.BlockSpec((1, ns), lambda i, g: (0, cw_b_blk + g)),
            pl.BlockSpec((1, ns), lambda i, g: (0, cw_c_blk + g)),
            pl.BlockSpec((1, seq, DT_LANES), lambda i, g: (i, 0, 0)),
            pl.BlockSpec((2 * HEADS_PER_GROUP, seq), lambda i, g: (g, i)),
            pl.BlockSpec((2 * HEADS_PER_GROUP, 1), lambda i, g: (g, 0)),
            pl.BlockSpec((1, SUBLANE, HEADS_PER_GROUP * LANE), lambda i, g: (g, 0, 0)),
            pl.BlockSpec((1, SUBLANE, gw), lambda i, g: (g, 0, 0)),
        ],
        out_specs=pl.BlockSpec((1, seq, gw), lambda i, g: (i, 0, g)),
        out_shape=jax.ShapeDtypeStruct((b, seq, SSD_INNER), BF16),
        scratch_shapes=[
            pltpu.VMEM((seq + 2 * SUBLANE, gw), F32),
            pltpu.VMEM((seq + 2 * SUBLANE, ns), F32),
            pltpu.VMEM((seq + 2 * SUBLANE, ns), F32),
            pltpu.VMEM((seq, gw), BF16),
            pltpu.VMEM((seq, ns), BF16),
            pltpu.VMEM((seq, ns), BF16),
            pltpu.VMEM((seq // SSD_CHUNK, ns, SSD_CHUNK), BF16),
            pltpu.VMEM((seq // SSD_CHUNK, 2 * HEADS_PER_GROUP, SSD_CHUNK), F32),
            pltpu.VMEM((seq, gw), F32),
            pltpu.VMEM((2, ns, gw), F32),
            pltpu.VMEM((2, DT_LANES, HEADS_PER_GROUP * LANE), BF16),
            pltpu.VMEM((2, SSD_CHUNK, SSD_CHUNK), BF16),
        ],
        compiler_params=pltpu.CompilerParams(
            dimension_semantics=("parallel", "arbitrary"), vmem_limit_bytes=VMEM_LIMIT),
        name="ssd",
    )(proj3, proj3, proj3, conv_w, conv_w, conv_w, conv_b, conv_b, conv_b, dt, dtt, alog_col, pcol, p64)


def _win_kernel(sink_ref, q_ref, k_ref, v_ref, o_ref, base_s, *, seq):
    blk = WIN_HALF
    kwin = 3 * blk
    rows = WIN_REP * blk
    kv = pl.program_id(1)
    scale = HEAD_DIM ** -0.5

    ri = lax.broadcasted_iota(jnp.int32, (rows, kwin), 0) % blk
    ci = lax.broadcasted_iota(jnp.int32, (rows, kwin), 1)
    base_s[...] = ci - ri
    sink_col = jnp.concatenate(
        [jnp.full((blk, 1), sink_ref[kv * WIN_REP + r], F32) for r in range(WIN_REP)], axis=0)

    def body(qi, carry):
        q0 = pl.multiple_of(qi * blk, blk)
        k0 = pl.multiple_of(jnp.clip(q0 - blk, 0, seq - kwin), blk)
        qt = q_ref[0, pl.ds(q0, blk), :]
        qs = jnp.concatenate([qt[:, r * HEAD_DIM:(r + 1) * HEAD_DIM] for r in range(WIN_REP)], axis=0)
        s = _dot_nt(qs, k_ref[0, pl.ds(k0, kwin), :]) * scale
        delta = base_s[...] + (k0 - q0)
        s = jnp.where(jnp.abs(delta) <= WIN_HALF, s, NEG_INF)
        m = jnp.maximum(jnp.max(s, axis=-1, keepdims=True), sink_col)
        e = jnp.exp(s - m)
        den = jnp.sum(e, axis=-1, keepdims=True) + jnp.exp(sink_col - m)
        o = _dot(e.astype(BF16), v_ref[0, pl.ds(k0, kwin), :]) / den
        for r in range(WIN_REP):
            o_ref[0, pl.ds(q0, blk), r * HEAD_DIM:(r + 1) * HEAD_DIM] = (
                o[r * blk:(r + 1) * blk, :].astype(o_ref.dtype))
        return carry

    lax.fori_loop(0, seq // blk, body, 0)


def _win_attn(proj3, sink):
    b, seq, _ = proj3.shape
    assert seq >= 3 * WIN_HALF and seq % WIN_HALF == 0
    qw = WIN_REP * HEAD_DIM
    kern = functools.partial(_win_kernel, seq=seq)
    return pl.pallas_call(
        kern,
        grid=(b, WIN_KV_HEADS),
        in_specs=[
            pl.BlockSpec(memory_space=pltpu.SMEM),
            pl.BlockSpec((1, seq, qw), lambda i, g: (i, 0, OFF_QW // qw + g)),
            pl.BlockSpec((1, seq, HEAD_DIM), lambda i, g: (i, 0, OFF_KW // HEAD_DIM + g)),
            pl.BlockSpec((1, seq, HEAD_DIM), lambda i, g: (i, 0, OFF_VW // HEAD_DIM + g)),
        ],
        out_specs=pl.BlockSpec((1, seq, qw), lambda i, g: (i, 0, g)),
        out_shape=jax.ShapeDtypeStruct((b, seq, WIN_Q_HEADS * HEAD_DIM), BF16),
        scratch_shapes=[pltpu.VMEM((WIN_REP * WIN_HALF, 3 * WIN_HALF), jnp.int32)],
        compiler_params=pltpu.CompilerParams(
            dimension_semantics=("parallel", "parallel"), vmem_limit_bytes=VMEM_LIMIT),
        name="win_attn",
    )(sink, proj3, proj3, proj3)


def _dil_kernel(*refs, seq):
    qkv_refs = refs[:9]
    o_ref = refs[9]
    stage_s, qd_s, kd_s, vd_s, od_s, ld_s, og_s, lg_s = refs[10:]
    tq = 2 * DIL_HALF
    scale = HEAD_DIM ** -0.5

    def deinterleave(src_ref, dst_ref, dil):
        n_sub = seq // dil
        stage_s[...] = src_ref[0].astype(F32)
        for r in range(dil):
            dst_ref[r * n_sub:(r + 1) * n_sub, :] = stage_s[pl.ds(r, n_sub, stride=dil), :].astype(BF16)

    def banded(q_at, k_at, v_at, n_sub):
        kw = min(n_sub, tq + 2 * DIL_HALF)
        ri = lax.broadcasted_iota(jnp.int32, (tq, kw), 0)
        ci = lax.broadcasted_iota(jnp.int32, (tq, kw), 1)
        base = ci - ri

        def body(ti, carry):
            q0 = pl.multiple_of(ti * tq, tq)
            seg0 = (q0 // n_sub) * n_sub
            k0 = pl.multiple_of(jnp.clip(q0 - DIL_HALF, seg0, seg0 + n_sub - kw), DIL_HALF)
            s = _dot_nt(q_at(pl.ds(q0, tq)), k_at(pl.ds(k0, kw))) * scale
            s = jnp.where(jnp.abs(base + (k0 - q0)) <= DIL_HALF, s, NEG_INF)
            m = jnp.max(s, axis=-1, keepdims=True)
            e = jnp.exp(s - m)
            den = jnp.sum(e, axis=-1, keepdims=True)
            od_s[pl.ds(q0, tq), :] = _dot(e.astype(BF16), v_at(pl.ds(k0, kw))) / den
            ld_s[pl.ds(q0, tq), :] = jnp.broadcast_to(m + jnp.log(den), (tq, HEAD_DIM))
            return carry

        lax.fori_loop(0, seq // tq, body, 0)

    for gi, (_, dil) in enumerate(DIL_PATTERNS):
        q_ref, k_ref, v_ref = qkv_refs[gi], qkv_refs[DIL_GROUPS + gi], qkv_refs[2 * DIL_GROUPS + gi]
        n_sub = seq // dil
        if dil == 1:
            banded(lambda sl: q_ref[0, sl, :], lambda sl: k_ref[0, sl, :], lambda sl: v_ref[0, sl, :], n_sub)
            og_s[gi] = od_s[...]
            lg_s[gi] = ld_s[...]
        else:
            deinterleave(q_ref, qd_s, dil)
            deinterleave(k_ref, kd_s, dil)
            deinterleave(v_ref, vd_s, dil)
            banded(lambda sl: qd_s[sl, :], lambda sl: kd_s[sl, :], lambda sl: vd_s[sl, :], n_sub)
            for r in range(dil):
                og_s[gi, pl.ds(r, n_sub, stride=dil), :] = od_s[r * n_sub:(r + 1) * n_sub, :]
                lg_s[gi, pl.ds(r, n_sub, stride=dil), :] = ld_s[r * n_sub:(r + 1) * n_sub, :]

    m = jnp.maximum(jnp.maximum(lg_s[0], lg_s[1]), lg_s[2])
    num = jnp.zeros((seq, HEAD_DIM), F32)
    den = jnp.zeros((seq, HEAD_DIM), F32)
    for gi in range(DIL_GROUPS):
        w = jnp.exp(lg_s[gi] - m)
        num = num + w * og_s[gi]
        den = den + w
    o_ref[0] = (num / den).astype(o_ref.dtype)


def _dil_attn(proj3):
    b, seq, _ = proj3.shape
    for _, dil in DIL_PATTERNS:
        assert (seq // dil) % (2 * DIL_HALF) == 0
    base_blk = OFF_QKVD // HEAD_DIM
    in_specs = []
    for part in range(3):
        for gi in range(DIL_GROUPS):
            blk0 = base_blk + (part * DIL_GROUPS + gi) * DIL_HEADS
            in_specs.append(pl.BlockSpec((1, seq, HEAD_DIM), lambda i, h, blk0=blk0: (i, 0, blk0 + h)))
    kern = functools.partial(_dil_kernel, seq=seq)
    return pl.pallas_call(
        kern,
        grid=(b, DIL_HEADS),
        in_specs=in_specs,
        out_specs=pl.BlockSpec((1, seq, HEAD_DIM), lambda i, h: (i, 0, h)),
        out_shape=jax.ShapeDtypeStruct((b, seq, DIL_WIDTH), BF16),
        scratch_shapes=[
            pltpu.VMEM((seq, HEAD_DIM), F32),
            pltpu.VMEM((seq, HEAD_DIM), BF16),
            pltpu.VMEM((seq, HEAD_DIM), BF16),
            pltpu.VMEM((seq, HEAD_DIM), BF16),
            pltpu.VMEM((seq, HEAD_DIM), F32),
            pltpu.VMEM((seq, HEAD_DIM), F32),
            pltpu.VMEM((DIL_GROUPS, seq, HEAD_DIM), F32),
            pltpu.VMEM((DIL_GROUPS, seq, HEAD_DIM), F32),
        ],
        compiler_params=pltpu.CompilerParams(
            dimension_semantics=("parallel", "parallel"), vmem_limit_bytes=VMEM_LIMIT),
        name="dil_attn",
    )(*([proj3] * 9))


def _rope_tables(seq):
    inv = ROPE_THETA ** (-jnp.arange(0, ROPE_DIM, 2, dtype=F32) / ROPE_DIM)
    ang = jnp.arange(seq, dtype=F32)[:, None] * inv[None, :]
    cos, sin = jnp.cos(ang), jnp.sin(ang)
    zeros = jnp.zeros((seq, HEAD_DIM - ROPE_DIM), F32)
    zh = jnp.zeros_like(sin)
    ct = jnp.concatenate([cos, cos, jnp.ones_like(zeros)], axis=1)
    sa = jnp.concatenate([zh, sin, zeros], axis=1)
    sb = jnp.concatenate([-sin, zh, zeros], axis=1)
    return ct, sa, sb


def _group_major(v):
    return v.reshape(2, SSD_GROUPS, HEADS_PER_GROUP).transpose(1, 0, 2).reshape(DT_COLS)


def _layer_params(w_in, dt_bias, a_log, d_skip):
    w_main = jnp.concatenate([w_in[:, :DT_IN_OFF], w_in[:, DT_IN_OFF + DT_COLS:]], axis=1).astype(BF16)
    w_dt = w_in[:, DT_IN_OFF:DT_IN_OFF + DT_COLS]
    w_dt = w_dt.reshape(D_MODEL, 2, SSD_GROUPS, HEADS_PER_GROUP).transpose(0, 2, 1, 3).reshape(D_MODEL, DT_COLS)
    w_dt = jnp.pad(w_dt, ((0, 0), (0, DT_LANES - DT_COLS))).astype(BF16)
    b_dt = jnp.pad(_group_major(dt_bias.astype(F32)), (0, DT_LANES - DT_COLS))
    alog = jnp.pad(_group_major(a_log.astype(F32)), (0, DT_LANES - DT_COLS))
    pcol = jnp.repeat(a_log.astype(F32).reshape(2, SSD_GROUPS, HEADS_PER_GROUP).transpose(1, 0, 2), LANE, axis=2)
    pcol = jnp.pad(pcol, ((0, 0), (0, SUBLANE - 2), (0, 0)))
    p64 = jnp.repeat(d_skip.astype(F32).reshape(SSD_GROUPS, 1, HEADS_PER_GROUP), SSD_HEAD_DIM, axis=2)
    p64 = jnp.pad(p64, ((0, 0), (0, SUBLANE - 1), (0, 0)))
    return (w_main, w_dt, w_dt.T, b_dt.reshape(1, DT_LANES), b_dt.reshape(DT_LANES, 1),
            alog.reshape(DT_LANES, 1), pcol, p64)


def kernel(x, g_mix, w_in, conv_w, conv_b, dt_bias, a_log, d_skip, ssd_norm, w_a, w_b, w_c, sink,
           w_out, g_mlp, w_up, w_down, g_final):
    b, seq, _ = x.shape
    m = b * seq
    tm = min(1024, seq)
    ct, sa, sb = _rope_tables(seq)
    xf = x.reshape(m, D_MODEL).astype(F32)
    for i in range(DEPTH):
        w_main, w_dt, w_dtt, b_dt, b_dtt, alog_col, pcol, p64 = _layer_params(
            w_in[i], dt_bias[i], a_log[i], d_skip[i])
        proj, dt, dtt = _proj(xf, g_mix[i].reshape(1, D_MODEL), w_main, w_dt, w_dtt, b_dt, b_dtt,
                              ct, sa, sb, seq, tm=tm, tn=512)
        proj3 = proj.reshape(b, seq, N_MAIN)
        y_ssd = _ssd(proj3, conv_w[i], conv_b[i].reshape(1, XBC_WIDTH), dt.reshape(b, seq, DT_LANES),
                     dtt, alog_col, pcol, p64)
        y_b = _dil_attn(proj3)
        y_c = _win_attn(proj3, sink[i].astype(F32))
        merged = _merge(y_ssd.reshape(m, SSD_INNER), proj, ssd_norm[i].reshape(1, SSD_INNER),
                        y_b.reshape(m, DIL_WIDTH), y_c.reshape(m, WIN_Q_HEADS * HEAD_DIM),
                        w_a[i].astype(BF16), w_b[i].astype(BF16), w_c[i].astype(BF16), tm=512, tn=512)
        xf = _matmul_res(merged, w_out[i].astype(BF16), xf, tm=tm, tn=1024, tk=D_MODEL)
        u = _mlp_up(xf, g_mlp[i].reshape(1, D_MODEL), w_up[i].astype(BF16), tm=tm, tn=1024)
        xf = _matmul_res(u, w_down[i].astype(BF16), xf, tm=tm, tn=1024, tk=2048)
    out = _final_norm(xf, g_final.reshape(1, D_MODEL), tm=min(512, seq))
    return out.reshape(b, seq, D_MODEL).astype(x.dtype)
```

```python
import functools

import jax
import jax.numpy as jnp
from jax import lax
from jax.experimental import pallas as pl
from jax.experimental.pallas import tpu as pltpu

F32 = jnp.float32
BF16 = jnp.bfloat16

D_MODEL = 2048
DEPTH = 2
SSD_HEADS = 32
SSD_HEAD_DIM = 64
SSD_INNER = SSD_HEADS * SSD_HEAD_DIM
SSD_GROUPS = 8
SSD_STATE = 128
SSD_CHUNK = 128
HEADS_PER_GROUP = SSD_HEADS // SSD_GROUPS
GROUP_WIDTH = HEADS_PER_GROUP * SSD_HEAD_DIM
CONV_WIDTH = 5
CONV_PAD = (CONV_WIDTH - 1) // 2
XBC_WIDTH = SSD_INNER + 2 * SSD_GROUPS * SSD_STATE
HEAD_DIM = 128
ROPE_DIM = HEAD_DIM // 4
ROPE_HALF = ROPE_DIM // 2
ROPE_THETA = 500000.0
DIL_PATTERNS = ((128, 1), (512, 4), (2048, 16))
DIL_GROUPS = len(DIL_PATTERNS)
DIL_HEADS = 8
DIL_WIDTH = DIL_HEADS * HEAD_DIM
DIL_HALF = 64
WIN_Q_HEADS = 16
WIN_KV_HEADS = 4
WIN_REP = WIN_Q_HEADS // WIN_KV_HEADS
WIN_HALF = 128
D_FF = 4 * D_MODEL
N_BRANCH = 3
EPS = 1e-6
NEG_INF = -1e30
LOG2E = 1.4426950408889634

OFF_Z = 0
OFF_XBC = OFF_Z + SSD_INNER
OFF_QKVD = OFF_XBC + XBC_WIDTH
OFF_QW = OFF_QKVD + 3 * DIL_GROUPS * DIL_WIDTH
OFF_KW = OFF_QW + WIN_Q_HEADS * HEAD_DIM
OFF_VW = OFF_KW + WIN_KV_HEADS * HEAD_DIM
OFF_GATE = OFF_VW + WIN_KV_HEADS * HEAD_DIM
N_MAIN = OFF_GATE + N_BRANCH * D_MODEL
DT_COLS = 2 * SSD_HEADS
DT_IN_OFF = SSD_INNER + XBC_WIDTH
DT_LANES = 128

LANE = 128
SUBLANE = 8
VMEM_LIMIT = 56 * 1024 * 1024


def _split3(x):
    hi = x.astype(BF16)
    r1 = x - hi.astype(F32)
    mid = r1.astype(BF16)
    r2 = r1 - mid.astype(F32)
    return hi, mid, r2.astype(BF16)


def _dot(a, b):
    return jnp.dot(a, b, preferred_element_type=F32)


def _dot_nt(a, b):
    return lax.dot_general(a, b, (((1,), (1,)), ((), ())), preferred_element_type=F32)


def _exact_right(x, mat):
    hi, mid, lo = _split3(x)
    return _dot(hi, mat) + _dot(mid, mat) + _dot(lo, mat)


def _softplus(x):
    return jnp.maximum(x, 0.0) + jnp.log1p(jnp.exp(-jnp.abs(x)))


def _norm_rows(x, g):
    ms = jnp.mean(x * x, axis=-1, keepdims=True)
    return x * lax.rsqrt(ms + EPS) * g


def _rope_tile(a, ct, sa, sb):
    return a * ct + pltpu.roll(a, ROPE_HALF, 1) * sa + pltpu.roll(a, HEAD_DIM - ROPE_HALF, 1) * sb


def _proj_kernel(x_ref, g_ref, w_ref, wdtt_ref, bdtt_ref, ct_ref, sa_ref, sb_ref,
                 o_ref, dtt_ref, h_ref, *, tn, tile_kinds):
    j = pl.program_id(1)

    @pl.when(j == 0)
    def _():
        hb = _norm_rows(x_ref[...], g_ref[...]).astype(BF16)
        h_ref[...] = hb
        dtt_ref[...] = _softplus(_dot_nt(wdtt_ref[...], hb) + bdtt_ref[...])

    def tile_body(kind):
        acc = _dot(h_ref[...], w_ref[...])
        if 'r' in kind:
            ct, sa, sb = ct_ref[...], sa_ref[...], sb_ref[...]
        for c, k in enumerate(kind):
            sl = slice(c * HEAD_DIM, (c + 1) * HEAD_DIM)
            a = acc[:, sl]
            if k == 'r':
                a = _rope_tile(a, ct, sa, sb)
            elif k == 'g':
                a = 1.0 / (1.0 + jnp.exp(-a))
            o_ref[:, sl] = a.astype(o_ref.dtype)

    for kind in sorted(set(tile_kinds)):
        idx = [t for t, k in enumerate(tile_kinds) if k == kind]
        runs, start = [], idx[0]
        for a, b in zip(idx, idx[1:] + [None]):
            if b != a + 1:
                runs.append((start, a))
                start = b
        cond = None
        for lo, hi in runs:
            c = jnp.logical_and(j >= lo, j <= hi)
            cond = c if cond is None else jnp.logical_or(cond, c)
        pl.when(cond)(functools.partial(tile_body, kind))


def _proj_tile_kinds(tn):
    def head_kind(col):
        if OFF_QKVD <= col < OFF_VW:
            return 'r'
        return 'g' if col >= OFF_GATE else 'p'
    return tuple(''.join(head_kind(t * tn + c * HEAD_DIM) for c in range(tn // HEAD_DIM))
                 for t in range(N_MAIN // tn))


def _proj(x2d, g, w_main, w_dtt, b_dtt, ct, sa, sb, seq, *, tm, tn):
    m = x2d.shape[0]
    assert seq % tm == 0 and m % tm == 0 and N_MAIN % tn == 0 and tn % HEAD_DIM == 0
    seq_tiles = seq // tm
    kern = functools.partial(_proj_kernel, tn=tn, tile_kinds=_proj_tile_kinds(tn))
    return pl.pallas_call(
        kern,
        grid=(m // tm, N_MAIN // tn),
        in_specs=[
            pl.BlockSpec((tm, D_MODEL), lambda i, j: (i, 0)),
            pl.BlockSpec((1, D_MODEL), lambda i, j: (0, 0)),
            pl.BlockSpec((D_MODEL, tn), lambda i, j: (0, j)),
            pl.BlockSpec((DT_LANES, D_MODEL), lambda i, j: (0, 0)),
            pl.BlockSpec((DT_LANES, 1), lambda i, j: (0, 0)),
            pl.BlockSpec((tm, HEAD_DIM), lambda i, j: (i % seq_tiles, 0)),
            pl.BlockSpec((tm, HEAD_DIM), lambda i, j: (i % seq_tiles, 0)),
            pl.BlockSpec((tm, HEAD_DIM), lambda i, j: (i % seq_tiles, 0)),
        ],
        out_specs=[
            pl.BlockSpec((tm, tn), lambda i, j: (i, j)),
            pl.BlockSpec((DT_LANES, tm), lambda i, j: (0, i)),
        ],
        out_shape=[
            jax.ShapeDtypeStruct((m, N_MAIN), BF16),
            jax.ShapeDtypeStruct((DT_LANES, m), F32),
        ],
        scratch_shapes=[pltpu.VMEM((tm, D_MODEL), BF16)],
        compiler_params=pltpu.CompilerParams(
            dimension_semantics=("parallel", "arbitrary"), vmem_limit_bytes=VMEM_LIMIT),
        name="proj",
    )(x2d, g, w_main, w_dtt, b_dtt, ct, sa, sb)


def _up_kernel(x_ref, g_ref, w_ref, o_ref, h_ref):
    @pl.when(pl.program_id(1) == 0)
    def _():
        h_ref[...] = _norm_rows(x_ref[...], g_ref[...]).astype(BF16)

    acc = jnp.maximum(_dot(h_ref[...], w_ref[...]), 0.0)
    o_ref[...] = (acc * acc).astype(o_ref.dtype)


def _mlp_up(x2d, g, w, *, tm, tn):
    m, n = x2d.shape[0], w.shape[1]
    return pl.pallas_call(
        _up_kernel,
        grid=(m // tm, n // tn),
        in_specs=[
            pl.BlockSpec((tm, D_MODEL), lambda i, j: (i, 0)),
            pl.BlockSpec((1, D_MODEL), lambda i, j: (0, 0)),
            pl.BlockSpec((D_MODEL, tn), lambda i, j: (0, j)),
        ],
        out_specs=pl.BlockSpec((tm, tn), lambda i, j: (i, j)),
        out_shape=jax.ShapeDtypeStruct((m, n), BF16),
        scratch_shapes=[pltpu.VMEM((tm, D_MODEL), BF16)],
        compiler_params=pltpu.CompilerParams(
            dimension_semantics=("parallel", "arbitrary"), vmem_limit_bytes=VMEM_LIMIT),
        name="mlp_up",
    )(x2d, g, w)


def _matmul_res_kernel(a_ref, w_ref, r_ref, o_ref, acc_ref):
    k = pl.program_id(2)

    @pl.when(k == 0)
    def _():
        acc_ref[...] = jnp.zeros_like(acc_ref)

    acc_ref[...] += _dot(a_ref[...], w_ref[...])

    @pl.when(k == pl.num_programs(2) - 1)
    def _():
        o_ref[...] = r_ref[...] + acc_ref[...]


def _matmul_res(a, w, res, *, tm, tn, tk):
    m, kdim = a.shape
    n = w.shape[1]
    return pl.pallas_call(
        _matmul_res_kernel,
        grid=(m // tm, n // tn, kdim // tk),
        in_specs=[
            pl.BlockSpec((tm, tk), lambda i, j, k: (i, k)),
            pl.BlockSpec((tk, tn), lambda i, j, k: (k, j)),
            pl.BlockSpec((tm, tn), lambda i, j, k: (i, j)),
        ],
        out_specs=pl.BlockSpec((tm, tn), lambda i, j, k: (i, j)),
        out_shape=jax.ShapeDtypeStruct((m, n), F32),
        scratch_shapes=[pltpu.VMEM((tm, tn), F32)],
        compiler_params=pltpu.CompilerParams(
            dimension_semantics=("parallel", "parallel", "arbitrary"), vmem_limit_bytes=VMEM_LIMIT),
        name="matmul_res",
    )(a, w, res)


def _merge_kernel(ys_ref, z_ref, nw_ref, yb_ref, yc_ref, g0_ref, g1_ref, g2_ref,
                  wa_ref, wb_ref, wc_ref, o_ref, ya_ref):
    @pl.when(pl.program_id(1) == 0)
    def _():
        z = z_ref[...].astype(F32)
        y = ys_ref[...].astype(F32) * (z / (1.0 + jnp.exp(-z)))
        ya_ref[...] = _norm_rows(y, nw_ref[...]).astype(BF16)

    acc = g0_ref[...].astype(F32) * _dot(ya_ref[...], wa_ref[...])
    acc += g1_ref[...].astype(F32) * _dot(yb_ref[...], wb_ref[...])
    acc += g2_ref[...].astype(F32) * _dot(yc_ref[...], wc_ref[...])
    o_ref[...] = acc.astype(o_ref.dtype)


def _merge(y_ssd, proj, norm_w, y_b, y_c, w_a, w_b, w_c, *, tm, tn):
    m = y_ssd.shape[0]
    gate_blk = OFF_GATE // tn
    per_gate = D_MODEL // tn
    return pl.pallas_call(
        _merge_kernel,
        grid=(m // tm, D_MODEL // tn),
        in_specs=[
            pl.BlockSpec((tm, SSD_INNER), lambda i, j: (i, 0)),
            pl.BlockSpec((tm, SSD_INNER), lambda i, j: (i, OFF_Z // SSD_INNER)),
            pl.BlockSpec((1, SSD_INNER), lambda i, j: (0, 0)),
            pl.BlockSpec((tm, DIL_WIDTH), lambda i, j: (i, 0)),
            pl.BlockSpec((tm, WIN_Q_HEADS * HEAD_DIM), lambda i, j: (i, 0)),
            pl.BlockSpec((tm, tn), lambda i, j: (i, gate_blk + j)),
            pl.BlockSpec((tm, tn), lambda i, j: (i, gate_blk + per_gate + j)),
            pl.BlockSpec((tm, tn), lambda i, j: (i, gate_blk + 2 * per_gate + j)),
            pl.BlockSpec((SSD_INNER, tn), lambda i, j: (0, j)),
            pl.BlockSpec((DIL_WIDTH, tn), lambda i, j: (0, j)),
            pl.BlockSpec((WIN_Q_HEADS * HEAD_DIM, tn), lambda i, j: (0, j)),
        ],
        out_specs=pl.BlockSpec((tm, tn), lambda i, j: (i, j)),
        out_shape=jax.ShapeDtypeStruct((m, D_MODEL), BF16),
        scratch_shapes=[pltpu.VMEM((tm, SSD_INNER), BF16)],
        compiler_params=pltpu.CompilerParams(
            dimension_semantics=("parallel", "arbitrary"), vmem_limit_bytes=VMEM_LIMIT),
        name="merge",
    )(y_ssd, proj, norm_w, y_b, y_c, proj, proj, proj, w_a, w_b, w_c)


def _final_norm_kernel(x_ref, g_ref, o_ref):
    o_ref[...] = _norm_rows(x_ref[...], g_ref[...])


def _final_norm(x2d, g, *, tm):
    m = x2d.shape[0]
    return pl.pallas_call(
        _final_norm_kernel,
        grid=(m // tm,),
        in_specs=[pl.BlockSpec((tm, D_MODEL), lambda i: (i, 0)),
                  pl.BlockSpec((1, D_MODEL), lambda i: (0, 0))],
        out_specs=pl.BlockSpec((tm, D_MODEL), lambda i: (i, 0)),
        out_shape=jax.ShapeDtypeStruct((m, D_MODEL), F32),
        compiler_params=pltpu.CompilerParams(dimension_semantics=("parallel",)),
        name="final_norm",
    )(x2d, g)


def _ssd_kernel(xs_ref, bm_ref, cm_ref, cwx_ref, cwb_ref, cwc_ref, cbx_ref, cbb_ref, cbc_ref,
                dtt_ref, alt_ref, p64_ref,
                y_ref,
                padx_ref, padb_ref, padc_ref, xs_s, bm_s, cm_s, bmt_s, dtr_s, y_s, st_s, tri_s,
                *, seq):
    t = SSD_CHUNK
    n_chunks = seq // t
    for c in range(n_chunks):
        dtr_s[c] = dtt_ref[:, c * t:(c + 1) * t]

    halo = SUBLANE
    for pad_ref, src_ref in ((padx_ref, xs_ref), (padb_ref, bm_ref), (padc_ref, cm_ref)):
        width = pad_ref.shape[1]
        pad_ref[0:halo, :] = jnp.zeros((halo, width), F32)
        pad_ref[halo + seq:2 * halo + seq, :] = jnp.zeros((halo, width), F32)
        pad_ref[halo:halo + seq, :] = src_ref[0].astype(F32)

    def conv_silu(pad_ref, w_ref, b_ref, r0, lanes):
        acc = b_ref[:, lanes]
        for k in range(CONV_WIDTH):
            lo = halo - CONV_PAD + k + r0
            acc = acc + w_ref[k:k + 1, lanes] * pad_ref[lo:lo + t, lanes]
        return acc / (1.0 + jnp.exp(-acc))

    for c in range(n_chunks):
        r0 = c * t
        rows = slice(r0, r0 + t)
        for half in range(GROUP_WIDTH // LANE):
            lanes = slice(half * LANE, (half + 1) * LANE)
            xv = conv_silu(padx_ref, cwx_ref, cbx_ref, r0, lanes)
            xs_s[rows, lanes] = xv.astype(BF16)
            y_s[rows, lanes] = xv * p64_ref[0, 0:1, lanes]
        bv = conv_silu(padb_ref, cwb_ref, cbb_ref, r0, slice(0, LANE))
        bm_s[rows, :] = bv.astype(BF16)
        bmt_s[c] = bv.T.astype(BF16)
        cm_s[rows, :] = conv_silu(padc_ref, cwc_ref, cbc_ref, r0, slice(0, LANE)).astype(BF16)

    ri = lax.broadcasted_iota(jnp.int32, (t, t), 0)
    ci = lax.broadcasted_iota(jnp.int32, (t, t), 1)
    tri_s[0] = (ri >= ci).astype(BF16)
    tri_s[1] = (ri <= ci).astype(BF16)
    st_s[...] = jnp.zeros_like(st_s)

    a_rows = -jnp.exp(alt_ref[...])
    head_of_lane = lax.broadcasted_iota(jnp.int32, (1, GROUP_WIDTH), 1) // SSD_HEAD_DIM
    low_half = lax.broadcasted_iota(jnp.int32, (1, LANE), 1) < SSD_HEAD_DIM

    def lanes_from_row(row):
        return jnp.broadcast_to(row, (t, t)).T

    def one_direction(d, c):
        rows = pl.ds(pl.multiple_of(c * t, t), t)
        hs = slice(d * HEADS_PER_GROUP, (d + 1) * HEADS_PER_GROUP)
        row_tri = tri_s[1] if d == 0 else tri_s[0]
        valid = (ri >= ci) if d == 0 else (ri <= ci)
        end = t - 1 if d == 0 else 0

        dt_all = dtr_s[c]
        dt_rows = dt_all[hs, :]
        cs_rows = (_exact_right(dt_all, row_tri) * a_rows)[hs, :]
        tot = cs_rows[:, end:end + 1]
        w_rows = dt_rows * jnp.exp(tot - cs_rows)
        grow = jnp.exp(tot)

        xs_c = xs_s[rows, :]
        cm_c = cm_s[rows, :]
        cm_f = cm_c.astype(F32)
        cb = _dot_nt(cm_c, bm_s[rows, :])
        st = st_s[d]
        st_b = st.astype(BF16)

        y = jnp.zeros((t, GROUP_WIDTH), F32)
        scale_row = jnp.zeros((1, GROUP_WIDTH), F32)
        w_cols = []
        for h in range(HEADS_PER_GROUP):
            row = cs_rows[h:h + 1, :]
            col = lanes_from_row(row)
            decay = jnp.exp(jnp.where(valid, col - row, NEG_INF))
            l_mat = (cb * decay * dt_rows[h:h + 1, :]).astype(BF16)
            c_in = (cm_f * jnp.exp(col)).astype(BF16)
            head = head_of_lane == h
            xs_h = jnp.where(head, xs_c, jnp.zeros_like(xs_c))
            st_h = jnp.where(head, st_b, jnp.zeros_like(st_b))
            y = y + _dot(jnp.concatenate([l_mat, c_in], axis=1),
                         jnp.concatenate([xs_h, st_h], axis=0))
            w_cols.append(lanes_from_row(w_rows[h:h + 1, :]))
            scale_row = scale_row + jnp.where(head, grow[h:h + 1, :], 0.0)
        w_exp = jnp.concatenate([jnp.where(low_half, w_cols[0], w_cols[1]),
                                 jnp.where(low_half, w_cols[2], w_cols[3])], axis=1)
        xw = (xs_c.astype(F32) * w_exp).astype(BF16)
        y_s[rows, :] += y
        st_s[d] = st * scale_row + _dot(bmt_s[c], xw)

    def body(i, carry):
        one_direction(0, i)
        one_direction(1, n_chunks - 1 - i)
        return carry

    lax.fori_loop(0, n_chunks, body, 0, unroll=2)
    y_ref[0] = y_s[...].astype(y_ref.dtype)


def _ssd(proj3, conv_w, conv_b, dtt, alog_col, p64):
    b, seq, _ = proj3.shape
    gw, ns = GROUP_WIDTH, SSD_STATE
    xs_blk = OFF_XBC // gw
    bm_blk = (OFF_XBC + SSD_INNER) // ns
    cm_blk = (OFF_XBC + SSD_INNER + SSD_GROUPS * ns) // ns
    cw_b_blk = SSD_INNER // ns
    cw_c_blk = (SSD_INNER + SSD_GROUPS * ns) // ns
    kern = functools.partial(_ssd_kernel, seq=seq)
    return pl.pallas_call(
        kern,
        grid=(b, SSD_GROUPS),
        in_specs=[
            pl.BlockSpec((1, seq, gw), lambda i, g: (i, 0, xs_blk + g)),
            pl.BlockSpec((1, seq, ns), lambda i, g: (i, 0, bm_blk + g)),
            pl.BlockSpec((1, seq, ns), lambda i, g: (i, 0, cm_blk + g)),
            pl.BlockSpec((CONV_WIDTH, gw), lambda i, g: (0, g)),
            pl.BlockSpec((CONV_WIDTH, ns), lambda i, g: (0, cw_b_blk + g)),
            pl.BlockSpec((CONV_WIDTH, ns), lambda i, g: (0, cw_c_blk + g)),
            pl.BlockSpec((1, gw), lambda i, g: (0, g)),
            pl.BlockSpec((1, ns), lambda i, g: (0, cw_b_blk + g)),
            pl.BlockSpec((1, ns), lambda i, g: (0, cw_c_blk + g)),
            pl.BlockSpec((2 * HEADS_PER_GROUP, seq), lambda i, g: (g, i)),
            pl.BlockSpec((2 * HEADS_PER_GROUP, 1), lambda i, g: (g, 0)),
            pl.BlockSpec((1, SUBLANE, gw), lambda i, g: (g, 0, 0)),
        ],
        out_specs=pl.BlockSpec((1, seq, gw), lambda i, g: (i, 0, g)),
        out_shape=jax.ShapeDtypeStruct((b, seq, SSD_INNER), BF16),
        scratch_shapes=[
            pltpu.VMEM((seq + 2 * SUBLANE, gw), F32),
            pltpu.VMEM((seq + 2 * SUBLANE, ns), F32),
            pltpu.VMEM((seq + 2 * SUBLANE, ns), F32),
            pltpu.VMEM((seq, gw), BF16),
            pltpu.VMEM((seq, ns), BF16),
            pltpu.VMEM((seq, ns), BF16),
            pltpu.VMEM((seq // SSD_CHUNK, ns, SSD_CHUNK), BF16),
            pltpu.VMEM((seq // SSD_CHUNK, 2 * HEADS_PER_GROUP, SSD_CHUNK), F32),
            pltpu.VMEM((seq, gw), F32),
            pltpu.VMEM((2, ns, gw), F32),
            pltpu.VMEM((2, SSD_CHUNK, SSD_CHUNK), BF16),
        ],
        compiler_params=pltpu.CompilerParams(
            dimension_semantics=("parallel", "arbitrary"), vmem_limit_bytes=VMEM_LIMIT),
        name="ssd",
    )(proj3, proj3, proj3, conv_w, conv_w, conv_w, conv_b, conv_b, conv_b, dtt, alog_col, p64)


def _win_kernel(sink_ref, q_ref, k_ref, v_ref, o_ref, bias_s, vext_s, *, seq):
    blk = WIN_HALF
    kwin = 3 * blk
    n_blk = seq // blk
    kv = pl.program_id(1)
    scale2 = (HEAD_DIM ** -0.5) * LOG2E

    ri = lax.broadcasted_iota(jnp.int32, (blk, kwin), 0)
    ci = lax.broadcasted_iota(jnp.int32, (blk, kwin), 1)
    for w, shift in enumerate((0, -blk, -2 * blk)):
        bias_s[w] = jnp.where(jnp.abs(ci + shift - ri) <= WIN_HALF, 0.0, NEG_INF).astype(F32)
    vext_s[:, 0:HEAD_DIM] = v_ref[0]
    vext_s[:, HEAD_DIM:2 * HEAD_DIM] = jnp.ones((seq, HEAD_DIM), BF16)
    sinks = [sink_ref[kv * WIN_REP + r] * LOG2E for r in range(WIN_REP)]

    def one_block(qi, which):
        q0 = qi * blk if isinstance(qi, int) else pl.multiple_of(qi * blk, blk)
        k0 = (0, q0 - blk, seq - kwin)[which]
        if not isinstance(k0, int):
            k0 = pl.multiple_of(k0, blk)
        qt = q_ref[0, pl.ds(q0, blk), :]
        kt = k_ref[0, pl.ds(k0, kwin), :]
        vt = vext_s[pl.ds(k0, kwin), :]
        bias = bias_s[which]
        for r in range(WIN_REP):
            s = _dot_nt(qt[:, r * HEAD_DIM:(r + 1) * HEAD_DIM], kt) * scale2 + bias
            m = jnp.maximum(jnp.max(s, axis=-1, keepdims=True), sinks[r])
            e = jnp.exp2(s - m)
            acc = _dot(e.astype(BF16), vt)
            den = acc[:, HEAD_DIM:] + jnp.exp2(sinks[r] - m)
            o_ref[0, pl.ds(q0, blk), r * HEAD_DIM:(r + 1) * HEAD_DIM] = (
                acc[:, :HEAD_DIM] / den).astype(o_ref.dtype)

    one_block(0, 0)
    one_block(n_blk - 1, 2)

    def body(qi, carry):
        one_block(qi, 1)
        return carry

    lax.fori_loop(1, n_blk - 1, body, 0, unroll=2)


def _win_attn(proj3, sink):
    b, seq, _ = proj3.shape
    assert seq >= 3 * WIN_HALF and seq % WIN_HALF == 0
    qw = WIN_REP * HEAD_DIM
    kern = functools.partial(_win_kernel, seq=seq)
    return pl.pallas_call(
        kern,
        grid=(b, WIN_KV_HEADS),
        in_specs=[
            pl.BlockSpec(memory_space=pltpu.SMEM),
            pl.BlockSpec((1, seq, qw), lambda i, g: (i, 0, OFF_QW // qw + g)),
            pl.BlockSpec((1, seq, HEAD_DIM), lambda i, g: (i, 0, OFF_KW // HEAD_DIM + g)),
            pl.BlockSpec((1, seq, HEAD_DIM), lambda i, g: (i, 0, OFF_VW // HEAD_DIM + g)),
        ],
        out_specs=pl.BlockSpec((1, seq, qw), lambda i, g: (i, 0, g)),
        out_shape=jax.ShapeDtypeStruct((b, seq, WIN_Q_HEADS * HEAD_DIM), BF16),
        scratch_shapes=[pltpu.VMEM((3, WIN_HALF, 3 * WIN_HALF), F32),
                        pltpu.VMEM((seq, 2 * HEAD_DIM), BF16)],
        compiler_params=pltpu.CompilerParams(
            dimension_semantics=("parallel", "parallel"), vmem_limit_bytes=VMEM_LIMIT),
        name="win_attn",
    )(sink, proj3, proj3, proj3)


def _dil_kernel(*refs, seq):
    qkv_refs = refs[:9]
    o_ref = refs[9]
    stage_s, qd_s, kd_s, vext_s, od_s, ld_s, og_s, lg_s, bias_s, bias1_s = refs[10:]
    tq = 2 * DIL_HALF
    kw = tq + 2 * DIL_HALF
    scale2 = (HEAD_DIM ** -0.5) * LOG2E

    ri = lax.broadcasted_iota(jnp.int32, (tq, kw), 0)
    ci = lax.broadcasted_iota(jnp.int32, (tq, kw), 1)
    for w, shift in enumerate((0, -DIL_HALF, -2 * DIL_HALF)):
        bias_s[w] = jnp.where(jnp.abs(ci + shift - ri) <= DIL_HALF, 0.0, NEG_INF).astype(F32)
    ri1 = lax.broadcasted_iota(jnp.int32, (tq, tq), 0)
    ci1 = lax.broadcasted_iota(jnp.int32, (tq, tq), 1)
    bias1_s[...] = jnp.where(jnp.abs(ci1 - ri1) <= DIL_HALF, 0.0, NEG_INF).astype(F32)
    vext_s[:, HEAD_DIM:2 * HEAD_DIM] = jnp.ones((seq, HEAD_DIM), BF16)

    def deinterleave(src_ref, dst_ref, dil):
        n_sub = seq // dil
        stage_s[...] = src_ref[0].astype(F32)
        for r in range(dil):
            dst_ref[r * n_sub:(r + 1) * n_sub, 0:HEAD_DIM] = (
                stage_s[pl.ds(r, n_sub, stride=dil), :].astype(BF16))

    def aligned(x, mult):
        return x if isinstance(x, int) else pl.multiple_of(x, mult)

    def run_group(q_at, k_at, n_sub):
        def tile(q0, k0, width, bias):
            q0, k0 = aligned(q0, tq), aligned(k0, DIL_HALF)
            s = _dot_nt(q_at(pl.ds(q0, tq)), k_at(pl.ds(k0, width))) * scale2 + bias
            m = jnp.max(s, axis=-1, keepdims=True)
            e = jnp.exp2(s - m)
            acc = _dot(e.astype(BF16), vext_s[pl.ds(k0, width), :])
            den = acc[:, HEAD_DIM:]
            od_s[pl.ds(q0, tq), :] = acc[:, :HEAD_DIM] / den
            ld_s[pl.ds(q0, tq), :] = m + jnp.log2(den)

        tiles_per_seg = n_sub // tq
        if tiles_per_seg == 1:
            def body1(ti, carry):
                tile(ti * tq, ti * tq, tq, bias1_s[...])
                return carry
            lax.fori_loop(0, seq // tq, body1, 0, unroll=4)
            return

        def segment(seg0):
            tile(seg0, seg0, kw, bias_s[0])
            inner = tiles_per_seg - 2
            if inner <= 2:
                for ti in range(1, 1 + inner):
                    tile(seg0 + ti * tq, seg0 + ti * tq - DIL_HALF, kw, bias_s[1])
            else:
                def body(ti, carry):
                    tile(seg0 + ti * tq, seg0 + ti * tq - DIL_HALF, kw, bias_s[1])
                    return carry
                lax.fori_loop(1, 1 + inner, body, 0, unroll=2)
            tile(seg0 + n_sub - tq, seg0 + n_sub - kw, kw, bias_s[2])

        n_seg = seq // n_sub
        if n_seg == 1:
            segment(0)
        else:
            def seg_body(si, carry):
                segment(si * n_sub)
                return carry
            lax.fori_loop(0, n_seg, seg_body, 0)

    for gi, (_, dil) in enumerate(DIL_PATTERNS):
        q_ref, k_ref, v_ref = qkv_refs[gi], qkv_refs[DIL_GROUPS + gi], qkv_refs[2 * DIL_GROUPS + gi]
        n_sub = seq // dil
        if dil == 1:
            vext_s[:, 0:HEAD_DIM] = v_ref[0]
            run_group(lambda sl: q_ref[0, sl, :], lambda sl: k_ref[0, sl, :], n_sub)
            og_s[gi] = od_s[...]
            lg_s[gi] = ld_s[...]
        else:
            deinterleave(q_ref, qd_s, dil)
            deinterleave(k_ref, kd_s, dil)
            deinterleave(v_ref, vext_s, dil)
            run_group(lambda sl: qd_s[sl, :], lambda sl: kd_s[sl, :], n_sub)
            for r in range(dil):
                og_s[gi, pl.ds(r, n_sub, stride=dil), :] = od_s[r * n_sub:(r + 1) * n_sub, :]
                lg_s[gi, pl.ds(r, n_sub, stride=dil), :] = ld_s[r * n_sub:(r + 1) * n_sub, :]

    m = jnp.maximum(jnp.maximum(lg_s[0], lg_s[1]), lg_s[2])
    num = jnp.zeros((seq, HEAD_DIM), F32)
    den = jnp.zeros((seq, HEAD_DIM), F32)
    for gi in range(DIL_GROUPS):
        w = jnp.exp2(lg_s[gi] - m)
        num = num + w * og_s[gi]
        den = den + w
    o_ref[0] = (num / den).astype(o_ref.dtype)


def _dil_attn(proj3):
    b, seq, _ = proj3.shape
    tq = 2 * DIL_HALF
    for _, dil in DIL_PATTERNS:
        assert (seq // dil) % tq == 0
    base_blk = OFF_QKVD // HEAD_DIM
    in_specs = []
    for part in range(3):
        for gi in range(DIL_GROUPS):
            blk0 = base_blk + (part * DIL_GROUPS + gi) * DIL_HEADS
            in_specs.append(pl.BlockSpec((1, seq, HEAD_DIM), lambda i, h, blk0=blk0: (i, 0, blk0 + h)))
    kern = functools.partial(_dil_kernel, seq=seq)
    return pl.pallas_call(
        kern,
        grid=(b, DIL_HEADS),
        in_specs=in_specs,
        out_specs=pl.BlockSpec((1, seq, HEAD_DIM), lambda i, h: (i, 0, h)),
        out_shape=jax.ShapeDtypeStruct((b, seq, DIL_WIDTH), BF16),
        scratch_shapes=[
            pltpu.VMEM((seq, HEAD_DIM), F32),
            pltpu.VMEM((seq, HEAD_DIM), BF16),
            pltpu.VMEM((seq, HEAD_DIM), BF16),
            pltpu.VMEM((seq, 2 * HEAD_DIM), BF16),
            pltpu.VMEM((seq, HEAD_DIM), F32),
            pltpu.VMEM((seq, HEAD_DIM), F32),
            pltpu.VMEM((DIL_GROUPS, seq, HEAD_DIM), F32),
            pltpu.VMEM((DIL_GROUPS, seq, HEAD_DIM), F32),
            pltpu.VMEM((3, tq, tq + 2 * DIL_HALF), F32),
            pltpu.VMEM((tq, tq), F32),
        ],
        compiler_params=pltpu.CompilerParams(
            dimension_semantics=("parallel", "parallel"), vmem_limit_bytes=VMEM_LIMIT),
        name="dil_attn",
    )(*([proj3] * 9))


def _rope_tables(seq):
    inv = ROPE_THETA ** (-jnp.arange(0, ROPE_DIM, 2, dtype=F32) / ROPE_DIM)
    ang = jnp.arange(seq, dtype=F32)[:, None] * inv[None, :]
    cos, sin = jnp.cos(ang), jnp.sin(ang)
    zeros = jnp.zeros((seq, HEAD_DIM - ROPE_DIM), F32)
    zh = jnp.zeros_like(sin)
    ct = jnp.concatenate([cos, cos, jnp.ones_like(zeros)], axis=1)
    sa = jnp.concatenate([zh, sin, zeros], axis=1)
    sb = jnp.concatenate([-sin, zh, zeros], axis=1)
    return ct, sa, sb


def _group_major(v):
    return v.reshape(2, SSD_GROUPS, HEADS_PER_GROUP).transpose(1, 0, 2).reshape(DT_COLS)


def _layer_params(w_in, dt_bias, a_log, d_skip):
    w_main = jnp.concatenate([w_in[:, :DT_IN_OFF], w_in[:, DT_IN_OFF + DT_COLS:]], axis=1).astype(BF16)
    w_dt = w_in[:, DT_IN_OFF:DT_IN_OFF + DT_COLS]
    w_dt = w_dt.reshape(D_MODEL, 2, SSD_GROUPS, HEADS_PER_GROUP).transpose(0, 2, 1, 3).reshape(D_MODEL, DT_COLS)
    w_dtt = jnp.pad(w_dt, ((0, 0), (0, DT_LANES - DT_COLS))).astype(BF16).T
    b_dt = jnp.pad(_group_major(dt_bias.astype(F32)), (0, DT_LANES - DT_COLS))
    alog = jnp.pad(_group_major(a_log.astype(F32)), (0, DT_LANES - DT_COLS))
    p64 = jnp.repeat(d_skip.astype(F32).reshape(SSD_GROUPS, 1, HEADS_PER_GROUP), SSD_HEAD_DIM, axis=2)
    p64 = jnp.pad(p64, ((0, 0), (0, SUBLANE - 1), (0, 0)))
    return w_main, w_dtt, b_dt.reshape(DT_LANES, 1), alog.reshape(DT_LANES, 1), p64


def kernel(x, g_mix, w_in, conv_w, conv_b, dt_bias, a_log, d_skip, ssd_norm, w_a, w_b, w_c, sink,
           w_out, g_mlp, w_up, w_down, g_final):
    b, seq, _ = x.shape
    m = b * seq
    tm = min(1024, seq)
    ct, sa, sb = _rope_tables(seq)
    xf = x.reshape(m, D_MODEL).astype(F32)
    for i in range(DEPTH):
        w_main, w_dtt, b_dtt, alog_col, p64 = _layer_params(w_in[i], dt_bias[i], a_log[i], d_skip[i])
        proj, dtt = _proj(xf, g_mix[i].reshape(1, D_MODEL), w_main, w_dtt, b_dtt, ct, sa, sb, seq,
                          tm=tm, tn=1024)
        proj3 = proj.reshape(b, seq, N_MAIN)
        y_ssd = _ssd(proj3, conv_w[i], conv_b[i].reshape(1, XBC_WIDTH), dtt, alog_col, p64)
        y_b = _dil_attn(proj3)
        y_c = _win_attn(proj3, sink[i].astype(F32))
        merged = _merge(y_ssd.reshape(m, SSD_INNER), proj, ssd_norm[i].reshape(1, SSD_INNER),
                        y_b.reshape(m, DIL_WIDTH), y_c.reshape(m, WIN_Q_HEADS * HEAD_DIM),
                        w_a[i].astype(BF16), w_b[i].astype(BF16), w_c[i].astype(BF16), tm=512, tn=512)
        xf = _matmul_res(merged, w_out[i].astype(BF16), xf, tm=tm, tn=1024, tk=D_MODEL)
        u = _mlp_up(xf, g_mlp[i].reshape(1, D_MODEL), w_up[i].astype(BF16), tm=tm, tn=1024)
        xf = _matmul_res(u, w_down[i].astype(BF16), xf, tm=tm, tn=1024, tk=2048)
    out = _final_norm(xf, g_final.reshape(1, D_MODEL), tm=min(512, seq))
    return out.reshape(b, seq, D_MODEL).astype(x.dtype)
```

```python
import functools

import jax
import jax.numpy as jnp
from jax import lax
from jax.experimental import pallas as pl
from jax.experimental.pallas import tpu as pltpu

F32 = jnp.float32
BF16 = jnp.bfloat16

D_MODEL = 2048
DEPTH = 2
SSD_HEADS = 32
SSD_HEAD_DIM = 64
SSD_INNER = SSD_HEADS * SSD_HEAD_DIM
SSD_GROUPS = 8
SSD_STATE = 128
SSD_CHUNK = 128
HEADS_PER_GROUP = SSD_HEADS // SSD_GROUPS
GROUP_WIDTH = HEADS_PER_GROUP * SSD_HEAD_DIM
CONV_WIDTH = 5
CONV_PAD = (CONV_WIDTH - 1) // 2
XBC_WIDTH = SSD_INNER + 2 * SSD_GROUPS * SSD_STATE
HEAD_DIM = 128
ROPE_DIM = HEAD_DIM // 4
ROPE_HALF = ROPE_DIM // 2
ROPE_THETA = 500000.0
DIL_PATTERNS = ((128, 1), (512, 4), (2048, 16))
DIL_GROUPS = len(DIL_PATTERNS)
DIL_HEADS = 8
DIL_WIDTH = DIL_HEADS * HEAD_DIM
DIL_HALF = 64
TILE_BATCH = 4
WIN_Q_HEADS = 16
WIN_KV_HEADS = 4
WIN_REP = WIN_Q_HEADS // WIN_KV_HEADS
WIN_HALF = 128
D_FF = 4 * D_MODEL
N_BRANCH = 3
EPS = 1e-6
NEG_INF = -1e30
LOG2E = 1.4426950408889634

OFF_Z = 0
OFF_XBC = OFF_Z + SSD_INNER
OFF_QKVD = OFF_XBC + XBC_WIDTH
OFF_QW = OFF_QKVD + 3 * DIL_GROUPS * DIL_WIDTH
OFF_KW = OFF_QW + WIN_Q_HEADS * HEAD_DIM
OFF_VW = OFF_KW + WIN_KV_HEADS * HEAD_DIM
OFF_GATE = OFF_VW + WIN_KV_HEADS * HEAD_DIM
N_MAIN = OFF_GATE + N_BRANCH * D_MODEL
DT_COLS = 2 * SSD_HEADS
DT_IN_OFF = SSD_INNER + XBC_WIDTH
DT_LANES = 128

LANE = 128
SUBLANE = 8
VMEM_LIMIT = 56 * 1024 * 1024


def _split3(x):
    hi = x.astype(BF16)
    r1 = x - hi.astype(F32)
    mid = r1.astype(BF16)
    r2 = r1 - mid.astype(F32)
    return hi, mid, r2.astype(BF16)


def _dot(a, b):
    return jnp.dot(a, b, preferred_element_type=F32)


def _dot_nt(a, b):
    return lax.dot_general(a, b, (((1,), (1,)), ((), ())), preferred_element_type=F32)


def _exact_right(x, mat):
    hi, mid, lo = _split3(x)
    return _dot(hi, mat) + _dot(mid, mat) + _dot(lo, mat)


def _softplus(x):
    return jnp.maximum(x, 0.0) + jnp.log1p(jnp.exp(-jnp.abs(x)))


def _norm_rows(x, g):
    ms = jnp.mean(x * x, axis=-1, keepdims=True)
    return x * lax.rsqrt(ms + EPS) * g


def _rope_tile(a, ct, sa, sb):
    return a * ct + pltpu.roll(a, ROPE_HALF, 1) * sa + pltpu.roll(a, HEAD_DIM - ROPE_HALF, 1) * sb


def _proj_kernel(x_ref, g_ref, w_ref, wdtt_ref, bdtt_ref, ct_ref, sa_ref, sb_ref,
                 o_ref, dtt_ref, h_ref, *, tn, tile_kinds):
    j = pl.program_id(1)

    @pl.when(j == 0)
    def _():
        hb = _norm_rows(x_ref[...], g_ref[...]).astype(BF16)
        h_ref[...] = hb
        dtt_ref[...] = _softplus(_dot_nt(wdtt_ref[...], hb) + bdtt_ref[...])

    def tile_body(kind):
        acc = _dot(h_ref[...], w_ref[...])
        if 'r' in kind:
            ct, sa, sb = ct_ref[...], sa_ref[...], sb_ref[...]
        for c, k in enumerate(kind):
            sl = slice(c * HEAD_DIM, (c + 1) * HEAD_DIM)
            a = acc[:, sl]
            if k == 'r':
                a = _rope_tile(a, ct, sa, sb)
            elif k == 'g':
                a = 1.0 / (1.0 + jnp.exp(-a))
            o_ref[:, sl] = a.astype(o_ref.dtype)

    for kind in sorted(set(tile_kinds)):
        idx = [t for t, k in enumerate(tile_kinds) if k == kind]
        runs, start = [], idx[0]
        for a, b in zip(idx, idx[1:] + [None]):
            if b != a + 1:
                runs.append((start, a))
                start = b
        cond = None
        for lo, hi in runs:
            c = jnp.logical_and(j >= lo, j <= hi)
            cond = c if cond is None else jnp.logical_or(cond, c)
        pl.when(cond)(functools.partial(tile_body, kind))


def _proj_tile_kinds(tn):
    def head_kind(col):
        if OFF_QKVD <= col < OFF_VW:
            return 'r'
        return 'g' if col >= OFF_GATE else 'p'
    return tuple(''.join(head_kind(t * tn + c * HEAD_DIM) for c in range(tn // HEAD_DIM))
                 for t in range(N_MAIN // tn))


def _proj(x2d, g, w_main, w_dtt, b_dtt, ct, sa, sb, seq, *, tm, tn):
    m = x2d.shape[0]
    assert seq % tm == 0 and m % tm == 0 and N_MAIN % tn == 0 and tn % HEAD_DIM == 0
    seq_tiles = seq // tm
    kern = functools.partial(_proj_kernel, tn=tn, tile_kinds=_proj_tile_kinds(tn))
    return pl.pallas_call(
        kern,
        grid=(m // tm, N_MAIN // tn),
        in_specs=[
            pl.BlockSpec((tm, D_MODEL), lambda i, j: (i, 0)),
            pl.BlockSpec((1, D_MODEL), lambda i, j: (0, 0)),
            pl.BlockSpec((D_MODEL, tn), lambda i, j: (0, j)),
            pl.BlockSpec((DT_LANES, D_MODEL), lambda i, j: (0, 0)),
            pl.BlockSpec((DT_LANES, 1), lambda i, j: (0, 0)),
            pl.BlockSpec((tm, HEAD_DIM), lambda i, j: (i % seq_tiles, 0)),
            pl.BlockSpec((tm, HEAD_DIM), lambda i, j: (i % seq_tiles, 0)),
            pl.BlockSpec((tm, HEAD_DIM), lambda i, j: (i % seq_tiles, 0)),
        ],
        out_specs=[
            pl.BlockSpec((tm, tn), lambda i, j: (i, j)),
            pl.BlockSpec((DT_LANES, tm), lambda i, j: (0, i)),
        ],
        out_shape=[
            jax.ShapeDtypeStruct((m, N_MAIN), BF16),
            jax.ShapeDtypeStruct((DT_LANES, m), F32),
        ],
        scratch_shapes=[pltpu.VMEM((tm, D_MODEL), BF16)],
        compiler_params=pltpu.CompilerParams(
            dimension_semantics=("parallel", "arbitrary"), vmem_limit_bytes=VMEM_LIMIT),
        name="proj",
    )(x2d, g, w_main, w_dtt, b_dtt, ct, sa, sb)


def _up_kernel(x_ref, g_ref, w_ref, o_ref, h_ref):
    @pl.when(pl.program_id(1) == 0)
    def _():
        h_ref[...] = _norm_rows(x_ref[...], g_ref[...]).astype(BF16)

    acc = jnp.maximum(_dot(h_ref[...], w_ref[...]), 0.0)
    o_ref[...] = (acc * acc).astype(o_ref.dtype)


def _mlp_up(x2d, g, w, *, tm, tn):
    m, n = x2d.shape[0], w.shape[1]
    return pl.pallas_call(
        _up_kernel,
        grid=(m // tm, n // tn),
        in_specs=[
            pl.BlockSpec((tm, D_MODEL), lambda i, j: (i, 0)),
            pl.BlockSpec((1, D_MODEL), lambda i, j: (0, 0)),
            pl.BlockSpec((D_MODEL, tn), lambda i, j: (0, j)),
        ],
        out_specs=pl.BlockSpec((tm, tn), lambda i, j: (i, j)),
        out_shape=jax.ShapeDtypeStruct((m, n), BF16),
        scratch_shapes=[pltpu.VMEM((tm, D_MODEL), BF16)],
        compiler_params=pltpu.CompilerParams(
            dimension_semantics=("parallel", "arbitrary"), vmem_limit_bytes=VMEM_LIMIT),
        name="mlp_up",
    )(x2d, g, w)


def _matmul_res_kernel(a_ref, w_ref, r_ref, o_ref, acc_ref):
    k = pl.program_id(2)

    @pl.when(k == 0)
    def _():
        acc_ref[...] = jnp.zeros_like(acc_ref)

    acc_ref[...] += _dot(a_ref[...], w_ref[...])

    @pl.when(k == pl.num_programs(2) - 1)
    def _():
        o_ref[...] = r_ref[...] + acc_ref[...]


def _matmul_res(a, w, res, *, tm, tn, tk):
    m, kdim = a.shape
    n = w.shape[1]
    return pl.pallas_call(
        _matmul_res_kernel,
        grid=(m // tm, n // tn, kdim // tk),
        in_specs=[
            pl.BlockSpec((tm, tk), lambda i, j, k: (i, k)),
            pl.BlockSpec((tk, tn), lambda i, j, k: (k, j)),
            pl.BlockSpec((tm, tn), lambda i, j, k: (i, j)),
        ],
        out_specs=pl.BlockSpec((tm, tn), lambda i, j, k: (i, j)),
        out_shape=jax.ShapeDtypeStruct((m, n), F32),
        scratch_shapes=[pltpu.VMEM((tm, tn), F32)],
        compiler_params=pltpu.CompilerParams(
            dimension_semantics=("parallel", "parallel", "arbitrary"), vmem_limit_bytes=VMEM_LIMIT),
        name="matmul_res",
    )(a, w, res)


def _merge_kernel(ys_ref, z_ref, nw_ref, yb_ref, yc_ref, g0_ref, g1_ref, g2_ref,
                  wa_ref, wb_ref, wc_ref, o_ref, ya_ref):
    @pl.when(pl.program_id(1) == 0)
    def _():
        z = z_ref[...].astype(F32)
        y = ys_ref[...].astype(F32) * (z / (1.0 + jnp.exp(-z)))
        ya_ref[...] = _norm_rows(y, nw_ref[...]).astype(BF16)

    acc = g0_ref[...].astype(F32) * _dot(ya_ref[...], wa_ref[...])
    acc += g1_ref[...].astype(F32) * _dot(yb_ref[...], wb_ref[...])
    acc += g2_ref[...].astype(F32) * _dot(yc_ref[...], wc_ref[...])
    o_ref[...] = acc.astype(o_ref.dtype)


def _merge(y_ssd, proj, norm_w, y_b, y_c, w_a, w_b, w_c, *, tm, tn):
    m = y_ssd.shape[0]
    gate_blk = OFF_GATE // tn
    per_gate = D_MODEL // tn
    return pl.pallas_call(
        _merge_kernel,
        grid=(m // tm, D_MODEL // tn),
        in_specs=[
            pl.BlockSpec((tm, SSD_INNER), lambda i, j: (i, 0)),
            pl.BlockSpec((tm, SSD_INNER), lambda i, j: (i, OFF_Z // SSD_INNER)),
            pl.BlockSpec((1, SSD_INNER), lambda i, j: (0, 0)),
            pl.BlockSpec((tm, DIL_WIDTH), lambda i, j: (i, 0)),
            pl.BlockSpec((tm, WIN_Q_HEADS * HEAD_DIM), lambda i, j: (i, 0)),
            pl.BlockSpec((tm, tn), lambda i, j: (i, gate_blk + j)),
            pl.BlockSpec((tm, tn), lambda i, j: (i, gate_blk + per_gate + j)),
            pl.BlockSpec((tm, tn), lambda i, j: (i, gate_blk + 2 * per_gate + j)),
            pl.BlockSpec((SSD_INNER, tn), lambda i, j: (0, j)),
            pl.BlockSpec((DIL_WIDTH, tn), lambda i, j: (0, j)),
            pl.BlockSpec((WIN_Q_HEADS * HEAD_DIM, tn), lambda i, j: (0, j)),
        ],
        out_specs=pl.BlockSpec((tm, tn), lambda i, j: (i, j)),
        out_shape=jax.ShapeDtypeStruct((m, D_MODEL), BF16),
        scratch_shapes=[pltpu.VMEM((tm, SSD_INNER), BF16)],
        compiler_params=pltpu.CompilerParams(
            dimension_semantics=("parallel", "arbitrary"), vmem_limit_bytes=VMEM_LIMIT),
        name="merge",
    )(y_ssd, proj, norm_w, y_b, y_c, proj, proj, proj, w_a, w_b, w_c)


def _final_norm_kernel(x_ref, g_ref, o_ref):
    o_ref[...] = _norm_rows(x_ref[...], g_ref[...])


def _final_norm(x2d, g, *, tm):
    m = x2d.shape[0]
    return pl.pallas_call(
        _final_norm_kernel,
        grid=(m // tm,),
        in_specs=[pl.BlockSpec((tm, D_MODEL), lambda i: (i, 0)),
                  pl.BlockSpec((1, D_MODEL), lambda i: (0, 0))],
        out_specs=pl.BlockSpec((tm, D_MODEL), lambda i: (i, 0)),
        out_shape=jax.ShapeDtypeStruct((m, D_MODEL), F32),
        compiler_params=pltpu.CompilerParams(dimension_semantics=("parallel",)),
        name="final_norm",
    )(x2d, g)


def _ssd_kernel(xs_ref, bm_ref, cm_ref, cwx_ref, cwb_ref, cwc_ref, cbx_ref, cbb_ref, cbc_ref,
                dtt_ref, alt_ref, p64_ref,
                y_ref,
                padx_ref, padb_ref, padc_ref, xs_s, bm_s, cm_s, bmt_s, dtr_s, y_s, st_s, tri_s,
                u_s, e_s, scl_s, *, seq):
    t = SSD_CHUNK
    n_chunks = seq // t
    for c in range(n_chunks):
        dtr_s[c] = dtt_ref[:, c * t:(c + 1) * t]

    halo = SUBLANE
    for pad_ref, src_ref in ((padx_ref, xs_ref), (padb_ref, bm_ref), (padc_ref, cm_ref)):
        width = pad_ref.shape[1]
        pad_ref[0:halo, :] = jnp.zeros((halo, width), F32)
        pad_ref[halo + seq:2 * halo + seq, :] = jnp.zeros((halo, width), F32)
        pad_ref[halo:halo + seq, :] = src_ref[0].astype(F32)

    def conv_silu(pad_ref, w_ref, b_ref, r0, lanes):
        acc = b_ref[:, lanes]
        for k in range(CONV_WIDTH):
            lo = halo - CONV_PAD + k + r0
            acc = acc + w_ref[k:k + 1, lanes] * pad_ref[lo:lo + t, lanes]
        return acc / (1.0 + jnp.exp(-acc))

    for c in range(n_chunks):
        r0 = c * t
        rows = slice(r0, r0 + t)
        for half in range(GROUP_WIDTH // LANE):
            lanes = slice(half * LANE, (half + 1) * LANE)
            xv = conv_silu(padx_ref, cwx_ref, cbx_ref, r0, lanes)
            xs_s[rows, lanes] = xv.astype(BF16)
            y_s[rows, lanes] = xv * p64_ref[0, 0:1, lanes]
        bv = conv_silu(padb_ref, cwb_ref, cbb_ref, r0, slice(0, LANE))
        bm_s[rows, :] = bv.astype(BF16)
        bmt_s[c] = bv.T.astype(BF16)
        cm_s[rows, :] = conv_silu(padc_ref, cwc_ref, cbc_ref, r0, slice(0, LANE)).astype(BF16)

    ri = lax.broadcasted_iota(jnp.int32, (t, t), 0)
    ci = lax.broadcasted_iota(jnp.int32, (t, t), 1)
    lower, upper = ri >= ci, ri <= ci
    tri_s[:, 0:t] = upper.astype(BF16)
    tri_s[:, t:2 * t] = lower.astype(BF16)
    st_s[...] = jnp.zeros_like(st_s)

    nh = HEADS_PER_GROUP
    a_rows = -jnp.exp(alt_ref[...])
    head_of_lane = lax.broadcasted_iota(jnp.int32, (1, GROUP_WIDTH), 1) // SSD_HEAD_DIM
    low_half = lax.broadcasted_iota(jnp.int32, (1, LANE), 1) < SSD_HEAD_DIM

    def lanes_from_row(row):
        return jnp.broadcast_to(row, (t, t)).T

    def expand64(cols):
        return jnp.concatenate([jnp.where(low_half, cols[0], cols[1]),
                                jnp.where(low_half, cols[2], cols[3])], axis=1)

    def prepare(chunks):
        loaded = []
        for c in chunks:
            rows = pl.ds(pl.multiple_of(c * t, t), t)
            loaded.append((c, rows, dtr_s[c], xs_s[rows, :], cm_s[rows, :], bm_s[rows, :], bmt_s[c]))
        cums = [_exact_right(dt_all, tri_s[...]) for _, _, dt_all, _, _, _, _ in loaded]
        cbs = [_dot_nt(cm_c, bm_c) for _, _, _, _, cm_c, bm_c, _ in loaded]
        staged = []
        for (c, rows, dt_all, xs_c, _, _, bmt_c), cum, cb in zip(loaded, cums, cbs):
            cs = (cum[0:nh, 0:t] * a_rows[0:nh], cum[nh:2 * nh, t:2 * t] * a_rows[nh:2 * nh])
            dts = (dt_all[0:nh, :], dt_all[nh:2 * nh, :])
            tots = (cs[0][:, t - 1:t], cs[1][:, 0:1])
            w_rows = [dts[d] * jnp.exp(tots[d] - cs[d]) for d in range(2)]
            grow = [jnp.exp(tots[d]) for d in range(2)]
            bmt_f = bmt_c.astype(F32)
            l_parts, bw, seen, scale = [], ([], []), ([], []), [0.0, 0.0]
            for h in range(nh):
                mix = jnp.zeros((t, t), F32)
                for d, valid in ((0, lower), (1, upper)):
                    row = cs[d][h:h + 1, :]
                    col = lanes_from_row(row)
                    mix = mix + jnp.exp(jnp.where(valid, col - row, NEG_INF)) * dts[d][h:h + 1, :]
                    seen[d].append(jnp.exp(col))
                    bw[d].append((bmt_f * w_rows[d][h:h + 1, :]).astype(BF16))
                    scale[d] = scale[d] + jnp.where(head_of_lane == h, grow[d][h:h + 1, :], 0.0)
                l_parts.append((cb * mix).astype(BF16))
            lhs = jnp.concatenate([jnp.concatenate(l_parts, axis=1), jnp.concatenate(bw[0], axis=1),
                                   jnp.concatenate(bw[1], axis=1)], axis=0)
            xs_stack = jnp.concatenate(
                [jnp.where(head_of_lane == h, xs_c, jnp.zeros_like(xs_c)) for h in range(nh)], axis=0)
            staged.append((c, rows, lhs, xs_stack, seen, scale))
        outs = [_dot(lhs, xs_stack) for _, _, lhs, xs_stack, _, _ in staged]
        for (c, rows, _, _, seen, scale), out in zip(staged, outs):
            y_s[rows, :] += out[0:t, :]
            for d in range(2):
                u_s[d, c] = out[(d + 1) * t:(d + 2) * t, :]
                e_s[d, c] = expand64(seen[d])
                scl_s[d, c] = jnp.broadcast_to(scale[d], (SUBLANE, GROUP_WIDTH))

    batch = 2
    def prep_body(i, carry):
        prepare([i * batch + u for u in range(batch)])
        return carry

    lax.fori_loop(0, n_chunks // batch, prep_body, 0)

    def scan_body(i, carry):
        steps = []
        for d, c in ((0, i), (1, n_chunks - 1 - i)):
            rows = pl.ds(pl.multiple_of(c * t, t), t)
            steps.append((d, c, rows, st_s[d], cm_s[rows, :]))
        reads = [_dot(cm_c, st.astype(BF16)) for _, _, _, st, cm_c in steps]
        for (d, c, rows, st, _), read in zip(steps, reads):
            y_s[rows, :] += read * e_s[d, c]
            st_s[d] = st * scl_s[d, c][0:1, :] + u_s[d, c]
        return carry

    lax.fori_loop(0, n_chunks, scan_body, 0, unroll=2)
    y_ref[0] = y_s[...].astype(y_ref.dtype)


def _ssd(proj3, conv_w, conv_b, dtt, alog_col, p64):
    b, seq, _ = proj3.shape
    gw, ns = GROUP_WIDTH, SSD_STATE
    xs_blk = OFF_XBC // gw
    bm_blk = (OFF_XBC + SSD_INNER) // ns
    cm_blk = (OFF_XBC + SSD_INNER + SSD_GROUPS * ns) // ns
    cw_b_blk = SSD_INNER // ns
    cw_c_blk = (SSD_INNER + SSD_GROUPS * ns) // ns
    kern = functools.partial(_ssd_kernel, seq=seq)
    return pl.pallas_call(
        kern,
        grid=(b, SSD_GROUPS),
        in_specs=[
            pl.BlockSpec((1, seq, gw), lambda i, g: (i, 0, xs_blk + g)),
            pl.BlockSpec((1, seq, ns), lambda i, g: (i, 0, bm_blk + g)),
            pl.BlockSpec((1, seq, ns), lambda i, g: (i, 0, cm_blk + g)),
            pl.BlockSpec((CONV_WIDTH, gw), lambda i, g: (0, g)),
            pl.BlockSpec((CONV_WIDTH, ns), lambda i, g: (0, cw_b_blk + g)),
            pl.BlockSpec((CONV_WIDTH, ns), lambda i, g: (0, cw_c_blk + g)),
            pl.BlockSpec((1, gw), lambda i, g: (0, g)),
            pl.BlockSpec((1, ns), lambda i, g: (0, cw_b_blk + g)),
            pl.BlockSpec((1, ns), lambda i, g: (0, cw_c_blk + g)),
            pl.BlockSpec((2 * HEADS_PER_GROUP, seq), lambda i, g: (g, i)),
            pl.BlockSpec((2 * HEADS_PER_GROUP, 1), lambda i, g: (g, 0)),
            pl.BlockSpec((1, SUBLANE, gw), lambda i, g: (g, 0, 0)),
        ],
        out_specs=pl.BlockSpec((1, seq, gw), lambda i, g: (i, 0, g)),
        out_shape=jax.ShapeDtypeStruct((b, seq, SSD_INNER), BF16),
        scratch_shapes=[
            pltpu.VMEM((seq + 2 * SUBLANE, gw), F32),
            pltpu.VMEM((seq + 2 * SUBLANE, ns), F32),
            pltpu.VMEM((seq + 2 * SUBLANE, ns), F32),
            pltpu.VMEM((seq, gw), BF16),
            pltpu.VMEM((seq, ns), BF16),
            pltpu.VMEM((seq, ns), BF16),
            pltpu.VMEM((seq // SSD_CHUNK, ns, SSD_CHUNK), BF16),
            pltpu.VMEM((seq // SSD_CHUNK, 2 * HEADS_PER_GROUP, SSD_CHUNK), F32),
            pltpu.VMEM((seq, gw), F32),
            pltpu.VMEM((2, ns, gw), F32),
            pltpu.VMEM((SSD_CHUNK, 2 * SSD_CHUNK), BF16),
            pltpu.VMEM((2, seq // SSD_CHUNK, ns, gw), F32),
            pltpu.VMEM((2, seq // SSD_CHUNK, SSD_CHUNK, gw), F32),
            pltpu.VMEM((2, seq // SSD_CHUNK, SUBLANE, gw), F32),
        ],
        compiler_params=pltpu.CompilerParams(
            dimension_semantics=("parallel", "arbitrary"), vmem_limit_bytes=VMEM_LIMIT),
        name="ssd",
    )(proj3, proj3, proj3, conv_w, conv_w, conv_w, conv_b, conv_b, conv_b, dtt, alog_col, p64)


def _win_kernel(sink_ref, q_ref, k_ref, v_ref, o_ref, bias_s, vext_s, *, seq):
    blk = WIN_HALF
    kwin = 3 * blk
    n_blk = seq // blk
    kv = pl.program_id(1)
    scale2 = (HEAD_DIM ** -0.5) * LOG2E

    ri = lax.broadcasted_iota(jnp.int32, (blk, kwin), 0)
    ci = lax.broadcasted_iota(jnp.int32, (blk, kwin), 1)
    for w, shift in enumerate((0, -blk, -2 * blk)):
        bias_s[w] = jnp.where(jnp.abs(ci + shift - ri) <= WIN_HALF, 0.0, NEG_INF).astype(F32)
    vext_s[:, 0:HEAD_DIM] = v_ref[0]
    vext_s[:, HEAD_DIM:2 * HEAD_DIM] = jnp.ones((seq, HEAD_DIM), BF16)
    sinks = [sink_ref[kv * WIN_REP + r] * LOG2E for r in range(WIN_REP)]

    def blocks(specs):
        loaded = []
        for qi, which in specs:
            q0 = qi * blk if isinstance(qi, int) else pl.multiple_of(qi * blk, blk)
            k0 = (0, q0 - blk, seq - kwin)[which]
            if not isinstance(k0, int):
                k0 = pl.multiple_of(k0, blk)
            loaded.append((q0, q_ref[0, pl.ds(q0, blk), :], k_ref[0, pl.ds(k0, kwin), :],
                           vext_s[pl.ds(k0, kwin), :], bias_s[which]))
        scores = [[_dot_nt(qt[:, r * HEAD_DIM:(r + 1) * HEAD_DIM], kt) for r in range(WIN_REP)]
                  for _, qt, kt, _, _ in loaded]
        maxes, probs = [], []
        for heads, (_, _, _, _, bias) in zip(scores, loaded):
            for r, s in enumerate(heads):
                s = s * scale2 + bias
                m = jnp.maximum(jnp.max(s, axis=-1, keepdims=True), sinks[r])
                maxes.append(m)
                probs.append(jnp.exp2(s - m).astype(BF16))
        accs = [_dot(probs[i * WIN_REP + r], vt)
                for i, (_, _, _, vt, _) in enumerate(loaded) for r in range(WIN_REP)]
        for i, (q0, _, _, _, _) in enumerate(loaded):
            for r in range(WIN_REP):
                acc, m = accs[i * WIN_REP + r], maxes[i * WIN_REP + r]
                den = acc[:, HEAD_DIM:] + jnp.exp2(sinks[r] - m)
                o_ref[0, pl.ds(q0, blk), r * HEAD_DIM:(r + 1) * HEAD_DIM] = (
                    acc[:, :HEAD_DIM] / den).astype(o_ref.dtype)

    blocks([(0, 0), (n_blk - 1, 2)])
    pairs = (n_blk - 2) // 2

    def body(pi, carry):
        blocks([(1 + 2 * pi, 1), (2 + 2 * pi, 1)])
        return carry

    lax.fori_loop(0, pairs, body, 0)
    if (n_blk - 2) % 2:
        blocks([(n_blk - 2, 1)])


def _win_attn(proj3, sink):
    b, seq, _ = proj3.shape
    assert seq >= 3 * WIN_HALF and seq % WIN_HALF == 0
    qw = WIN_REP * HEAD_DIM
    kern = functools.partial(_win_kernel, seq=seq)
    return pl.pallas_call(
        kern,
        grid=(b, WIN_KV_HEADS),
        in_specs=[
            pl.BlockSpec(memory_space=pltpu.SMEM),
            pl.BlockSpec((1, seq, qw), lambda i, g: (i, 0, OFF_QW // qw + g)),
            pl.BlockSpec((1, seq, HEAD_DIM), lambda i, g: (i, 0, OFF_KW // HEAD_DIM + g)),
            pl.BlockSpec((1, seq, HEAD_DIM), lambda i, g: (i, 0, OFF_VW // HEAD_DIM + g)),
        ],
        out_specs=pl.BlockSpec((1, seq, qw), lambda i, g: (i, 0, g)),
        out_shape=jax.ShapeDtypeStruct((b, seq, WIN_Q_HEADS * HEAD_DIM), BF16),
        scratch_shapes=[pltpu.VMEM((3, WIN_HALF, 3 * WIN_HALF), F32),
                        pltpu.VMEM((seq, 2 * HEAD_DIM), BF16)],
        compiler_params=pltpu.CompilerParams(
            dimension_semantics=("parallel", "parallel"), vmem_limit_bytes=VMEM_LIMIT),
        name="win_attn",
    )(sink, proj3, proj3, proj3)


def _dil_kernel(*refs, seq):
    qkv_refs = refs[:9]
    o_ref = refs[9]
    stage_s, qd_s, kd_s, vext_s, od_s, ld_s, og_s, lg_s, bias_s, bias1_s = refs[10:]
    tq = 2 * DIL_HALF
    kw = tq + 2 * DIL_HALF
    scale2 = (HEAD_DIM ** -0.5) * LOG2E

    ri = lax.broadcasted_iota(jnp.int32, (tq, kw), 0)
    ci = lax.broadcasted_iota(jnp.int32, (tq, kw), 1)
    for w, shift in enumerate((0, -DIL_HALF, -2 * DIL_HALF)):
        bias_s[w] = jnp.where(jnp.abs(ci + shift - ri) <= DIL_HALF, 0.0, NEG_INF).astype(F32)
    ri1 = lax.broadcasted_iota(jnp.int32, (tq, tq), 0)
    ci1 = lax.broadcasted_iota(jnp.int32, (tq, tq), 1)
    bias1_s[...] = jnp.where(jnp.abs(ci1 - ri1) <= DIL_HALF, 0.0, NEG_INF).astype(F32)
    vext_s[:, HEAD_DIM:2 * HEAD_DIM] = jnp.ones((seq, HEAD_DIM), BF16)

    def deinterleave(src_ref, dst_ref, dil):
        n_sub = seq // dil
        stage_s[...] = src_ref[0].astype(F32)
        for r in range(dil):
            dst_ref[r * n_sub:(r + 1) * n_sub, 0:HEAD_DIM] = (
                stage_s[pl.ds(r, n_sub, stride=dil), :].astype(BF16))

    def aligned(x, mult):
        return x if isinstance(x, int) else pl.multiple_of(x, mult)

    def run_group(q_at, k_at, n_sub):
        def tiles(specs):
            loaded = []
            for q0, k0, width, bias in specs:
                q0, k0 = aligned(q0, tq), aligned(k0, DIL_HALF)
                loaded.append((q0, q_at(pl.ds(q0, tq)), k_at(pl.ds(k0, width)),
                               vext_s[pl.ds(k0, width), :], bias))
            scores = [_dot_nt(qt, kt) for _, qt, kt, _, _ in loaded]
            maxes, probs = [], []
            for s, (_, _, _, _, bias) in zip(scores, loaded):
                s = s * scale2 + bias
                m = jnp.max(s, axis=-1, keepdims=True)
                maxes.append(m)
                probs.append(jnp.exp2(s - m).astype(BF16))
            accs = [_dot(p, vt) for p, (_, _, _, vt, _) in zip(probs, loaded)]
            for acc, m, (q0, _, _, _, _) in zip(accs, maxes, loaded):
                den = acc[:, HEAD_DIM:]
                od_s[pl.ds(q0, tq), :] = acc[:, :HEAD_DIM] / den
                ld_s[pl.ds(q0, tq), :] = m + jnp.log2(den)

        def first(seg0):
            return (seg0, seg0, kw, bias_s[0])

        def inner(q0):
            return (q0, q0 - DIL_HALF, kw, bias_s[1])

        def last(seg0):
            return (seg0 + n_sub - tq, seg0 + n_sub - kw, kw, bias_s[2])

        tiles_per_seg, n_seg, nb = n_sub // tq, seq // n_sub, TILE_BATCH
        if tiles_per_seg == 1:
            def body1(bi, carry):
                tiles([((bi * nb + u) * tq, (bi * nb + u) * tq, tq, bias1_s[...]) for u in range(nb)])
                return carry
            lax.fori_loop(0, seq // (tq * nb), body1, 0)
        elif tiles_per_seg == nb:
            def seg_body(si, carry):
                seg0 = si * n_sub
                tiles([first(seg0)] + [inner(seg0 + ti * tq) for ti in range(1, nb - 1)] + [last(seg0)])
                return carry
            lax.fori_loop(0, n_seg, seg_body, 0)
        else:
            assert n_seg == 1
            loops = (tiles_per_seg - 2) // nb
            tiles([first(0)])
            def inner_body(bi, carry):
                tiles([inner((1 + bi * nb + u) * tq) for u in range(nb)])
                return carry
            lax.fori_loop(0, loops, inner_body, 0)
            tiles([inner(ti * tq) for ti in range(1 + loops * nb, tiles_per_seg - 1)] + [last(0)])

    for gi, (_, dil) in enumerate(DIL_PATTERNS):
        q_ref, k_ref, v_ref = qkv_refs[gi], qkv_refs[DIL_GROUPS + gi], qkv_refs[2 * DIL_GROUPS + gi]
        n_sub = seq // dil
        if dil == 1:
            vext_s[:, 0:HEAD_DIM] = v_ref[0]
            run_group(lambda sl: q_ref[0, sl, :], lambda sl: k_ref[0, sl, :], n_sub)
            og_s[gi] = od_s[...]
            lg_s[gi] = ld_s[...]
        else:
            deinterleave(q_ref, qd_s, dil)
            deinterleave(k_ref, kd_s, dil)
            deinterleave(v_ref, vext_s, dil)
            run_group(lambda sl: qd_s[sl, :], lambda sl: kd_s[sl, :], n_sub)
            for r in range(dil):
                og_s[gi, pl.ds(r, n_sub, stride=dil), :] = od_s[r * n_sub:(r + 1) * n_sub, :]
                lg_s[gi, pl.ds(r, n_sub, stride=dil), :] = ld_s[r * n_sub:(r + 1) * n_sub, :]

    m = jnp.maximum(jnp.maximum(lg_s[0], lg_s[1]), lg_s[2])
    num = jnp.zeros((seq, HEAD_DIM), F32)
    den = jnp.zeros((seq, HEAD_DIM), F32)
    for gi in range(DIL_GROUPS):
        w = jnp.exp2(lg_s[gi] - m)
        num = num + w * og_s[gi]
        den = den + w
    o_ref[0] = (num / den).astype(o_ref.dtype)


def _dil_attn(proj3):
    b, seq, _ = proj3.shape
    tq = 2 * DIL_HALF
    for _, dil in DIL_PATTERNS:
        assert (seq // dil) % tq == 0
    base_blk = OFF_QKVD // HEAD_DIM
    in_specs = []
    for part in range(3):
        for gi in range(DIL_GROUPS):
            blk0 = base_blk + (part * DIL_GROUPS + gi) * DIL_HEADS
            in_specs.append(pl.BlockSpec((1, seq, HEAD_DIM), lambda i, h, blk0=blk0: (i, 0, blk0 + h)))
    kern = functools.partial(_dil_kernel, seq=seq)
    return pl.pallas_call(
        kern,
        grid=(b, DIL_HEADS),
        in_specs=in_specs,
        out_specs=pl.BlockSpec((1, seq, HEAD_DIM), lambda i, h: (i, 0, h)),
        out_shape=jax.ShapeDtypeStruct((b, seq, DIL_WIDTH), BF16),
        scratch_shapes=[
            pltpu.VMEM((seq, HEAD_DIM), F32),
            pltpu.VMEM((seq, HEAD_DIM), BF16),
            pltpu.VMEM((seq, HEAD_DIM), BF16),
            pltpu.VMEM((seq, 2 * HEAD_DIM), BF16),
            pltpu.VMEM((seq, HEAD_DIM), F32),
            pltpu.VMEM((seq, HEAD_DIM), F32),
            pltpu.VMEM((DIL_GROUPS, seq, HEAD_DIM), F32),
            pltpu.VMEM((DIL_GROUPS, seq, HEAD_DIM), F32),
            pltpu.VMEM((3, tq, tq + 2 * DIL_HALF), F32),
            pltpu.VMEM((tq, tq), F32),
        ],
        compiler_params=pltpu.CompilerParams(
            dimension_semantics=("parallel", "parallel"), vmem_limit_bytes=VMEM_LIMIT),
        name="dil_attn",
    )(*([proj3] * 9))


def _rope_tables(seq):
    inv = ROPE_THETA ** (-jnp.arange(0, ROPE_DIM, 2, dtype=F32) / ROPE_DIM)
    ang = jnp.arange(seq, dtype=F32)[:, None] * inv[None, :]
    cos, sin = jnp.cos(ang), jnp.sin(ang)
    zeros = jnp.zeros((seq, HEAD_DIM - ROPE_DIM), F32)
    zh = jnp.zeros_like(sin)
    ct = jnp.concatenate([cos, cos, jnp.ones_like(zeros)], axis=1)
    sa = jnp.concatenate([zh, sin, zeros], axis=1)
    sb = jnp.concatenate([-sin, zh, zeros], axis=1)
    return ct, sa, sb


def _group_major(v):
    return v.reshape(2, SSD_GROUPS, HEADS_PER_GROUP).transpose(1, 0, 2).reshape(DT_COLS)


def _layer_params(w_in, dt_bias, a_log, d_skip):
    w_main = jnp.concatenate([w_in[:, :DT_IN_OFF], w_in[:, DT_IN_OFF + DT_COLS:]], axis=1).astype(BF16)
    w_dt = w_in[:, DT_IN_OFF:DT_IN_OFF + DT_COLS]
    w_dt = w_dt.reshape(D_MODEL, 2, SSD_GROUPS, HEADS_PER_GROUP).transpose(0, 2, 1, 3).reshape(D_MODEL, DT_COLS)
    w_dtt = jnp.pad(w_dt, ((0, 0), (0, DT_LANES - DT_COLS))).astype(BF16).T
    b_dt = jnp.pad(_group_major(dt_bias.astype(F32)), (0, DT_LANES - DT_COLS))
    alog = jnp.pad(_group_major(a_log.astype(F32)), (0, DT_LANES - DT_COLS))
    p64 = jnp.repeat(d_skip.astype(F32).reshape(SSD_GROUPS, 1, HEADS_PER_GROUP), SSD_HEAD_DIM, axis=2)
    p64 = jnp.pad(p64, ((0, 0), (0, SUBLANE - 1), (0, 0)))
    return w_main, w_dtt, b_dt.reshape(DT_LANES, 1), alog.reshape(DT_LANES, 1), p64


def kernel(x, g_mix, w_in, conv_w, conv_b, dt_bias, a_log, d_skip, ssd_norm, w_a, w_b, w_c, sink,
           w_out, g_mlp, w_up, w_down, g_final):
    b, seq, _ = x.shape
    m = b * seq
    tm = min(1024, seq)
    ct, sa, sb = _rope_tables(seq)
    xf = x.reshape(m, D_MODEL).astype(F32)
    for i in range(DEPTH):
        w_main, w_dtt, b_dtt, alog_col, p64 = _layer_params(w_in[i], dt_bias[i], a_log[i], d_skip[i])
        proj, dtt = _proj(xf, g_mix[i].reshape(1, D_MODEL), w_main, w_dtt, b_dtt, ct, sa, sb, seq,
                          tm=tm, tn=1024)
        proj3 = proj.reshape(b, seq, N_MAIN)
        y_ssd = _ssd(proj3, conv_w[i], conv_b[i].reshape(1, XBC_WIDTH), dtt, alog_col, p64)
        y_b = _dil_attn(proj3)
        y_c = _win_attn(proj3, sink[i].astype(F32))
        merged = _merge(y_ssd.reshape(m, SSD_INNER), proj, ssd_norm[i].reshape(1, SSD_INNER),
                        y_b.reshape(m, DIL_WIDTH), y_c.reshape(m, WIN_Q_HEADS * HEAD_DIM),
                        w_a[i].astype(BF16), w_b[i].astype(BF16), w_c[i].astype(BF16), tm=512, tn=512)
        xf = _matmul_res(merged, w_out[i].astype(BF16), xf, tm=tm, tn=1024, tk=D_MODEL)
        u = _mlp_up(xf, g_mlp[i].reshape(1, D_MODEL), w_up[i].astype(BF16), tm=tm, tn=1024)
        xf = _matmul_res(u, w_down[i].astype(BF16), xf, tm=tm, tn=1024, tk=2048)
    out = _final_norm(xf, g_final.reshape(1, D_MODEL), tm=min(512, seq))
    return out.reshape(b, seq, D_MODEL).astype(x.dtype)
```

```python
import functools

import jax
import jax.numpy as jnp
from jax import lax
from jax.experimental import pallas as pl
from jax.experimental.pallas import tpu as pltpu

F32 = jnp.float32
BF16 = jnp.bfloat16

D_MODEL = 2048
DEPTH = 2
SSD_HEADS = 32
SSD_HEAD_DIM = 64
SSD_INNER = SSD_HEADS * SSD_HEAD_DIM
SSD_GROUPS = 8
SSD_STATE = 128
SSD_CHUNK = 128
HEADS_PER_GROUP = SSD_HEADS // SSD_GROUPS
GROUP_WIDTH = HEADS_PER_GROUP * SSD_HEAD_DIM
CONV_WIDTH = 5
CONV_PAD = (CONV_WIDTH - 1) // 2
XBC_WIDTH = SSD_INNER + 2 * SSD_GROUPS * SSD_STATE
HEAD_DIM = 128
ROPE_DIM = HEAD_DIM // 4
ROPE_HALF = ROPE_DIM // 2
ROPE_THETA = 500000.0
DIL_PATTERNS = ((128, 1), (512, 4), (2048, 16))
DIL_GROUPS = len(DIL_PATTERNS)
DIL_HEADS = 8
DIL_WIDTH = DIL_HEADS * HEAD_DIM
DIL_HALF = 64
TILE_BATCH = 4
WIN_Q_HEADS = 16
WIN_KV_HEADS = 4
WIN_REP = WIN_Q_HEADS // WIN_KV_HEADS
WIN_HALF = 128
D_FF = 4 * D_MODEL
N_BRANCH = 3
EPS = 1e-6
NEG_INF = -1e30
LOG2E = 1.4426950408889634

SSD_SLAB = GROUP_WIDTH + 2 * SSD_STATE
WIN_SLAB = (WIN_REP + 2) * HEAD_DIM
DIL_SLAB = 3 * DIL_GROUPS * HEAD_DIM
OFF_Z = 0
OFF_SSD = OFF_Z + SSD_INNER
OFF_WIN = OFF_SSD + SSD_GROUPS * SSD_SLAB
OFF_DIL = OFF_WIN + WIN_KV_HEADS * WIN_SLAB
OFF_GATE = OFF_DIL + DIL_HEADS * DIL_SLAB
N_MAIN = OFF_GATE + N_BRANCH * D_MODEL
assert OFF_SSD % SSD_SLAB == 0 and OFF_WIN % WIN_SLAB == 0 and OFF_DIL % DIL_SLAB == 0
DT_COLS = 2 * SSD_HEADS
DT_IN_OFF = SSD_INNER + XBC_WIDTH
DT_LANES = 128

LANE = 128
SUBLANE = 8
VMEM_LIMIT = 56 * 1024 * 1024


def _split3(x):
    hi = x.astype(BF16)
    r1 = x - hi.astype(F32)
    mid = r1.astype(BF16)
    r2 = r1 - mid.astype(F32)
    return hi, mid, r2.astype(BF16)


def _dot(a, b):
    return jnp.dot(a, b, preferred_element_type=F32)


def _dot_nt(a, b):
    return lax.dot_general(a, b, (((1,), (1,)), ((), ())), preferred_element_type=F32)


def _exact_right(x, mat):
    hi, mid, lo = _split3(x)
    return _dot(hi, mat) + _dot(mid, mat) + _dot(lo, mat)


def _softplus(x):
    return jnp.maximum(x, 0.0) + jnp.log1p(jnp.exp(-jnp.abs(x)))


def _norm_rows(x, g):
    ms = jnp.mean(x * x, axis=-1, keepdims=True)
    return x * lax.rsqrt(ms + EPS) * g


def _rope_tile(a, ct, sa, sb):
    return a * ct + pltpu.roll(a, ROPE_HALF, 1) * sa + pltpu.roll(a, HEAD_DIM - ROPE_HALF, 1) * sb


def _proj_kernel(x_ref, g_ref, w_ref, wdtt_ref, bdtt_ref, ct_ref, sa_ref, sb_ref,
                 o_ref, dtt_ref, h_ref, *, tn, tile_kinds):
    j = pl.program_id(1)

    @pl.when(j == 0)
    def _():
        hb = _norm_rows(x_ref[...], g_ref[...]).astype(BF16)
        h_ref[...] = hb
        dtt_ref[...] = _softplus(_dot_nt(wdtt_ref[...], hb) + bdtt_ref[...])

    def tile_body(kind):
        acc = _dot(h_ref[...], w_ref[...])
        if 'r' in kind:
            ct, sa, sb = ct_ref[...], sa_ref[...], sb_ref[...]
        for c, k in enumerate(kind):
            sl = slice(c * HEAD_DIM, (c + 1) * HEAD_DIM)
            a = acc[:, sl]
            if k == 'r':
                a = _rope_tile(a, ct, sa, sb)
            elif k == 'g':
                a = 1.0 / (1.0 + jnp.exp(-a))
            o_ref[:, sl] = a.astype(o_ref.dtype)

    for kind in sorted(set(tile_kinds)):
        idx = [t for t, k in enumerate(tile_kinds) if k == kind]
        runs, start = [], idx[0]
        for a, b in zip(idx, idx[1:] + [None]):
            if b != a + 1:
                runs.append((start, a))
                start = b
        cond = None
        for lo, hi in runs:
            c = jnp.logical_and(j >= lo, j <= hi)
            cond = c if cond is None else jnp.logical_or(cond, c)
        pl.when(cond)(functools.partial(tile_body, kind))


def _proj_tile_kinds(tn):
    def head_kind(col):
        if col >= OFF_GATE:
            return 'g'
        if col >= OFF_DIL:
            return 'r'
        if col >= OFF_WIN:
            return 'r' if (col - OFF_WIN) % WIN_SLAB < (WIN_REP + 1) * HEAD_DIM else 'p'
        return 'p'
    return tuple(''.join(head_kind(t * tn + c * HEAD_DIM) for c in range(tn // HEAD_DIM))
                 for t in range(N_MAIN // tn))


def _proj(x2d, g, w_main, w_dtt, b_dtt, ct, sa, sb, seq, *, tm, tn):
    m = x2d.shape[0]
    assert seq % tm == 0 and m % tm == 0 and N_MAIN % tn == 0 and tn % HEAD_DIM == 0
    seq_tiles = seq // tm
    kern = functools.partial(_proj_kernel, tn=tn, tile_kinds=_proj_tile_kinds(tn))
    return pl.pallas_call(
        kern,
        grid=(m // tm, N_MAIN // tn),
        in_specs=[
            pl.BlockSpec((tm, D_MODEL), lambda i, j: (i, 0)),
            pl.BlockSpec((1, D_MODEL), lambda i, j: (0, 0)),
            pl.BlockSpec((D_MODEL, tn), lambda i, j: (0, j)),
            pl.BlockSpec((DT_LANES, D_MODEL), lambda i, j: (0, 0)),
            pl.BlockSpec((DT_LANES, 1), lambda i, j: (0, 0)),
            pl.BlockSpec((tm, HEAD_DIM), lambda i, j: (i % seq_tiles, 0)),
            pl.BlockSpec((tm, HEAD_DIM), lambda i, j: (i % seq_tiles, 0)),
            pl.BlockSpec((tm, HEAD_DIM), lambda i, j: (i % seq_tiles, 0)),
        ],
        out_specs=[
            pl.BlockSpec((tm, tn), lambda i, j: (i, j)),
            pl.BlockSpec((DT_LANES, tm), lambda i, j: (0, i)),
        ],
        out_shape=[
            jax.ShapeDtypeStruct((m, N_MAIN), BF16),
            jax.ShapeDtypeStruct((DT_LANES, m), F32),
        ],
        scratch_shapes=[pltpu.VMEM((tm, D_MODEL), BF16)],
        compiler_params=pltpu.CompilerParams(
            dimension_semantics=("parallel", "arbitrary"), vmem_limit_bytes=VMEM_LIMIT),
        name="proj",
    )(x2d, g, w_main, w_dtt, b_dtt, ct, sa, sb)


def _up_kernel(x_ref, g_ref, w_ref, o_ref, h_ref):
    @pl.when(pl.program_id(1) == 0)
    def _():
        h_ref[...] = _norm_rows(x_ref[...], g_ref[...]).astype(BF16)

    acc = jnp.maximum(_dot(h_ref[...], w_ref[...]), 0.0)
    o_ref[...] = (acc * acc).astype(o_ref.dtype)


def _mlp_up(x2d, g, w, *, tm, tn):
    m, n = x2d.shape[0], w.shape[1]
    return pl.pallas_call(
        _up_kernel,
        grid=(m // tm, n // tn),
        in_specs=[
            pl.BlockSpec((tm, D_MODEL), lambda i, j: (i, 0)),
            pl.BlockSpec((1, D_MODEL), lambda i, j: (0, 0)),
            pl.BlockSpec((D_MODEL, tn), lambda i, j: (0, j)),
        ],
        out_specs=pl.BlockSpec((tm, tn), lambda i, j: (i, j)),
        out_shape=jax.ShapeDtypeStruct((m, n), BF16),
        scratch_shapes=[pltpu.VMEM((tm, D_MODEL), BF16)],
        compiler_params=pltpu.CompilerParams(
            dimension_semantics=("parallel", "arbitrary"), vmem_limit_bytes=VMEM_LIMIT),
        name="mlp_up",
    )(x2d, g, w)


def _matmul_res_kernel(a_ref, w_ref, r_ref, o_ref, acc_ref):
    k = pl.program_id(2)

    @pl.when(k == 0)
    def _():
        acc_ref[...] = jnp.zeros_like(acc_ref)

    acc_ref[...] += _dot(a_ref[...], w_ref[...])

    @pl.when(k == pl.num_programs(2) - 1)
    def _():
        o_ref[...] = r_ref[...] + acc_ref[...]


def _matmul_res(a, w, res, *, tm, tn, tk):
    m, kdim = a.shape
    n = w.shape[1]
    return pl.pallas_call(
        _matmul_res_kernel,
        grid=(m // tm, n // tn, kdim // tk),
        in_specs=[
            pl.BlockSpec((tm, tk), lambda i, j, k: (i, k)),
            pl.BlockSpec((tk, tn), lambda i, j, k: (k, j)),
            pl.BlockSpec((tm, tn), lambda i, j, k: (i, j)),
        ],
        out_specs=pl.BlockSpec((tm, tn), lambda i, j, k: (i, j)),
        out_shape=jax.ShapeDtypeStruct((m, n), F32),
        scratch_shapes=[pltpu.VMEM((tm, tn), F32)],
        compiler_params=pltpu.CompilerParams(
            dimension_semantics=("parallel", "parallel", "arbitrary"), vmem_limit_bytes=VMEM_LIMIT),
        name="matmul_res",
    )(a, w, res)


def _merge_kernel(ys_ref, z_ref, nw_ref, yb_ref, yc_ref, g0_ref, g1_ref, g2_ref,
                  wa_ref, wb_ref, wc_ref, o_ref, ya_ref):
    @pl.when(pl.program_id(1) == 0)
    def _():
        z = z_ref[...].astype(F32)
        y = ys_ref[...].astype(F32) * (z / (1.0 + jnp.exp(-z)))
        ya_ref[...] = _norm_rows(y, nw_ref[...]).astype(BF16)

    acc = g0_ref[...].astype(F32) * _dot(ya_ref[...], wa_ref[...])
    acc += g1_ref[...].astype(F32) * _dot(yb_ref[...], wb_ref[...])
    acc += g2_ref[...].astype(F32) * _dot(yc_ref[...], wc_ref[...])
    o_ref[...] = acc.astype(o_ref.dtype)


def _merge(y_ssd, proj, norm_w, y_b, y_c, w_a, w_b, w_c, *, tm, tn):
    m = y_ssd.shape[0]
    gate_blk = OFF_GATE // tn
    per_gate = D_MODEL // tn
    return pl.pallas_call(
        _merge_kernel,
        grid=(m // tm, D_MODEL // tn),
        in_specs=[
            pl.BlockSpec((tm, SSD_INNER), lambda i, j: (i, 0)),
            pl.BlockSpec((tm, SSD_INNER), lambda i, j: (i, OFF_Z // SSD_INNER)),
            pl.BlockSpec((1, SSD_INNER), lambda i, j: (0, 0)),
            pl.BlockSpec((tm, DIL_WIDTH), lambda i, j: (i, 0)),
            pl.BlockSpec((tm, WIN_Q_HEADS * HEAD_DIM), lambda i, j: (i, 0)),
            pl.BlockSpec((tm, tn), lambda i, j: (i, gate_blk + j)),
            pl.BlockSpec((tm, tn), lambda i, j: (i, gate_blk + per_gate + j)),
            pl.BlockSpec((tm, tn), lambda i, j: (i, gate_blk + 2 * per_gate + j)),
            pl.BlockSpec((SSD_INNER, tn), lambda i, j: (0, j)),
            pl.BlockSpec((DIL_WIDTH, tn), lambda i, j: (0, j)),
            pl.BlockSpec((WIN_Q_HEADS * HEAD_DIM, tn), lambda i, j: (0, j)),
        ],
        out_specs=pl.BlockSpec((tm, tn), lambda i, j: (i, j)),
        out_shape=jax.ShapeDtypeStruct((m, D_MODEL), BF16),
        scratch_shapes=[pltpu.VMEM((tm, SSD_INNER), BF16)],
        compiler_params=pltpu.CompilerParams(
            dimension_semantics=("parallel", "arbitrary"), vmem_limit_bytes=VMEM_LIMIT),
        name="merge",
    )(y_ssd, proj, norm_w, y_b, y_c, proj, proj, proj, w_a, w_b, w_c)


def _final_norm_kernel(x_ref, g_ref, o_ref):
    o_ref[...] = _norm_rows(x_ref[...], g_ref[...])


def _final_norm(x2d, g, *, tm):
    m = x2d.shape[0]
    return pl.pallas_call(
        _final_norm_kernel,
        grid=(m // tm,),
        in_specs=[pl.BlockSpec((tm, D_MODEL), lambda i: (i, 0)),
                  pl.BlockSpec((1, D_MODEL), lambda i: (0, 0))],
        out_specs=pl.BlockSpec((tm, D_MODEL), lambda i: (i, 0)),
        out_shape=jax.ShapeDtypeStruct((m, D_MODEL), F32),
        compiler_params=pltpu.CompilerParams(dimension_semantics=("parallel",)),
        name="final_norm",
    )(x2d, g)


def _ssd_kernel(xbc_ref, cw_ref, cb_ref, dtt_ref, alt_ref, p64_ref,
                y_ref,
                pad_s, xs_s, bm_s, cm_s, bmt_s, dtr_s, y_s, st_s, tri_s, u_s, e_s, scl_s, *, seq):
    t = SSD_CHUNK
    n_chunks = seq // t
    for c in range(n_chunks):
        dtr_s[c] = dtt_ref[:, c * t:(c + 1) * t]

    halo = SUBLANE
    for lane_tile in range(SSD_SLAB // LANE):
        lanes = slice(lane_tile * LANE, (lane_tile + 1) * LANE)
        pad_s[lane_tile, 0:halo, :] = jnp.zeros((halo, LANE), F32)
        pad_s[lane_tile, halo + seq:2 * halo + seq, :] = jnp.zeros((halo, LANE), F32)
        pad_s[lane_tile, halo:halo + seq, :] = xbc_ref[0, :, lanes].astype(F32)

    def conv_silu(r0, lane_tile):
        lanes = slice(lane_tile * LANE, (lane_tile + 1) * LANE)
        acc = cb_ref[:, lanes]
        for k in range(CONV_WIDTH):
            lo = halo - CONV_PAD + k + r0
            acc = acc + cw_ref[k:k + 1, lanes] * pad_s[lane_tile, lo:lo + t, :]
        return acc / (1.0 + jnp.exp(-acc))

    x_tiles = GROUP_WIDTH // LANE
    for c in range(n_chunks):
        r0 = c * t
        rows = slice(r0, r0 + t)
        for lane_tile in range(x_tiles):
            lanes = slice(lane_tile * LANE, (lane_tile + 1) * LANE)
            xv = conv_silu(r0, lane_tile)
            xs_s[rows, lanes] = xv.astype(BF16)
            y_s[rows, lanes] = xv * p64_ref[0, 0:1, lanes]
        bv = conv_silu(r0, x_tiles)
        bm_s[rows, :] = bv.astype(BF16)
        bmt_s[c] = bv.T.astype(BF16)
        cm_s[rows, :] = conv_silu(r0, x_tiles + 1).astype(BF16)

    ri = lax.broadcasted_iota(jnp.int32, (t, t), 0)
    ci = lax.broadcasted_iota(jnp.int32, (t, t), 1)
    lower, upper = ri >= ci, ri <= ci
    tri_s[:, 0:t] = upper.astype(BF16)
    tri_s[:, t:2 * t] = lower.astype(BF16)
    st_s[...] = jnp.zeros_like(st_s)

    nh = HEADS_PER_GROUP
    a_rows = -jnp.exp(alt_ref[...])
    head_of_lane = lax.broadcasted_iota(jnp.int32, (1, GROUP_WIDTH), 1) // SSD_HEAD_DIM
    low_half = lax.broadcasted_iota(jnp.int32, (1, LANE), 1) < SSD_HEAD_DIM

    def lanes_from_row(row):
        return jnp.broadcast_to(row, (t, t)).T

    def expand64(cols):
        return jnp.concatenate([jnp.where(low_half, cols[0], cols[1]),
                                jnp.where(low_half, cols[2], cols[3])], axis=1)

    def prepare(chunks):
        loaded = []
        for c in chunks:
            rows = pl.ds(pl.multiple_of(c * t, t), t)
            loaded.append((c, rows, dtr_s[c], xs_s[rows, :], cm_s[rows, :], bm_s[rows, :], bmt_s[c]))
        cums = [_exact_right(dt_all, tri_s[...]) for _, _, dt_all, _, _, _, _ in loaded]
        cbs = [_dot_nt(cm_c, bm_c) for _, _, _, _, cm_c, bm_c, _ in loaded]
        staged = []
        for (c, rows, dt_all, xs_c, _, _, bmt_c), cum, cb in zip(loaded, cums, cbs):
            cs = (cum[0:nh, 0:t] * a_rows[0:nh], cum[nh:2 * nh, t:2 * t] * a_rows[nh:2 * nh])
            dts = (dt_all[0:nh, :], dt_all[nh:2 * nh, :])
            tots = (cs[0][:, t - 1:t], cs[1][:, 0:1])
            w_rows = [dts[d] * jnp.exp(tots[d] - cs[d]) for d in range(2)]
            grow = [jnp.exp(tots[d]) for d in range(2)]
            bmt_f = bmt_c.astype(F32)
            l_parts, bw, seen, scale = [], ([], []), ([], []), [0.0, 0.0]
            for h in range(nh):
                mix = jnp.zeros((t, t), F32)
                for d, valid in ((0, lower), (1, upper)):
                    row = cs[d][h:h + 1, :]
                    col = lanes_from_row(row)
                    mix = mix + jnp.exp(jnp.where(valid, col - row, NEG_INF)) * dts[d][h:h + 1, :]
                    seen[d].append(jnp.exp(col))
                    bw[d].append((bmt_f * w_rows[d][h:h + 1, :]).astype(BF16))
                    scale[d] = scale[d] + jnp.where(head_of_lane == h, grow[d][h:h + 1, :], 0.0)
                l_parts.append((cb * mix).astype(BF16))
            lhs = jnp.concatenate([jnp.concatenate(l_parts, axis=1), jnp.concatenate(bw[0], axis=1),
                                   jnp.concatenate(bw[1], axis=1)], axis=0)
            xs_stack = jnp.concatenate(
                [jnp.where(head_of_lane == h, xs_c, jnp.zeros_like(xs_c)) for h in range(nh)], axis=0)
            staged.append((c, rows, lhs, xs_stack, seen, scale))
        outs = [_dot(lhs, xs_stack) for _, _, lhs, xs_stack, _, _ in staged]
        for (c, rows, _, _, seen, scale), out in zip(staged, outs):
            y_s[rows, :] += out[0:t, :]
            for d in range(2):
                u_s[d, c] = out[(d + 1) * t:(d + 2) * t, :]
                e_s[d, c] = expand64(seen[d])
                scl_s[d, c] = jnp.broadcast_to(scale[d], (SUBLANE, GROUP_WIDTH))

    batch = 2
    def prep_body(i, carry):
        prepare([i * batch + u for u in range(batch)])
        return carry

    lax.fori_loop(0, n_chunks // batch, prep_body, 0)

    def scan_body(i, carry):
        steps = []
        for d, c in ((0, i), (1, n_chunks - 1 - i)):
            rows = pl.ds(pl.multiple_of(c * t, t), t)
            steps.append((d, c, rows, st_s[d], cm_s[rows, :]))
        reads = [_dot(cm_c, st.astype(BF16)) for _, _, _, st, cm_c in steps]
        for (d, c, rows, st, _), read in zip(steps, reads):
            y_s[rows, :] += read * e_s[d, c]
            st_s[d] = st * scl_s[d, c][0:1, :] + u_s[d, c]
        return carry

    lax.fori_loop(0, n_chunks, scan_body, 0, unroll=2)
    y_ref[0] = y_s[...].astype(y_ref.dtype)


def _ssd(proj3, conv_w, conv_b, dtt, alog_col, p64):
    b, seq, _ = proj3.shape
    gw, ns = GROUP_WIDTH, SSD_STATE
    slab0 = OFF_SSD // SSD_SLAB
    kern = functools.partial(_ssd_kernel, seq=seq)
    return pl.pallas_call(
        kern,
        grid=(b, SSD_GROUPS),
        in_specs=[
            pl.BlockSpec((1, seq, SSD_SLAB), lambda i, g: (i, 0, slab0 + g)),
            pl.BlockSpec((CONV_WIDTH, SSD_SLAB), lambda i, g: (0, g)),
            pl.BlockSpec((1, SSD_SLAB), lambda i, g: (0, g)),
            pl.BlockSpec((2 * HEADS_PER_GROUP, seq), lambda i, g: (g, i)),
            pl.BlockSpec((2 * HEADS_PER_GROUP, 1), lambda i, g: (g, 0)),
            pl.BlockSpec((1, SUBLANE, gw), lambda i, g: (g, 0, 0)),
        ],
        out_specs=pl.BlockSpec((1, seq, gw), lambda i, g: (i, 0, g)),
        out_shape=jax.ShapeDtypeStruct((b, seq, SSD_INNER), BF16),
        scratch_shapes=[
            pltpu.VMEM((SSD_SLAB // LANE, seq + 2 * SUBLANE, LANE), F32),
            pltpu.VMEM((seq, gw), BF16),
            pltpu.VMEM((seq, ns), BF16),
            pltpu.VMEM((seq, ns), BF16),
            pltpu.VMEM((seq // SSD_CHUNK, ns, SSD_CHUNK), BF16),
            pltpu.VMEM((seq // SSD_CHUNK, 2 * HEADS_PER_GROUP, SSD_CHUNK), F32),
            pltpu.VMEM((seq, gw), F32),
            pltpu.VMEM((2, ns, gw), F32),
            pltpu.VMEM((SSD_CHUNK, 2 * SSD_CHUNK), BF16),
            pltpu.VMEM((2, seq // SSD_CHUNK, ns, gw), F32),
            pltpu.VMEM((2, seq // SSD_CHUNK, SSD_CHUNK, gw), F32),
            pltpu.VMEM((2, seq // SSD_CHUNK, SUBLANE, gw), F32),
        ],
        compiler_params=pltpu.CompilerParams(
            dimension_semantics=("parallel", "arbitrary"), vmem_limit_bytes=VMEM_LIMIT),
        name="ssd",
    )(proj3, conv_w, conv_b, dtt, alog_col, p64)


def _win_kernel(sink_ref, qkv_ref, o_ref, bias_s, vext_s, *, seq):
    q_cols = slice(0, WIN_REP * HEAD_DIM)
    k_cols = slice(WIN_REP * HEAD_DIM, (WIN_REP + 1) * HEAD_DIM)
    v_cols = slice((WIN_REP + 1) * HEAD_DIM, (WIN_REP + 2) * HEAD_DIM)
    blk = WIN_HALF
    kwin = 3 * blk
    n_blk = seq // blk
    kv = pl.program_id(1)
    scale2 = (HEAD_DIM ** -0.5) * LOG2E

    ri = lax.broadcasted_iota(jnp.int32, (blk, kwin), 0)
    ci = lax.broadcasted_iota(jnp.int32, (blk, kwin), 1)
    for w, shift in enumerate((0, -blk, -2 * blk)):
        bias_s[w] = jnp.where(jnp.abs(ci + shift - ri) <= WIN_HALF, 0.0, NEG_INF).astype(F32)
    vext_s[:, 0:HEAD_DIM] = qkv_ref[0, :, v_cols]
    vext_s[:, HEAD_DIM:2 * HEAD_DIM] = jnp.ones((seq, HEAD_DIM), BF16)
    sinks = [sink_ref[kv * WIN_REP + r] * LOG2E for r in range(WIN_REP)]

    def blocks(specs):
        loaded = []
        for qi, which in specs:
            q0 = qi * blk if isinstance(qi, int) else pl.multiple_of(qi * blk, blk)
            k0 = (0, q0 - blk, seq - kwin)[which]
            if not isinstance(k0, int):
                k0 = pl.multiple_of(k0, blk)
            loaded.append((q0, qkv_ref[0, pl.ds(q0, blk), q_cols], qkv_ref[0, pl.ds(k0, kwin), k_cols],
                           vext_s[pl.ds(k0, kwin), :], bias_s[which]))
        scores = [[_dot_nt(qt[:, r * HEAD_DIM:(r + 1) * HEAD_DIM], kt) for r in range(WIN_REP)]
                  for _, qt, kt, _, _ in loaded]
        maxes, probs = [], []
        for heads, (_, _, _, _, bias) in zip(scores, loaded):
            for r, s in enumerate(heads):
                s = s * scale2 + bias
                m = jnp.maximum(jnp.max(s, axis=-1, keepdims=True), sinks[r])
                maxes.append(m)
                probs.append(jnp.exp2(s - m).astype(BF16))
        accs = [_dot(probs[i * WIN_REP + r], vt)
                for i, (_, _, _, vt, _) in enumerate(loaded) for r in range(WIN_REP)]
        for i, (q0, _, _, _, _) in enumerate(loaded):
            for r in range(WIN_REP):
                acc, m = accs[i * WIN_REP + r], maxes[i * WIN_REP + r]
                den = acc[:, HEAD_DIM:] + jnp.exp2(sinks[r] - m)
                o_ref[0, pl.ds(q0, blk), r * HEAD_DIM:(r + 1) * HEAD_DIM] = (
                    acc[:, :HEAD_DIM] / den).astype(o_ref.dtype)

    blocks([(0, 0), (n_blk - 1, 2)])
    pairs = (n_blk - 2) // 2

    def body(pi, carry):
        blocks([(1 + 2 * pi, 1), (2 + 2 * pi, 1)])
        return carry

    lax.fori_loop(0, pairs, body, 0)
    if (n_blk - 2) % 2:
        blocks([(n_blk - 2, 1)])


def _win_attn(proj3, sink):
    b, seq, _ = proj3.shape
    assert seq >= 3 * WIN_HALF and seq % WIN_HALF == 0
    qw = WIN_REP * HEAD_DIM
    kern = functools.partial(_win_kernel, seq=seq)
    return pl.pallas_call(
        kern,
        grid=(b, WIN_KV_HEADS),
        in_specs=[
            pl.BlockSpec(memory_space=pltpu.SMEM),
            pl.BlockSpec((1, seq, WIN_SLAB), lambda i, g: (i, 0, OFF_WIN // WIN_SLAB + g)),
        ],
        out_specs=pl.BlockSpec((1, seq, qw), lambda i, g: (i, 0, g)),
        out_shape=jax.ShapeDtypeStruct((b, seq, WIN_Q_HEADS * HEAD_DIM), BF16),
        scratch_shapes=[pltpu.VMEM((3, WIN_HALF, 3 * WIN_HALF), F32),
                        pltpu.VMEM((seq, 2 * HEAD_DIM), BF16)],
        compiler_params=pltpu.CompilerParams(
            dimension_semantics=("parallel", "parallel"), vmem_limit_bytes=VMEM_LIMIT),
        name="win_attn",
    )(sink, proj3)


def _dil_kernel(qkv_ref, o_ref, stage_s, qd_s, kd_s, vext_s, od_s, ld_s, og_s, lg_s, bias_s, bias1_s,
                *, seq):
    def cols(part, gi):
        c0 = (part * DIL_GROUPS + gi) * HEAD_DIM
        return slice(c0, c0 + HEAD_DIM)

    tq = 2 * DIL_HALF
    kw = tq + 2 * DIL_HALF
    scale2 = (HEAD_DIM ** -0.5) * LOG2E

    ri = lax.broadcasted_iota(jnp.int32, (tq, kw), 0)
    ci = lax.broadcasted_iota(jnp.int32, (tq, kw), 1)
    for w, shift in enumerate((0, -DIL_HALF, -2 * DIL_HALF)):
        bias_s[w] = jnp.where(jnp.abs(ci + shift - ri) <= DIL_HALF, 0.0, NEG_INF).astype(F32)
    ri1 = lax.broadcasted_iota(jnp.int32, (tq, tq), 0)
    ci1 = lax.broadcasted_iota(jnp.int32, (tq, tq), 1)
    bias1_s[...] = jnp.where(jnp.abs(ci1 - ri1) <= DIL_HALF, 0.0, NEG_INF).astype(F32)
    vext_s[:, HEAD_DIM:2 * HEAD_DIM] = jnp.ones((seq, HEAD_DIM), BF16)

    def deinterleave(src_cols, dst_ref, dil):
        n_sub = seq // dil
        stage_s[...] = qkv_ref[0, :, src_cols].astype(F32)
        for r in range(dil):
            dst_ref[r * n_sub:(r + 1) * n_sub, 0:HEAD_DIM] = (
                stage_s[pl.ds(r, n_sub, stride=dil), :].astype(BF16))

    def aligned(x, mult):
        return x if isinstance(x, int) else pl.multiple_of(x, mult)

    def run_group(q_at, k_at, n_sub):
        def tiles(specs):
            loaded = []
            for q0, k0, width, bias in specs:
                q0, k0 = aligned(q0, tq), aligned(k0, DIL_HALF)
                loaded.append((q0, q_at(pl.ds(q0, tq)), k_at(pl.ds(k0, width)),
                               vext_s[pl.ds(k0, width), :], bias))
            scores = [_dot_nt(qt, kt) for _, qt, kt, _, _ in loaded]
            maxes, probs = [], []
            for s, (_, _, _, _, bias) in zip(scores, loaded):
                s = s * scale2 + bias
                m = jnp.max(s, axis=-1, keepdims=True)
                maxes.append(m)
                probs.append(jnp.exp2(s - m).astype(BF16))
            accs = [_dot(p, vt) for p, (_, _, _, vt, _) in zip(probs, loaded)]
            for acc, m, (q0, _, _, _, _) in zip(accs, maxes, loaded):
                den = acc[:, HEAD_DIM:]
                od_s[pl.ds(q0, tq), :] = acc[:, :HEAD_DIM] / den
                ld_s[pl.ds(q0, tq), :] = m + jnp.log2(den)

        def first(seg0):
            return (seg0, seg0, kw, bias_s[0])

        def inner(q0):
            return (q0, q0 - DIL_HALF, kw, bias_s[1])

        def last(seg0):
            return (seg0 + n_sub - tq, seg0 + n_sub - kw, kw, bias_s[2])

        tiles_per_seg, n_seg, nb = n_sub // tq, seq // n_sub, TILE_BATCH
        if tiles_per_seg == 1:
            def body1(bi, carry):
                tiles([((bi * nb + u) * tq, (bi * nb + u) * tq, tq, bias1_s[...]) for u in range(nb)])
                return carry
            lax.fori_loop(0, seq // (tq * nb), body1, 0)
        elif tiles_per_seg == nb:
            def seg_body(si, carry):
                seg0 = si * n_sub
                tiles([first(seg0)] + [inner(seg0 + ti * tq) for ti in range(1, nb - 1)] + [last(seg0)])
                return carry
            lax.fori_loop(0, n_seg, seg_body, 0)
        else:
            assert n_seg == 1
            loops = (tiles_per_seg - 2) // nb
            tiles([first(0)])
            def inner_body(bi, carry):
                tiles([inner((1 + bi * nb + u) * tq) for u in range(nb)])
                return carry
            lax.fori_loop(0, loops, inner_body, 0)
            tiles([inner(ti * tq) for ti in range(1 + loops * nb, tiles_per_seg - 1)] + [last(0)])

    for gi, (_, dil) in enumerate(DIL_PATTERNS):
        n_sub = seq // dil
        if dil == 1:
            vext_s[:, 0:HEAD_DIM] = qkv_ref[0, :, cols(2, gi)]
            run_group(lambda sl: qkv_ref[0, sl, cols(0, gi)], lambda sl: qkv_ref[0, sl, cols(1, gi)], n_sub)
            og_s[gi] = od_s[...]
            lg_s[gi] = ld_s[...]
        else:
            deinterleave(cols(0, gi), qd_s, dil)
            deinterleave(cols(1, gi), kd_s, dil)
            deinterleave(cols(2, gi), vext_s, dil)
            run_group(lambda sl: qd_s[sl, :], lambda sl: kd_s[sl, :], n_sub)
            for r in range(dil):
                og_s[gi, pl.ds(r, n_sub, stride=dil), :] = od_s[r * n_sub:(r + 1) * n_sub, :]
                lg_s[gi, pl.ds(r, n_sub, stride=dil), :] = ld_s[r * n_sub:(r + 1) * n_sub, :]

    m = jnp.maximum(jnp.maximum(lg_s[0], lg_s[1]), lg_s[2])
    num = jnp.zeros((seq, HEAD_DIM), F32)
    den = jnp.zeros((seq, HEAD_DIM), F32)
    for gi in range(DIL_GROUPS):
        w = jnp.exp2(lg_s[gi] - m)
        num = num + w * og_s[gi]
        den = den + w
    o_ref[0] = (num / den).astype(o_ref.dtype)


def _dil_attn(proj3):
    b, seq, _ = proj3.shape
    tq = 2 * DIL_HALF
    for _, dil in DIL_PATTERNS:
        assert (seq // dil) % tq == 0
    kern = functools.partial(_dil_kernel, seq=seq)
    return pl.pallas_call(
        kern,
        grid=(b, DIL_HEADS),
        in_specs=[pl.BlockSpec((1, seq, DIL_SLAB), lambda i, h: (i, 0, OFF_DIL // DIL_SLAB + h))],
        out_specs=pl.BlockSpec((1, seq, HEAD_DIM), lambda i, h: (i, 0, h)),
        out_shape=jax.ShapeDtypeStruct((b, seq, DIL_WIDTH), BF16),
        scratch_shapes=[
            pltpu.VMEM((seq, HEAD_DIM), F32),
            pltpu.VMEM((seq, HEAD_DIM), BF16),
            pltpu.VMEM((seq, HEAD_DIM), BF16),
            pltpu.VMEM((seq, 2 * HEAD_DIM), BF16),
            pltpu.VMEM((seq, HEAD_DIM), F32),
            pltpu.VMEM((seq, HEAD_DIM), F32),
            pltpu.VMEM((DIL_GROUPS, seq, HEAD_DIM), F32),
            pltpu.VMEM((DIL_GROUPS, seq, HEAD_DIM), F32),
            pltpu.VMEM((3, tq, tq + 2 * DIL_HALF), F32),
            pltpu.VMEM((tq, tq), F32),
        ],
        compiler_params=pltpu.CompilerParams(
            dimension_semantics=("parallel", "parallel"), vmem_limit_bytes=VMEM_LIMIT),
        name="dil_attn",
    )(proj3)


def _rope_tables(seq):
    inv = ROPE_THETA ** (-jnp.arange(0, ROPE_DIM, 2, dtype=F32) / ROPE_DIM)
    ang = jnp.arange(seq, dtype=F32)[:, None] * inv[None, :]
    cos, sin = jnp.cos(ang), jnp.sin(ang)
    zeros = jnp.zeros((seq, HEAD_DIM - ROPE_DIM), F32)
    zh = jnp.zeros_like(sin)
    ct = jnp.concatenate([cos, cos, jnp.ones_like(zeros)], axis=1)
    sa = jnp.concatenate([zh, sin, zeros], axis=1)
    sb = jnp.concatenate([-sin, zh, zeros], axis=1)
    return ct, sa, sb


def _group_major(v):
    return v.reshape(2, SSD_GROUPS, HEADS_PER_GROUP).transpose(1, 0, 2).reshape(DT_COLS)


def _ssd_slabs(a):
    rows = a.shape[0]
    x = a[:, :SSD_INNER].reshape(rows, SSD_GROUPS, GROUP_WIDTH)
    bm = a[:, SSD_INNER:SSD_INNER + SSD_GROUPS * SSD_STATE].reshape(rows, SSD_GROUPS, SSD_STATE)
    cm = a[:, SSD_INNER + SSD_GROUPS * SSD_STATE:].reshape(rows, SSD_GROUPS, SSD_STATE)
    return jnp.concatenate([x, bm, cm], axis=2).reshape(rows, XBC_WIDTH)


def _main_weight(w_in):
    d = w_in.shape[0]
    o = DT_IN_OFF + DT_COLS
    n_dil = 3 * DIL_GROUPS * DIL_WIDTH
    n_q, n_kv = WIN_Q_HEADS * HEAD_DIM, WIN_KV_HEADS * HEAD_DIM
    z = w_in[:, :SSD_INNER]
    ssd = _ssd_slabs(w_in[:, SSD_INNER:DT_IN_OFF])
    dil = w_in[:, o:o + n_dil].reshape(d, 3 * DIL_GROUPS, DIL_HEADS, HEAD_DIM)
    dil = dil.transpose(0, 2, 1, 3).reshape(d, n_dil)
    q = w_in[:, o + n_dil:o + n_dil + n_q].reshape(d, WIN_KV_HEADS, WIN_REP * HEAD_DIM)
    k = w_in[:, o + n_dil + n_q:o + n_dil + n_q + n_kv].reshape(d, WIN_KV_HEADS, HEAD_DIM)
    v = w_in[:, o + n_dil + n_q + n_kv:o + n_dil + n_q + 2 * n_kv].reshape(d, WIN_KV_HEADS, HEAD_DIM)
    win = jnp.concatenate([q, k, v], axis=2).reshape(d, WIN_KV_HEADS * WIN_SLAB)
    gates = w_in[:, o + n_dil + n_q + 2 * n_kv:]
    return jnp.concatenate([z, ssd, win, dil, gates], axis=1).astype(BF16)


def _layer_params(w_in, dt_bias, a_log, d_skip):
    w_main = _main_weight(w_in)
    w_dt = w_in[:, DT_IN_OFF:DT_IN_OFF + DT_COLS]
    w_dt = w_dt.reshape(D_MODEL, 2, SSD_GROUPS, HEADS_PER_GROUP).transpose(0, 2, 1, 3).reshape(D_MODEL, DT_COLS)
    w_dtt = jnp.pad(w_dt, ((0, 0), (0, DT_LANES - DT_COLS))).astype(BF16).T
    b_dt = jnp.pad(_group_major(dt_bias.astype(F32)), (0, DT_LANES - DT_COLS))
    alog = jnp.pad(_group_major(a_log.astype(F32)), (0, DT_LANES - DT_COLS))
    p64 = jnp.repeat(d_skip.astype(F32).reshape(SSD_GROUPS, 1, HEADS_PER_GROUP), SSD_HEAD_DIM, axis=2)
    p64 = jnp.pad(p64, ((0, 0), (0, SUBLANE - 1), (0, 0)))
    return w_main, w_dtt, b_dt.reshape(DT_LANES, 1), alog.reshape(DT_LANES, 1), p64


def kernel(x, g_mix, w_in, conv_w, conv_b, dt_bias, a_log, d_skip, ssd_norm, w_a, w_b, w_c, sink,
           w_out, g_mlp, w_up, w_down, g_final):
    b, seq, _ = x.shape
    m = b * seq
    tm = min(1024, seq)
    ct, sa, sb = _rope_tables(seq)
    xf = x.reshape(m, D_MODEL).astype(F32)
    for i in range(DEPTH):
        w_main, w_dtt, b_dtt, alog_col, p64 = _layer_params(w_in[i], dt_bias[i], a_log[i], d_skip[i])
        proj, dtt = _proj(xf, g_mix[i].reshape(1, D_MODEL), w_main, w_dtt, b_dtt, ct, sa, sb, seq,
                          tm=tm, tn=1024)
        proj3 = proj.reshape(b, seq, N_MAIN)
        y_ssd = _ssd(proj3, _ssd_slabs(conv_w[i].astype(F32)),
                     _ssd_slabs(conv_b[i].astype(F32).reshape(1, XBC_WIDTH)), dtt, alog_col, p64)
        y_b = _dil_attn(proj3)
        y_c = _win_attn(proj3, sink[i].astype(F32))
        merged = _merge(y_ssd.reshape(m, SSD_INNER), proj, ssd_norm[i].reshape(1, SSD_INNER),
                        y_b.reshape(m, DIL_WIDTH), y_c.reshape(m, WIN_Q_HEADS * HEAD_DIM),
                        w_a[i].astype(BF16), w_b[i].astype(BF16), w_c[i].astype(BF16), tm=tm, tn=256)
        xf = _matmul_res(merged, w_out[i].astype(BF16), xf, tm=tm, tn=1024, tk=D_MODEL)
        u = _mlp_up(xf, g_mlp[i].reshape(1, D_MODEL), w_up[i].astype(BF16), tm=tm, tn=1024)
        xf = _matmul_res(u, w_down[i].astype(BF16), xf, tm=tm, tn=1024, tk=2048)
    out = _final_norm(xf, g_final.reshape(1, D_MODEL), tm=min(512, seq))
    return out.reshape(b, seq, D_MODEL).astype(x.dtype)
```

```python
import functools

import jax
import jax.numpy as jnp
from jax import lax
from jax.experimental import pallas as pl
from jax.experimental.pallas import tpu as pltpu

F32 = jnp.float32
BF16 = jnp.bfloat16

D_MODEL = 2048
DEPTH = 2
SSD_HEADS = 32
SSD_HEAD_DIM = 64
SSD_INNER = SSD_HEADS * SSD_HEAD_DIM
SSD_GROUPS = 8
SSD_STATE = 128
SSD_CHUNK = 128
HEADS_PER_GROUP = SSD_HEADS // SSD_GROUPS
GROUP_WIDTH = HEADS_PER_GROUP * SSD_HEAD_DIM
CONV_WIDTH = 5
CONV_PAD = (CONV_WIDTH - 1) // 2
XBC_WIDTH = SSD_INNER + 2 * SSD_GROUPS * SSD_STATE
HEAD_DIM = 128
ROPE_DIM = HEAD_DIM // 4
ROPE_HALF = ROPE_DIM // 2
ROPE_THETA = 500000.0
DIL_PATTERNS = ((128, 1), (512, 4), (2048, 16))
DIL_GROUPS = len(DIL_PATTERNS)
DIL_HEADS = 8
DIL_WIDTH = DIL_HEADS * HEAD_DIM
DIL_HALF = 64
TILE_BATCH = 7
SEGMENT_TILE_BATCH = 8
WIN_Q_HEADS = 16
WIN_KV_HEADS = 4
WIN_REP = WIN_Q_HEADS // WIN_KV_HEADS
WIN_HALF = 128
D_FF = 4 * D_MODEL
N_BRANCH = 3
EPS = 1e-6
NEG_INF = -1e30
LOG2E = 1.4426950408889634

SSD_SLAB = GROUP_WIDTH + 2 * SSD_STATE
WIN_SLAB = (WIN_REP + 2) * HEAD_DIM
DIL_SLAB = 3 * DIL_GROUPS * HEAD_DIM
OFF_Z = 0
OFF_SSD = OFF_Z + SSD_INNER
OFF_WIN = OFF_SSD + SSD_GROUPS * SSD_SLAB
OFF_DIL = OFF_WIN + WIN_KV_HEADS * WIN_SLAB
OFF_GATE = OFF_DIL + DIL_HEADS * DIL_SLAB
N_MAIN = OFF_GATE + N_BRANCH * D_MODEL
assert OFF_SSD % SSD_SLAB == 0 and OFF_WIN % WIN_SLAB == 0 and OFF_DIL % DIL_SLAB == 0
DT_COLS = 2 * SSD_HEADS
DT_IN_OFF = SSD_INNER + XBC_WIDTH
DT_LANES = 128

LANE = 128
SUBLANE = 8
VMEM_LIMIT = 56 * 1024 * 1024


def _split3(x):
    hi = x.astype(BF16)
    r1 = x - hi.astype(F32)
    mid = r1.astype(BF16)
    r2 = r1 - mid.astype(F32)
    return hi, mid, r2.astype(BF16)


def _dot(a, b):
    return jnp.dot(a, b, preferred_element_type=F32)


def _dot_nt(a, b):
    return lax.dot_general(a, b, (((1,), (1,)), ((), ())), preferred_element_type=F32)


def _exact_right(x, mat):
    hi, mid, lo = _split3(x)
    return _dot(hi, mat) + _dot(mid, mat) + _dot(lo, mat)


def _softplus(x):
    return jnp.maximum(x, 0.0) + jnp.log1p(jnp.exp(-jnp.abs(x)))


def _norm_rows(x, g):
    ms = jnp.mean(x * x, axis=-1, keepdims=True)
    return x * lax.rsqrt(ms + EPS) * g


def _rope_tile(a, ct, sa, sb):
    return a * ct + pltpu.roll(a, ROPE_HALF, 1) * sa + pltpu.roll(a, HEAD_DIM - ROPE_HALF, 1) * sb


def _proj_kernel(x_ref, g_ref, w_ref, wdtt_ref, bdtt_ref, ct_ref, sa_ref, sb_ref,
                 o_ref, dtt_ref, h_ref, *, tn, tile_kinds):
    j = pl.program_id(1)

    @pl.when(j == 0)
    def _():
        hb = _norm_rows(x_ref[...], g_ref[...]).astype(BF16)
        h_ref[...] = hb
        dtt_ref[...] = _softplus(_dot_nt(wdtt_ref[...], hb) + bdtt_ref[...])

    def tile_body(kind):
        acc = _dot(h_ref[...], w_ref[...])
        if 'r' in kind:
            ct, sa, sb = ct_ref[...], sa_ref[...], sb_ref[...]
        for c, k in enumerate(kind):
            sl = slice(c * HEAD_DIM, (c + 1) * HEAD_DIM)
            a = acc[:, sl]
            if k == 'r':
                a = _rope_tile(a, ct, sa, sb)
            elif k == 'g':
                a = 1.0 / (1.0 + jnp.exp(-a))
            o_ref[:, sl] = a.astype(o_ref.dtype)

    for kind in sorted(set(tile_kinds)):
        idx = [t for t, k in enumerate(tile_kinds) if k == kind]
        runs, start = [], idx[0]
        for a, b in zip(idx, idx[1:] + [None]):
            if b != a + 1:
                runs.append((start, a))
                start = b
        cond = None
        for lo, hi in runs:
            c = jnp.logical_and(j >= lo, j <= hi)
            cond = c if cond is None else jnp.logical_or(cond, c)
        pl.when(cond)(functools.partial(tile_body, kind))


def _proj_tile_kinds(tn):
    def head_kind(col):
        if col >= OFF_GATE:
            return 'g'
        if col >= OFF_DIL:
            return 'r'
        if col >= OFF_WIN:
            return 'r' if (col - OFF_WIN) % WIN_SLAB < (WIN_REP + 1) * HEAD_DIM else 'p'
        return 'p'
    return tuple(''.join(head_kind(t * tn + c * HEAD_DIM) for c in range(tn // HEAD_DIM))
                 for t in range(N_MAIN // tn))


def _proj(x2d, g, w_main, w_dtt, b_dtt, ct, sa, sb, seq, *, tm, tn):
    m = x2d.shape[0]
    assert seq % tm == 0 and m % tm == 0 and N_MAIN % tn == 0 and tn % HEAD_DIM == 0
    seq_tiles = seq // tm
    kern = functools.partial(_proj_kernel, tn=tn, tile_kinds=_proj_tile_kinds(tn))
    return pl.pallas_call(
        kern,
        grid=(m // tm, N_MAIN // tn),
        in_specs=[
            pl.BlockSpec((tm, D_MODEL), lambda i, j: (i, 0)),
            pl.BlockSpec((1, D_MODEL), lambda i, j: (0, 0)),
            pl.BlockSpec((D_MODEL, tn), lambda i, j: (0, j)),
            pl.BlockSpec((DT_LANES, D_MODEL), lambda i, j: (0, 0)),
            pl.BlockSpec((DT_LANES, 1), lambda i, j: (0, 0)),
            pl.BlockSpec((tm, HEAD_DIM), lambda i, j: (i % seq_tiles, 0)),
            pl.BlockSpec((tm, HEAD_DIM), lambda i, j: (i % seq_tiles, 0)),
            pl.BlockSpec((tm, HEAD_DIM), lambda i, j: (i % seq_tiles, 0)),
        ],
        out_specs=[
            pl.BlockSpec((tm, tn), lambda i, j: (i, j)),
            pl.BlockSpec((DT_LANES, tm), lambda i, j: (0, i)),
        ],
        out_shape=[
            jax.ShapeDtypeStruct((m, N_MAIN), BF16),
            jax.ShapeDtypeStruct((DT_LANES, m), F32),
        ],
        scratch_shapes=[pltpu.VMEM((tm, D_MODEL), BF16)],
        compiler_params=pltpu.CompilerParams(
            dimension_semantics=("parallel", "arbitrary"), vmem_limit_bytes=VMEM_LIMIT),
        name="proj",
    )(x2d, g, w_main, w_dtt, b_dtt, ct, sa, sb)


def _up_kernel(x_ref, g_ref, w_ref, o_ref, h_ref):
    @pl.when(pl.program_id(1) == 0)
    def _():
        h_ref[...] = _norm_rows(x_ref[...], g_ref[...]).astype(BF16)

    acc = jnp.maximum(_dot(h_ref[...], w_ref[...]), 0.0)
    o_ref[...] = (acc * acc).astype(o_ref.dtype)


def _mlp_up(x2d, g, w, *, tm, tn):
    m, n = x2d.shape[0], w.shape[1]
    return pl.pallas_call(
        _up_kernel,
        grid=(m // tm, n // tn),
        in_specs=[
            pl.BlockSpec((tm, D_MODEL), lambda i, j: (i, 0)),
            pl.BlockSpec((1, D_MODEL), lambda i, j: (0, 0)),
            pl.BlockSpec((D_MODEL, tn), lambda i, j: (0, j)),
        ],
        out_specs=pl.BlockSpec((tm, tn), lambda i, j: (i, j)),
        out_shape=jax.ShapeDtypeStruct((m, n), BF16),
        scratch_shapes=[pltpu.VMEM((tm, D_MODEL), BF16)],
        compiler_params=pltpu.CompilerParams(
            dimension_semantics=("parallel", "arbitrary"), vmem_limit_bytes=VMEM_LIMIT),
        name="mlp_up",
    )(x2d, g, w)


def _matmul_res_kernel(a_ref, w_ref, r_ref, o_ref, acc_ref):
    k = pl.program_id(2)

    @pl.when(k == 0)
    def _():
        acc_ref[...] = jnp.zeros_like(acc_ref)

    acc_ref[...] += _dot(a_ref[...], w_ref[...])

    @pl.when(k == pl.num_programs(2) - 1)
    def _():
        o_ref[...] = r_ref[...] + acc_ref[...]


def _matmul_res(a, w, res, *, tm, tn, tk):
    m, kdim = a.shape
    n = w.shape[1]
    return pl.pallas_call(
        _matmul_res_kernel,
        grid=(m // tm, n // tn, kdim // tk),
        in_specs=[
            pl.BlockSpec((tm, tk), lambda i, j, k: (i, k)),
            pl.BlockSpec((tk, tn), lambda i, j, k: (k, j)),
            pl.BlockSpec((tm, tn), lambda i, j, k: (i, j)),
        ],
        out_specs=pl.BlockSpec((tm, tn), lambda i, j, k: (i, j)),
        out_shape=jax.ShapeDtypeStruct((m, n), F32),
        scratch_shapes=[pltpu.VMEM((tm, tn), F32)],
        compiler_params=pltpu.CompilerParams(
            dimension_semantics=("parallel", "parallel", "arbitrary"), vmem_limit_bytes=VMEM_LIMIT),
        name="matmul_res",
    )(a, w, res)


def _merge_kernel(ys_ref, z_ref, nw_ref, yb_ref, yc_ref, g0_ref, g1_ref, g2_ref,
                  wa_ref, wb_ref, wc_ref, o_ref, ya_ref):
    @pl.when(pl.program_id(1) == 0)
    def _():
        z = z_ref[...].astype(F32)
        y = ys_ref[...].astype(F32) * (z / (1.0 + jnp.exp(-z)))
        ya_ref[...] = _norm_rows(y, nw_ref[...]).astype(BF16)

    acc = g0_ref[...].astype(F32) * _dot(ya_ref[...], wa_ref[...])
    acc += g1_ref[...].astype(F32) * _dot(yb_ref[...], wb_ref[...])
    acc += g2_ref[...].astype(F32) * _dot(yc_ref[...], wc_ref[...])
    o_ref[...] = acc.astype(o_ref.dtype)


def _merge(y_ssd, proj, norm_w, y_b, y_c, w_a, w_b, w_c, *, tm, tn):
    m = y_ssd.shape[0]
    gate_blk = OFF_GATE // tn
    per_gate = D_MODEL // tn
    return pl.pallas_call(
        _merge_kernel,
        grid=(m // tm, D_MODEL // tn),
        in_specs=[
            pl.BlockSpec((tm, SSD_INNER), lambda i, j: (i, 0)),
            pl.BlockSpec((tm, SSD_INNER), lambda i, j: (i, OFF_Z // SSD_INNER)),
            pl.BlockSpec((1, SSD_INNER), lambda i, j: (0, 0)),
            pl.BlockSpec((tm, DIL_WIDTH), lambda i, j: (i, 0)),
            pl.BlockSpec((tm, WIN_Q_HEADS * HEAD_DIM), lambda i, j: (i, 0)),
            pl.BlockSpec((tm, tn), lambda i, j: (i, gate_blk + j)),
            pl.BlockSpec((tm, tn), lambda i, j: (i, gate_blk + per_gate + j)),
            pl.BlockSpec((tm, tn), lambda i, j: (i, gate_blk + 2 * per_gate + j)),
            pl.BlockSpec((SSD_INNER, tn), lambda i, j: (0, j)),
            pl.BlockSpec((DIL_WIDTH, tn), lambda i, j: (0, j)),
            pl.BlockSpec((WIN_Q_HEADS * HEAD_DIM, tn), lambda i, j: (0, j)),
        ],
        out_specs=pl.BlockSpec((tm, tn), lambda i, j: (i, j)),
        out_shape=jax.ShapeDtypeStruct((m, D_MODEL), BF16),
        scratch_shapes=[pltpu.VMEM((tm, SSD_INNER), BF16)],
        compiler_params=pltpu.CompilerParams(
            dimension_semantics=("parallel", "arbitrary"), vmem_limit_bytes=VMEM_LIMIT),
        name="merge",
    )(y_ssd, proj, norm_w, y_b, y_c, proj, proj, proj, w_a, w_b, w_c)


def _final_norm_kernel(x_ref, g_ref, o_ref):
    o_ref[...] = _norm_rows(x_ref[...], g_ref[...])


def _final_norm(x2d, g, *, tm):
    m = x2d.shape[0]
    return pl.pallas_call(
        _final_norm_kernel,
        grid=(m // tm,),
        in_specs=[pl.BlockSpec((tm, D_MODEL), lambda i: (i, 0)),
                  pl.BlockSpec((1, D_MODEL), lambda i: (0, 0))],
        out_specs=pl.BlockSpec((tm, D_MODEL), lambda i: (i, 0)),
        out_shape=jax.ShapeDtypeStruct((m, D_MODEL), F32),
        compiler_params=pltpu.CompilerParams(dimension_semantics=("parallel",)),
        name="final_norm",
    )(x2d, g)


def _ssd_kernel(xbc_ref, cw_ref, cb_ref, dtt_ref, alt_ref, p64_ref,
                y_ref,
                pad_s, xs_s, bm_s, cm_s, bmt_s, dtr_s, y_s, st_s, tri_s, u_s, e_s, scl_s, *, seq):
    t = SSD_CHUNK
    n_chunks = seq // t
    for c in range(n_chunks):
        dtr_s[c] = dtt_ref[:, c * t:(c + 1) * t]

    halo = SUBLANE
    for lane_tile in range(SSD_SLAB // LANE):
        lanes = slice(lane_tile * LANE, (lane_tile + 1) * LANE)
        pad_s[lane_tile, 0:halo, :] = jnp.zeros((halo, LANE), F32)
        pad_s[lane_tile, halo + seq:2 * halo + seq, :] = jnp.zeros((halo, LANE), F32)
        pad_s[lane_tile, halo:halo + seq, :] = xbc_ref[0, :, lanes].astype(F32)

    def conv_silu(r0, lane_tile):
        lanes = slice(lane_tile * LANE, (lane_tile + 1) * LANE)
        acc = cb_ref[:, lanes]
        for k in range(CONV_WIDTH):
            lo = halo - CONV_PAD + k + r0
            acc = acc + cw_ref[k:k + 1, lanes] * pad_s[lane_tile, lo:lo + t, :]
        return acc / (1.0 + jnp.exp(-acc))

    x_tiles = GROUP_WIDTH // LANE
    for c in range(n_chunks):
        r0 = c * t
        rows = slice(r0, r0 + t)
        for lane_tile in range(x_tiles):
            lanes = slice(lane_tile * LANE, (lane_tile + 1) * LANE)
            xv = conv_silu(r0, lane_tile)
            xs_s[rows, lanes] = xv.astype(BF16)
            y_s[rows, lanes] = xv * p64_ref[0, 0:1, lanes]
        bv = conv_silu(r0, x_tiles)
        bm_s[rows, :] = bv.astype(BF16)
        bmt_s[c] = bv.T.astype(BF16)
        cm_s[rows, :] = conv_silu(r0, x_tiles + 1).astype(BF16)

    ri = lax.broadcasted_iota(jnp.int32, (t, t), 0)
    ci = lax.broadcasted_iota(jnp.int32, (t, t), 1)
    lower, upper = ri >= ci, ri <= ci
    tri_s[:, 0:t] = upper.astype(BF16)
    tri_s[:, t:2 * t] = lower.astype(BF16)
    st_s[...] = jnp.zeros_like(st_s)

    nh = HEADS_PER_GROUP
    a_rows = -jnp.exp(alt_ref[...])
    head_of_lane = lax.broadcasted_iota(jnp.int32, (1, GROUP_WIDTH), 1) // SSD_HEAD_DIM
    low_half = lax.broadcasted_iota(jnp.int32, (1, LANE), 1) < SSD_HEAD_DIM

    def lanes_from_row(row):
        return jnp.broadcast_to(row, (t, t)).T

    def expand64(cols):
        return jnp.concatenate([jnp.where(low_half, cols[0], cols[1]),
                                jnp.where(low_half, cols[2], cols[3])], axis=1)

    def prepare(chunks):
        loaded = []
        for c in chunks:
            rows = pl.ds(pl.multiple_of(c * t, t), t)
            loaded.append((c, rows, dtr_s[c], xs_s[rows, :], cm_s[rows, :], bm_s[rows, :], bmt_s[c]))
        cums = [_exact_right(dt_all, tri_s[...]) for _, _, dt_all, _, _, _, _ in loaded]
        cbs = [_dot_nt(cm_c, bm_c) for _, _, _, _, cm_c, bm_c, _ in loaded]
        staged = []
        for (c, rows, dt_all, xs_c, _, _, bmt_c), cum, cb in zip(loaded, cums, cbs):
            cs = (cum[0:nh, 0:t] * a_rows[0:nh], cum[nh:2 * nh, t:2 * t] * a_rows[nh:2 * nh])
            dts = (dt_all[0:nh, :], dt_all[nh:2 * nh, :])
            tots = (cs[0][:, t - 1:t], cs[1][:, 0:1])
            w_rows = [dts[d] * jnp.exp(tots[d] - cs[d]) for d in range(2)]
            grow = [jnp.exp(tots[d]) for d in range(2)]
            bmt_f = bmt_c.astype(F32)
            l_parts, bw, seen, scale = [], ([], []), ([], []), [0.0, 0.0]
            for h in range(nh):
                mix = jnp.zeros((t, t), F32)
                for d, valid in ((0, lower), (1, upper)):
                    row = cs[d][h:h + 1, :]
                    col = lanes_from_row(row)
                    mix = mix + jnp.exp(jnp.where(valid, col - row, NEG_INF)) * dts[d][h:h + 1, :]
                    seen[d].append(jnp.exp(col))
                    bw[d].append((bmt_f * w_rows[d][h:h + 1, :]).astype(BF16))
                    scale[d] = scale[d] + jnp.where(head_of_lane == h, grow[d][h:h + 1, :], 0.0)
                l_parts.append((cb * mix).astype(BF16))
            lhs = jnp.concatenate([jnp.concatenate(l_parts, axis=1), jnp.concatenate(bw[0], axis=1),
                                   jnp.concatenate(bw[1], axis=1)], axis=0)
            xs_stack = jnp.concatenate(
                [jnp.where(head_of_lane == h, xs_c, jnp.zeros_like(xs_c)) for h in range(nh)], axis=0)
            staged.append((c, rows, lhs, xs_stack, seen, scale))
        outs = [_dot(lhs, xs_stack) for _, _, lhs, xs_stack, _, _ in staged]
        for (c, rows, _, _, seen, scale), out in zip(staged, outs):
            y_s[rows, :] += out[0:t, :]
            for d in range(2):
                u_s[d, c] = out[(d + 1) * t:(d + 2) * t, :]
                e_s[d, c] = expand64(seen[d])
                scl_s[d, c] = jnp.broadcast_to(scale[d], (SUBLANE, GROUP_WIDTH))

    batch = 8
    def prep_body(i, carry):
        prepare([i * batch + u for u in range(batch)])
        return carry

    lax.fori_loop(0, n_chunks // batch, prep_body, 0)

    def scan_body(i, carry):
        steps = []
        for d, c in ((0, i), (1, n_chunks - 1 - i)):
            rows = pl.ds(pl.multiple_of(c * t, t), t)
            steps.append((d, c, rows, st_s[d], cm_s[rows, :]))
        reads = [_dot(cm_c, st.astype(BF16)) for _, _, _, st, cm_c in steps]
        for (d, c, rows, st, _), read in zip(steps, reads):
            y_s[rows, :] += read * e_s[d, c]
            st_s[d] = st * scl_s[d, c][0:1, :] + u_s[d, c]
        return carry

    lax.fori_loop(0, n_chunks, scan_body, 0, unroll=8)
    y_ref[0] = y_s[...].astype(y_ref.dtype)


def _ssd(proj3, conv_w, conv_b, dtt, alog_col, p64):
    b, seq, _ = proj3.shape
    gw, ns = GROUP_WIDTH, SSD_STATE
    slab0 = OFF_SSD // SSD_SLAB
    kern = functools.partial(_ssd_kernel, seq=seq)
    return pl.pallas_call(
        kern,
        grid=(b, SSD_GROUPS),
        in_specs=[
            pl.BlockSpec((1, seq, SSD_SLAB), lambda i, g: (i, 0, slab0 + g)),
            pl.BlockSpec((CONV_WIDTH, SSD_SLAB), lambda i, g: (0, g)),
            pl.BlockSpec((1, SSD_SLAB), lambda i, g: (0, g)),
            pl.BlockSpec((2 * HEADS_PER_GROUP, seq), lambda i, g: (g, i)),
            pl.BlockSpec((2 * HEADS_PER_GROUP, 1), lambda i, g: (g, 0)),
            pl.BlockSpec((1, SUBLANE, gw), lambda i, g: (g, 0, 0)),
        ],
        out_specs=pl.BlockSpec((1, seq, gw), lambda i, g: (i, 0, g)),
        out_shape=jax.ShapeDtypeStruct((b, seq, SSD_INNER), BF16),
        scratch_shapes=[
            pltpu.VMEM((SSD_SLAB // LANE, seq + 2 * SUBLANE, LANE), F32),
            pltpu.VMEM((seq, gw), BF16),
            pltpu.VMEM((seq, ns), BF16),
            pltpu.VMEM((seq, ns), BF16),
            pltpu.VMEM((seq // SSD_CHUNK, ns, SSD_CHUNK), BF16),
            pltpu.VMEM((seq // SSD_CHUNK, 2 * HEADS_PER_GROUP, SSD_CHUNK), F32),
            pltpu.VMEM((seq, gw), F32),
            pltpu.VMEM((2, ns, gw), F32),
            pltpu.VMEM((SSD_CHUNK, 2 * SSD_CHUNK), BF16),
            pltpu.VMEM((2, seq // SSD_CHUNK, ns, gw), F32),
            pltpu.VMEM((2, seq // SSD_CHUNK, SSD_CHUNK, gw), F32),
            pltpu.VMEM((2, seq // SSD_CHUNK, SUBLANE, gw), F32),
        ],
        compiler_params=pltpu.CompilerParams(
            dimension_semantics=("parallel", "arbitrary"), vmem_limit_bytes=VMEM_LIMIT),
        name="ssd",
    )(proj3, conv_w, conv_b, dtt, alog_col, p64)


def _win_kernel(sink_ref, qkv_ref, o_ref, bias_s, vext_s, *, seq):
    q_cols = slice(0, WIN_REP * HEAD_DIM)
    k_cols = slice(WIN_REP * HEAD_DIM, (WIN_REP + 1) * HEAD_DIM)
    v_cols = slice((WIN_REP + 1) * HEAD_DIM, (WIN_REP + 2) * HEAD_DIM)
    blk = WIN_HALF
    kwin = 3 * blk
    n_blk = seq // blk
    kv = pl.program_id(1)
    scale2 = (HEAD_DIM ** -0.5) * LOG2E

    ri = lax.broadcasted_iota(jnp.int32, (blk, kwin), 0)
    ci = lax.broadcasted_iota(jnp.int32, (blk, kwin), 1)
    for w, shift in enumerate((0, -blk, -2 * blk)):
        bias_s[w] = jnp.where(jnp.abs(ci + shift - ri) <= WIN_HALF, 0.0, NEG_INF).astype(F32)
    vext_s[:, 0:HEAD_DIM] = qkv_ref[0, :, v_cols]
    vext_s[:, HEAD_DIM:2 * HEAD_DIM] = jnp.ones((seq, HEAD_DIM), BF16)
    sinks = [sink_ref[kv * WIN_REP + r] * LOG2E for r in range(WIN_REP)]

    def blocks(specs):
        loaded = []
        for qi, which in specs:
            q0 = qi * blk if isinstance(qi, int) else pl.multiple_of(qi * blk, blk)
            k0 = (0, q0 - blk, seq - kwin)[which]
            if not isinstance(k0, int):
                k0 = pl.multiple_of(k0, blk)
            loaded.append((q0, qkv_ref[0, pl.ds(q0, blk), q_cols], qkv_ref[0, pl.ds(k0, kwin), k_cols],
                           vext_s[pl.ds(k0, kwin), :], bias_s[which]))
        scores = [[_dot_nt(qt[:, r * HEAD_DIM:(r + 1) * HEAD_DIM], kt) for r in range(WIN_REP)]
                  for _, qt, kt, _, _ in loaded]
        maxes, probs = [], []
        for heads, (_, _, _, _, bias) in zip(scores, loaded):
            for r, s in enumerate(heads):
                s = s * scale2 + bias
                m = jnp.maximum(jnp.max(s, axis=-1, keepdims=True), sinks[r])
                maxes.append(m)
                probs.append(jnp.exp2(s - m).astype(BF16))
        accs = [_dot(probs[i * WIN_REP + r], vt)
                for i, (_, _, _, vt, _) in enumerate(loaded) for r in range(WIN_REP)]
        for i, (q0, _, _, _, _) in enumerate(loaded):
            for r in range(WIN_REP):
                acc, m = accs[i * WIN_REP + r], maxes[i * WIN_REP + r]
                den = acc[:, HEAD_DIM:] + jnp.exp2(sinks[r] - m)
                o_ref[0, pl.ds(q0, blk), r * HEAD_DIM:(r + 1) * HEAD_DIM] = (
                    acc[:, :HEAD_DIM] / den).astype(o_ref.dtype)

    blocks([(0, 0), (n_blk - 1, 2)])
    pairs = (n_blk - 2) // 2

    def body(pi, carry):
        blocks([(1 + 2 * pi, 1), (2 + 2 * pi, 1)])
        return carry

    lax.fori_loop(0, pairs, body, 0)
    if (n_blk - 2) % 2:
        blocks([(n_blk - 2, 1)])


def _win_attn(proj3, sink):
    b, seq, _ = proj3.shape
    assert seq >= 3 * WIN_HALF and seq % WIN_HALF == 0
    qw = WIN_REP * HEAD_DIM
    kern = functools.partial(_win_kernel, seq=seq)
    return pl.pallas_call(
        kern,
        grid=(b, WIN_KV_HEADS),
        in_specs=[
            pl.BlockSpec(memory_space=pltpu.SMEM),
            pl.BlockSpec((1, seq, WIN_SLAB), lambda i, g: (i, 0, OFF_WIN // WIN_SLAB + g)),
        ],
        out_specs=pl.BlockSpec((1, seq, qw), lambda i, g: (i, 0, g)),
        out_shape=jax.ShapeDtypeStruct((b, seq, WIN_Q_HEADS * HEAD_DIM), BF16),
        scratch_shapes=[pltpu.VMEM((3, WIN_HALF, 3 * WIN_HALF), F32),
                        pltpu.VMEM((seq, 2 * HEAD_DIM), BF16)],
        compiler_params=pltpu.CompilerParams(
            dimension_semantics=("parallel", "parallel"), vmem_limit_bytes=VMEM_LIMIT),
        name="win_attn",
    )(sink, proj3)


def _dil_kernel(qkv_ref, o_ref, stage_s, qd_s, kd_s, vext_s, od_s, ld_s, og_s, lg_s, bias_s, bias1_s,
                *, seq):
    def cols(part, gi):
        c0 = (part * DIL_GROUPS + gi) * HEAD_DIM
        return slice(c0, c0 + HEAD_DIM)

    tq = 2 * DIL_HALF
    kw = tq + 2 * DIL_HALF
    scale2 = (HEAD_DIM ** -0.5) * LOG2E

    ri = lax.broadcasted_iota(jnp.int32, (tq, kw), 0)
    ci = lax.broadcasted_iota(jnp.int32, (tq, kw), 1)
    for w, shift in enumerate((0, -DIL_HALF, -2 * DIL_HALF)):
        bias_s[w] = jnp.where(jnp.abs(ci + shift - ri) <= DIL_HALF, 0.0, NEG_INF).astype(F32)
    ri1 = lax.broadcasted_iota(jnp.int32, (tq, tq), 0)
    ci1 = lax.broadcasted_iota(jnp.int32, (tq, tq), 1)
    bias1_s[...] = jnp.where(jnp.abs(ci1 - ri1) <= DIL_HALF, 0.0, NEG_INF).astype(F32)
    vext_s[:, HEAD_DIM:2 * HEAD_DIM] = jnp.ones((seq, HEAD_DIM), BF16)

    def deinterleave(src_cols, dst_ref, dil):
        n_sub = seq // dil
        stage_s[...] = qkv_ref[0, :, src_cols].astype(F32)
        for r in range(dil):
            dst_ref[r * n_sub:(r + 1) * n_sub, 0:HEAD_DIM] = (
                stage_s[pl.ds(r, n_sub, stride=dil), :].astype(BF16))

    def aligned(x, mult):
        return x if isinstance(x, int) else pl.multiple_of(x, mult)

    def run_group(q_at, k_at, n_sub):
        def tiles(specs):
            loaded = []
            for q0, k0, width, bias in specs:
                q0, k0 = aligned(q0, tq), aligned(k0, DIL_HALF)
                loaded.append((q0, q_at(pl.ds(q0, tq)), k_at(pl.ds(k0, width)),
                               vext_s[pl.ds(k0, width), :], bias))
            scores = [_dot_nt(qt, kt) for _, qt, kt, _, _ in loaded]
            maxes, probs = [], []
            for s, (_, _, _, _, bias) in zip(scores, loaded):
                s = s * scale2 + bias
                m = jnp.max(s, axis=-1, keepdims=True)
                maxes.append(m)
                probs.append(jnp.exp2(s - m).astype(BF16))
            accs = [_dot(p, vt) for p, (_, _, _, vt, _) in zip(probs, loaded)]
            for acc, m, (q0, _, _, _, _) in zip(accs, maxes, loaded):
                den = acc[:, HEAD_DIM:]
                od_s[pl.ds(q0, tq), :] = acc[:, :HEAD_DIM] / den
                ld_s[pl.ds(q0, tq), :] = m + jnp.log2(den)

        def first(seg0):
            return (seg0, seg0, kw, bias_s[0])

        def inner(q0):
            return (q0, q0 - DIL_HALF, kw, bias_s[1])

        def last(seg0):
            return (seg0 + n_sub - tq, seg0 + n_sub - kw, kw, bias_s[2])

        tiles_per_seg, n_seg, nb = n_sub // tq, seq // n_sub, TILE_BATCH
        if tiles_per_seg == 1:
            wide = SEGMENT_TILE_BATCH
            def body1(bi, carry):
                tiles([((bi * wide + u) * tq, (bi * wide + u) * tq, tq, bias1_s[...]) for u in range(wide)])
                return carry
            lax.fori_loop(0, seq // (tq * wide), body1, 0)
        elif SEGMENT_TILE_BATCH % tiles_per_seg == 0:
            segs = SEGMENT_TILE_BATCH // tiles_per_seg
            def seg_body(si, carry):
                specs = []
                for u in range(segs):
                    seg0 = (si * segs + u) * n_sub
                    specs += ([first(seg0)] + [inner(seg0 + ti * tq) for ti in range(1, tiles_per_seg - 1)]
                              + [last(seg0)])
                tiles(specs)
                return carry
            lax.fori_loop(0, n_seg // segs, seg_body, 0)
        else:
            assert n_seg == 1
            loops = (tiles_per_seg - 2) // nb
            tiles([first(0)])
            def inner_body(bi, carry):
                tiles([inner((1 + bi * nb + u) * tq) for u in range(nb)])
                return carry
            lax.fori_loop(0, loops, inner_body, 0)
            tiles([inner(ti * tq) for ti in range(1 + loops * nb, tiles_per_seg - 1)] + [last(0)])

    for gi, (_, dil) in enumerate(DIL_PATTERNS):
        n_sub = seq // dil
        if dil == 1:
            vext_s[:, 0:HEAD_DIM] = qkv_ref[0, :, cols(2, gi)]
            run_group(lambda sl: qkv_ref[0, sl, cols(0, gi)], lambda sl: qkv_ref[0, sl, cols(1, gi)], n_sub)
            og_s[gi] = od_s[...]
            lg_s[gi] = ld_s[...]
        else:
            deinterleave(cols(0, gi), qd_s, dil)
            deinterleave(cols(1, gi), kd_s, dil)
            deinterleave(cols(2, gi), vext_s, dil)
            run_group(lambda sl: qd_s[sl, :], lambda sl: kd_s[sl, :], n_sub)
            for r in range(dil):
                og_s[gi, pl.ds(r, n_sub, stride=dil), :] = od_s[r * n_sub:(r + 1) * n_sub, :]
                lg_s[gi, pl.ds(r, n_sub, stride=dil), :] = ld_s[r * n_sub:(r + 1) * n_sub, :]

    m = jnp.maximum(jnp.maximum(lg_s[0], lg_s[1]), lg_s[2])
    num = jnp.zeros((seq, HEAD_DIM), F32)
    den = jnp.zeros((seq, HEAD_DIM), F32)
    for gi in range(DIL_GROUPS):
        w = jnp.exp2(lg_s[gi] - m)
        num = num + w * og_s[gi]
        den = den + w
    o_ref[0] = (num / den).astype(o_ref.dtype)


def _dil_attn(proj3):
    b, seq, _ = proj3.shape
    tq = 2 * DIL_HALF
    for _, dil in DIL_PATTERNS:
        assert (seq // dil) % tq == 0
    kern = functools.partial(_dil_kernel, seq=seq)
    return pl.pallas_call(
        kern,
        grid=(b, DIL_HEADS),
        in_specs=[pl.BlockSpec((1, seq, DIL_SLAB), lambda i, h: (i, 0, OFF_DIL // DIL_SLAB + h))],
        out_specs=pl.BlockSpec((1, seq, HEAD_DIM), lambda i, h: (i, 0, h)),
        out_shape=jax.ShapeDtypeStruct((b, seq, DIL_WIDTH), BF16),
        scratch_shapes=[
            pltpu.VMEM((seq, HEAD_DIM), F32),
            pltpu.VMEM((seq, HEAD_DIM), BF16),
            pltpu.VMEM((seq, HEAD_DIM), BF16),
            pltpu.VMEM((seq, 2 * HEAD_DIM), BF16),
            pltpu.VMEM((seq, HEAD_DIM), F32),
            pltpu.VMEM((seq, HEAD_DIM), F32),
            pltpu.VMEM((DIL_GROUPS, seq, HEAD_DIM), F32),
            pltpu.VMEM((DIL_GROUPS, seq, HEAD_DIM), F32),
            pltpu.VMEM((3, tq, tq + 2 * DIL_HALF), F32),
            pltpu.VMEM((tq, tq), F32),
        ],
        compiler_params=pltpu.CompilerParams(
            dimension_semantics=("parallel", "parallel"), vmem_limit_bytes=VMEM_LIMIT),
        name="dil_attn",
    )(proj3)


def _rope_tables(seq):
    inv = ROPE_THETA ** (-jnp.arange(0, ROPE_DIM, 2, dtype=F32) / ROPE_DIM)
    ang = jnp.arange(seq, dtype=F32)[:, None] * inv[None, :]
    cos, sin = jnp.cos(ang), jnp.sin(ang)
    zeros = jnp.zeros((seq, HEAD_DIM - ROPE_DIM), F32)
    zh = jnp.zeros_like(sin)
    ct = jnp.concatenate([cos, cos, jnp.ones_like(zeros)], axis=1)
    sa = jnp.concatenate([zh, sin, zeros], axis=1)
    sb = jnp.concatenate([-sin, zh, zeros], axis=1)
    return ct, sa, sb


def _group_major(v):
    return v.reshape(2, SSD_GROUPS, HEADS_PER_GROUP).transpose(1, 0, 2).reshape(DT_COLS)


def _ssd_slabs(a):
    rows = a.shape[0]
    x = a[:, :SSD_INNER].reshape(rows, SSD_GROUPS, GROUP_WIDTH)
    bm = a[:, SSD_INNER:SSD_INNER + SSD_GROUPS * SSD_STATE].reshape(rows, SSD_GROUPS, SSD_STATE)
    cm = a[:, SSD_INNER + SSD_GROUPS * SSD_STATE:].reshape(rows, SSD_GROUPS, SSD_STATE)
    return jnp.concatenate([x, bm, cm], axis=2).reshape(rows, XBC_WIDTH)


def _main_weight(w_in):
    d = w_in.shape[0]
    o = DT_IN_OFF + DT_COLS
    n_dil = 3 * DIL_GROUPS * DIL_WIDTH
    n_q, n_kv = WIN_Q_HEADS * HEAD_DIM, WIN_KV_HEADS * HEAD_DIM
    z = w_in[:, :SSD_INNER]
    ssd = _ssd_slabs(w_in[:, SSD_INNER:DT_IN_OFF])
    dil = w_in[:, o:o + n_dil].reshape(d, 3 * DIL_GROUPS, DIL_HEADS, HEAD_DIM)
    dil = dil.transpose(0, 2, 1, 3).reshape(d, n_dil)
    q = w_in[:, o + n_dil:o + n_dil + n_q].reshape(d, WIN_KV_HEADS, WIN_REP * HEAD_DIM)
    k = w_in[:, o + n_dil + n_q:o + n_dil + n_q + n_kv].reshape(d, WIN_KV_HEADS, HEAD_DIM)
    v = w_in[:, o + n_dil + n_q + n_kv:o + n_dil + n_q + 2 * n_kv].reshape(d, WIN_KV_HEADS, HEAD_DIM)
    win = jnp.concatenate([q, k, v], axis=2).reshape(d, WIN_KV_HEADS * WIN_SLAB)
    gates = w_in[:, o + n_dil + n_q + 2 * n_kv:]
    return jnp.concatenate([z, ssd, win, dil, gates], axis=1).astype(BF16)


def _layer_params(w_in, dt_bias, a_log, d_skip):
    w_main = _main_weight(w_in)
    w_dt = w_in[:, DT_IN_OFF:DT_IN_OFF + DT_COLS]
    w_dt = w_dt.reshape(D_MODEL, 2, SSD_GROUPS, HEADS_PER_GROUP).transpose(0, 2, 1, 3).reshape(D_MODEL, DT_COLS)
    w_dtt = jnp.pad(w_dt, ((0, 0), (0, DT_LANES - DT_COLS))).astype(BF16).T
    b_dt = jnp.pad(_group_major(dt_bias.astype(F32)), (0, DT_LANES - DT_COLS))
    alog = jnp.pad(_group_major(a_log.astype(F32)), (0, DT_LANES - DT_COLS))
    p64 = jnp.repeat(d_skip.astype(F32).reshape(SSD_GROUPS, 1, HEADS_PER_GROUP), SSD_HEAD_DIM, axis=2)
    p64 = jnp.pad(p64, ((0, 0), (0, SUBLANE - 1), (0, 0)))
    return w_main, w_dtt, b_dt.reshape(DT_LANES, 1), alog.reshape(DT_LANES, 1), p64


def kernel(x, g_mix, w_in, conv_w, conv_b, dt_bias, a_log, d_skip, ssd_norm, w_a, w_b, w_c, sink,
           w_out, g_mlp, w_up, w_down, g_final):
    b, seq, _ = x.shape
    m = b * seq
    tm = min(1024, seq)
    ct, sa, sb = _rope_tables(seq)
    xf = x.reshape(m, D_MODEL).astype(F32)
    for i in range(DEPTH):
        w_main, w_dtt, b_dtt, alog_col, p64 = _layer_params(w_in[i], dt_bias[i], a_log[i], d_skip[i])
        proj, dtt = _proj(xf, g_mix[i].reshape(1, D_MODEL), w_main, w_dtt, b_dtt, ct, sa, sb, seq,
                          tm=tm, tn=1024)
        proj3 = proj.reshape(b, seq, N_MAIN)
        y_ssd = _ssd(proj3, _ssd_slabs(conv_w[i].astype(F32)),
                     _ssd_slabs(conv_b[i].astype(F32).reshape(1, XBC_WIDTH)), dtt, alog_col, p64)
        y_b = _dil_attn(proj3)
        y_c = _win_attn(proj3, sink[i].astype(F32))
        merged = _merge(y_ssd.reshape(m, SSD_INNER), proj, ssd_norm[i].reshape(1, SSD_INNER),
                        y_b.reshape(m, DIL_WIDTH), y_c.reshape(m, WIN_Q_HEADS * HEAD_DIM),
                        w_a[i].astype(BF16), w_b[i].astype(BF16), w_c[i].astype(BF16), tm=tm, tn=256)
        xf = _matmul_res(merged, w_out[i].astype(BF16), xf, tm=tm, tn=1024, tk=D_MODEL)
        u = _mlp_up(xf, g_mlp[i].reshape(1, D_MODEL), w_up[i].astype(BF16), tm=tm, tn=1024)
        xf = _matmul_res(u, w_down[i].astype(BF16), xf, tm=tm, tn=1024, tk=2048)
    out = _final_norm(xf, g_final.reshape(1, D_MODEL), tm=min(512, seq))
    return out.reshape(b, seq, D_MODEL).astype(x.dtype)
```

```python
import functools

import jax
import jax.numpy as jnp
from jax import lax
from jax.experimental import pallas as pl
from jax.experimental.pallas import tpu as pltpu

F32 = jnp.float32
BF16 = jnp.bfloat16

D_MODEL = 2048
DEPTH = 2
SSD_HEADS = 32
SSD_HEAD_DIM = 64
SSD_INNER = SSD_HEADS * SSD_HEAD_DIM
SSD_GROUPS = 8
SSD_STATE = 128
SSD_CHUNK = 128
HEADS_PER_GROUP = SSD_HEADS // SSD_GROUPS
GROUP_WIDTH = HEADS_PER_GROUP * SSD_HEAD_DIM
CONV_WIDTH = 5
CONV_PAD = (CONV_WIDTH - 1) // 2
XBC_WIDTH = SSD_INNER + 2 * SSD_GROUPS * SSD_STATE
HEAD_DIM = 128
ROPE_DIM = HEAD_DIM // 4
ROPE_HALF = ROPE_DIM // 2
ROPE_PARTNER = HEAD_DIM // 2
ROPE_THETA = 500000.0
DIL_PATTERNS = ((128, 1), (512, 4), (2048, 16))
DIL_GROUPS = len(DIL_PATTERNS)
DIL_HEADS = 8
DIL_WIDTH = DIL_HEADS * HEAD_DIM
DIL_HALF = 64
TILE_BATCH = 7
SEGMENT_TILE_BATCH = 8
WIN_Q_HEADS = 16
WIN_KV_HEADS = 4
WIN_REP = WIN_Q_HEADS // WIN_KV_HEADS
WIN_HALF = 128
D_FF = 4 * D_MODEL
N_BRANCH = 3
EPS = 1e-6
NEG_INF = -1e30
LOG2E = 1.4426950408889634

SSD_SLAB = GROUP_WIDTH + 2 * SSD_STATE
WIN_SLAB = (WIN_REP + 2) * HEAD_DIM
DIL_SLAB = 3 * DIL_GROUPS * HEAD_DIM
OFF_Z = 0
OFF_SSD = OFF_Z + SSD_INNER
OFF_WIN = OFF_SSD + SSD_GROUPS * SSD_SLAB
OFF_DIL = OFF_WIN + WIN_KV_HEADS * WIN_SLAB
OFF_GATE = OFF_DIL + DIL_HEADS * DIL_SLAB
N_MAIN = OFF_GATE + N_BRANCH * D_MODEL
assert OFF_SSD % SSD_SLAB == 0 and OFF_WIN % WIN_SLAB == 0 and OFF_DIL % DIL_SLAB == 0
DT_COLS = 2 * SSD_HEADS
DT_IN_OFF = SSD_INNER + XBC_WIDTH
DT_LANES = 128

LANE = 128
SUBLANE = 8
VMEM_LIMIT = 56 * 1024 * 1024


def _split3(x):
    hi = x.astype(BF16)
    r1 = x - hi.astype(F32)
    mid = r1.astype(BF16)
    r2 = r1 - mid.astype(F32)
    return hi, mid, r2.astype(BF16)


def _dot(a, b):
    return jnp.dot(a, b, preferred_element_type=F32)


def _dot_nt(a, b):
    return lax.dot_general(a, b, (((1,), (1,)), ((), ())), preferred_element_type=F32)


def _exact_right(x, mat):
    hi, mid, lo = _split3(x)
    return _dot(hi, mat) + _dot(mid, mat) + _dot(lo, mat)


def _softplus(x):
    return jnp.maximum(x, 0.0) + jnp.log1p(jnp.exp(-jnp.abs(x)))


def _norm_rows(x, g):
    ms = jnp.mean(x * x, axis=-1, keepdims=True)
    return x * lax.rsqrt(ms + EPS) * g


def _rope_tile(a, ct, st):
    return a * ct + pltpu.roll(a, ROPE_PARTNER, 1) * st


def _proj_kernel(x_ref, g_ref, w_ref, wdtt_ref, bdtt_ref, ct_ref, st_ref,
                 o_ref, dtt_ref, h_ref, *, tn, tile_kinds):
    j = pl.program_id(1)

    @pl.when(j == 0)
    def _():
        hb = _norm_rows(x_ref[...], g_ref[...]).astype(BF16)
        h_ref[...] = hb
        dtt_ref[...] = _softplus(_dot_nt(wdtt_ref[...], hb) + bdtt_ref[...])

    def tile_body(kind):
        acc = _dot(h_ref[...], w_ref[...])
        if 'r' in kind:
            ct, st = ct_ref[...], st_ref[...]
        for c, k in enumerate(kind):
            sl = slice(c * HEAD_DIM, (c + 1) * HEAD_DIM)
            a = acc[:, sl]
            if k == 'r':
                a = _rope_tile(a, ct, st)
            elif k == 'g':
                a = 0.5 * jnp.tanh(0.5 * a) + 0.5
            o_ref[:, sl] = a.astype(o_ref.dtype)

    for kind in sorted(set(tile_kinds)):
        idx = [t for t, k in enumerate(tile_kinds) if k == kind]
        runs, start = [], idx[0]
        for a, b in zip(idx, idx[1:] + [None]):
            if b != a + 1:
                runs.append((start, a))
                start = b
        cond = None
        for lo, hi in runs:
            c = jnp.logical_and(j >= lo, j <= hi)
            cond = c if cond is None else jnp.logical_or(cond, c)
        pl.when(cond)(functools.partial(tile_body, kind))


def _proj_tile_kinds(tn):
    def head_kind(col):
        if col >= OFF_GATE:
            return 'g'
        if col >= OFF_DIL:
            return 'r'
        if col >= OFF_WIN:
            return 'r' if (col - OFF_WIN) % WIN_SLAB < (WIN_REP + 1) * HEAD_DIM else 'p'
        return 'p'
    return tuple(''.join(head_kind(t * tn + c * HEAD_DIM) for c in range(tn // HEAD_DIM))
                 for t in range(N_MAIN // tn))


def _proj(x2d, g, w_main, w_dtt, b_dtt, ct, st, seq, *, tm, tn):
    m = x2d.shape[0]
    assert seq % tm == 0 and m % tm == 0 and N_MAIN % tn == 0 and tn % HEAD_DIM == 0
    seq_tiles = seq // tm
    kern = functools.partial(_proj_kernel, tn=tn, tile_kinds=_proj_tile_kinds(tn))
    return pl.pallas_call(
        kern,
        grid=(m // tm, N_MAIN // tn),
        in_specs=[
            pl.BlockSpec((tm, D_MODEL), lambda i, j: (i, 0)),
            pl.BlockSpec((1, D_MODEL), lambda i, j: (0, 0)),
            pl.BlockSpec((D_MODEL, tn), lambda i, j: (0, j)),
            pl.BlockSpec((DT_LANES, D_MODEL), lambda i, j: (0, 0)),
            pl.BlockSpec((DT_LANES, 1), lambda i, j: (0, 0)),
            pl.BlockSpec((tm, HEAD_DIM), lambda i, j: (i % seq_tiles, 0)),
            pl.BlockSpec((tm, HEAD_DIM), lambda i, j: (i % seq_tiles, 0)),
        ],
        out_specs=[
            pl.BlockSpec((tm, tn), lambda i, j: (i, j)),
            pl.BlockSpec((DT_LANES, tm), lambda i, j: (0, i)),
        ],
        out_shape=[
            jax.ShapeDtypeStruct((m, N_MAIN), BF16),
            jax.ShapeDtypeStruct((DT_LANES, m), F32),
        ],
        scratch_shapes=[pltpu.VMEM((tm, D_MODEL), BF16)],
        compiler_params=pltpu.CompilerParams(
            dimension_semantics=("parallel", "arbitrary"), vmem_limit_bytes=VMEM_LIMIT),
        name="proj",
    )(x2d, g, w_main, w_dtt, b_dtt, ct, st)


def _up_kernel(x_ref, g_ref, w_ref, o_ref, h_ref):
    @pl.when(pl.program_id(1) == 0)
    def _():
        h_ref[...] = _norm_rows(x_ref[...], g_ref[...]).astype(BF16)

    acc = jnp.maximum(_dot(h_ref[...], w_ref[...]), 0.0)
    o_ref[...] = (acc * acc).astype(o_ref.dtype)


def _mlp_up(x2d, g, w, *, tm, tn):
    m, n = x2d.shape[0], w.shape[1]
    return pl.pallas_call(
        _up_kernel,
        grid=(m // tm, n // tn),
        in_specs=[
            pl.BlockSpec((tm, D_MODEL), lambda i, j: (i, 0)),
            pl.BlockSpec((1, D_MODEL), lambda i, j: (0, 0)),
            pl.BlockSpec((D_MODEL, tn), lambda i, j: (0, j)),
        ],
        out_specs=pl.BlockSpec((tm, tn), lambda i, j: (i, j)),
        out_shape=jax.ShapeDtypeStruct((m, n), BF16),
        scratch_shapes=[pltpu.VMEM((tm, D_MODEL), BF16)],
        compiler_params=pltpu.CompilerParams(
            dimension_semantics=("parallel", "arbitrary"), vmem_limit_bytes=VMEM_LIMIT),
        name="mlp_up",
    )(x2d, g, w)


def _matmul_res_kernel(a_ref, w_ref, r_ref, o_ref, acc_ref):
    k = pl.program_id(2)

    @pl.when(k == 0)
    def _():
        acc_ref[...] = jnp.zeros_like(acc_ref)

    acc_ref[...] += _dot(a_ref[...], w_ref[...])

    @pl.when(k == pl.num_programs(2) - 1)
    def _():
        o_ref[...] = r_ref[...] + acc_ref[...]


def _matmul_res(a, w, res, *, tm, tn, tk):
    m, kdim = a.shape
    n = w.shape[1]
    return pl.pallas_call(
        _matmul_res_kernel,
        grid=(m // tm, n // tn, kdim // tk),
        in_specs=[
            pl.BlockSpec((tm, tk), lambda i, j, k: (i, k)),
            pl.BlockSpec((tk, tn), lambda i, j, k: (k, j)),
            pl.BlockSpec((tm, tn), lambda i, j, k: (i, j)),
        ],
        out_specs=pl.BlockSpec((tm, tn), lambda i, j, k: (i, j)),
        out_shape=jax.ShapeDtypeStruct((m, n), F32),
        scratch_shapes=[pltpu.VMEM((tm, tn), F32)],
        compiler_params=pltpu.CompilerParams(
            dimension_semantics=("parallel", "parallel", "arbitrary"), vmem_limit_bytes=VMEM_LIMIT),
        name="matmul_res",
    )(a, w, res)


def _gated_norm(ys, z, nw):
    z = z.astype(F32)
    y = ys.astype(F32) * (z / (1.0 + jnp.exp(-z)))
    return _norm_rows(y, nw).astype(BF16)


def _merge_kernel(ys_ref, z_ref, nw_ref, yb_ref, yc_ref, g0_ref, g1_ref, g2_ref,
                  wa_ref, wb_ref, wc_ref, o_ref, ya_ref):
    @pl.when(pl.program_id(1) == 0)
    def _():
        ya_ref[...] = _gated_norm(ys_ref[...], z_ref[...], nw_ref[...])

    acc = g0_ref[...].astype(F32) * _dot(ya_ref[...], wa_ref[...])
    acc += g1_ref[...].astype(F32) * _dot(yb_ref[...], wb_ref[...])
    acc += g2_ref[...].astype(F32) * _dot(yc_ref[...], wc_ref[...])
    o_ref[...] = acc.astype(o_ref.dtype)


def _merge(y_ssd, proj, norm_w, y_b, y_c, w_a, w_b, w_c, *, tm, tn):
    m = y_ssd.shape[0]
    gate_blk = OFF_GATE // tn
    per_gate = D_MODEL // tn
    return pl.pallas_call(
        _merge_kernel,
        grid=(m // tm, D_MODEL // tn),
        in_specs=[
            pl.BlockSpec((tm, SSD_INNER), lambda i, j: (i, 0)),
            pl.BlockSpec((tm, SSD_INNER), lambda i, j: (i, OFF_Z // SSD_INNER)),
            pl.BlockSpec((1, SSD_INNER), lambda i, j: (0, 0)),
            pl.BlockSpec((tm, DIL_WIDTH), lambda i, j: (i, 0)),
            pl.BlockSpec((tm, WIN_Q_HEADS * HEAD_DIM), lambda i, j: (i, 0)),
            pl.BlockSpec((tm, tn), lambda i, j: (i, gate_blk + j)),
            pl.BlockSpec((tm, tn), lambda i, j: (i, gate_blk + per_gate + j)),
            pl.BlockSpec((tm, tn), lambda i, j: (i, gate_blk + 2 * per_gate + j)),
            pl.BlockSpec((SSD_INNER, tn), lambda i, j: (0, j)),
            pl.BlockSpec((DIL_WIDTH, tn), lambda i, j: (0, j)),
            pl.BlockSpec((WIN_Q_HEADS * HEAD_DIM, tn), lambda i, j: (0, j)),
        ],
        out_specs=pl.BlockSpec((tm, tn), lambda i, j: (i, j)),
        out_shape=jax.ShapeDtypeStruct((m, D_MODEL), BF16),
        scratch_shapes=[pltpu.VMEM((tm, SSD_INNER), BF16)],
        compiler_params=pltpu.CompilerParams(
            dimension_semantics=("parallel", "arbitrary"), vmem_limit_bytes=VMEM_LIMIT),
        name="merge",
    )(y_ssd, proj, norm_w, y_b, y_c, proj, proj, proj, w_a, w_b, w_c)


def _final_norm_kernel(x_ref, g_ref, o_ref):
    o_ref[...] = _norm_rows(x_ref[...], g_ref[...])


def _final_norm(x2d, g, *, tm):
    m = x2d.shape[0]
    return pl.pallas_call(
        _final_norm_kernel,
        grid=(m // tm,),
        in_specs=[pl.BlockSpec((tm, D_MODEL), lambda i: (i, 0)),
                  pl.BlockSpec((1, D_MODEL), lambda i: (0, 0))],
        out_specs=pl.BlockSpec((tm, D_MODEL), lambda i: (i, 0)),
        out_shape=jax.ShapeDtypeStruct((m, D_MODEL), F32),
        compiler_params=pltpu.CompilerParams(dimension_semantics=("parallel",)),
        name="final_norm",
    )(x2d, g)


def _ssd_kernel(xbc_ref, cw_ref, cb_ref, dtt_ref, alt_ref, p64_ref,
                y_ref,
                pad_s, xs_s, bm_s, cm_s, bmt_s, dtr_s, y_s, st_s, tri_s, u_s, e_s, scl_s, *, seq):
    t = SSD_CHUNK
    n_chunks = seq // t
    for c in range(n_chunks):
        dtr_s[c] = dtt_ref[:, c * t:(c + 1) * t]

    halo = SUBLANE
    for lane_tile in range(SSD_SLAB // LANE):
        lanes = slice(lane_tile * LANE, (lane_tile + 1) * LANE)
        pad_s[lane_tile, 0:halo, :] = jnp.zeros((halo, LANE), F32)
        pad_s[lane_tile, halo + seq:2 * halo + seq, :] = jnp.zeros((halo, LANE), F32)
        pad_s[lane_tile, halo:halo + seq, :] = xbc_ref[0, :, lanes].astype(F32)

    def conv_silu(r0, lane_tile):
        lanes = slice(lane_tile * LANE, (lane_tile + 1) * LANE)
        acc = cb_ref[:, lanes]
        for k in range(CONV_WIDTH):
            lo = halo - CONV_PAD + k + r0
            acc = acc + cw_ref[k:k + 1, lanes] * pad_s[lane_tile, lo:lo + t, :]
        return acc / (1.0 + jnp.exp(-acc))

    x_tiles = GROUP_WIDTH // LANE
    for c in range(n_chunks):
        r0 = c * t
        rows = slice(r0, r0 + t)
        for lane_tile in range(x_tiles):
            lanes = slice(lane_tile * LANE, (lane_tile + 1) * LANE)
            xv = conv_silu(r0, lane_tile)
            xs_s[rows, lanes] = xv.astype(BF16)
            y_s[rows, lanes] = xv * p64_ref[0, 0:1, lanes]
        bv = conv_silu(r0, x_tiles)
        bm_s[rows, :] = bv.astype(BF16)
        bmt_s[c] = bv.T.astype(BF16)
        cm_s[rows, :] = conv_silu(r0, x_tiles + 1).astype(BF16)

    ri = lax.broadcasted_iota(jnp.int32, (t, t), 0)
    ci = lax.broadcasted_iota(jnp.int32, (t, t), 1)
    lower, upper = ri >= ci, ri <= ci
    tri_s[:, 0:t] = upper.astype(BF16)
    tri_s[:, t:2 * t] = lower.astype(BF16)
    st_s[...] = jnp.zeros_like(st_s)

    nh = HEADS_PER_GROUP
    a_rows = -jnp.exp(alt_ref[...])
    head_of_lane = lax.broadcasted_iota(jnp.int32, (1, GROUP_WIDTH), 1) // SSD_HEAD_DIM
    low_half = lax.broadcasted_iota(jnp.int32, (1, LANE), 1) < SSD_HEAD_DIM

    def lanes_from_row(row):
        return jnp.broadcast_to(row, (t, t)).T

    def expand64(cols):
        return jnp.concatenate([jnp.where(low_half, cols[0], cols[1]),
                                jnp.where(low_half, cols[2], cols[3])], axis=1)

    def prepare(chunks):
        loaded = []
        for c in chunks:
            rows = pl.ds(pl.multiple_of(c * t, t), t)
            loaded.append((c, rows, dtr_s[c], xs_s[rows, :], cm_s[rows, :], bm_s[rows, :], bmt_s[c]))
        cums = [_exact_right(dt_all, tri_s[...]) for _, _, dt_all, _, _, _, _ in loaded]
        cbs = [_dot_nt(cm_c, bm_c) for _, _, _, _, cm_c, bm_c, _ in loaded]
        staged = []
        for (c, rows, dt_all, xs_c, _, _, bmt_c), cum, cb in zip(loaded, cums, cbs):
            cs = (cum[0:nh, 0:t] * a_rows[0:nh], cum[nh:2 * nh, t:2 * t] * a_rows[nh:2 * nh])
            dts = (dt_all[0:nh, :], dt_all[nh:2 * nh, :])
            tots = (cs[0][:, t - 1:t], cs[1][:, 0:1])
            w_rows = [dts[d] * jnp.exp(tots[d] - cs[d]) for d in range(2)]
            grow = [jnp.exp(tots[d]) for d in range(2)]
            bmt_f = bmt_c.astype(F32)
            l_parts, bw, seen, scale = [], ([], []), ([], []), [0.0, 0.0]
            for h in range(nh):
                mix = jnp.zeros((t, t), F32)
                for d, valid in ((0, lower), (1, upper)):
                    row = cs[d][h:h + 1, :]
                    col = lanes_from_row(row)
                    mix = mix + jnp.exp(jnp.where(valid, col - row, NEG_INF)) * dts[d][h:h + 1, :]
                    seen[d].append(jnp.exp(col))
                    bw[d].append((bmt_f * w_rows[d][h:h + 1, :]).astype(BF16))
                    scale[d] = scale[d] + jnp.where(head_of_lane == h, grow[d][h:h + 1, :], 0.0)
                l_parts.append((cb * mix).astype(BF16))
            lhs = jnp.concatenate([jnp.concatenate(l_parts, axis=1), jnp.concatenate(bw[0], axis=1),
                                   jnp.concatenate(bw[1], axis=1)], axis=0)
            xs_stack = jnp.concatenate(
                [jnp.where(head_of_lane == h, xs_c, jnp.zeros_like(xs_c)) for h in range(nh)], axis=0)
            staged.append((c, rows, lhs, xs_stack, seen, scale))
        outs = [_dot(lhs, xs_stack) for _, _, lhs, xs_stack, _, _ in staged]
        for (c, rows, _, _, seen, scale), out in zip(staged, outs):
            y_s[rows, :] += out[0:t, :]
            for d in range(2):
                u_s[d, c] = out[(d + 1) * t:(d + 2) * t, :]
                e_s[d, c] = expand64(seen[d])
                scl_s[d, c] = jnp.broadcast_to(scale[d], (SUBLANE, GROUP_WIDTH))

    batch = 8
    def prep_body(i, carry):
        prepare([i * batch + u for u in range(batch)])
        return carry

    lax.fori_loop(0, n_chunks // batch, prep_body, 0)

    def scan_body(i, carry):
        steps = []
        for d, c in ((0, i), (1, n_chunks - 1 - i)):
            rows = pl.ds(pl.multiple_of(c * t, t), t)
            steps.append((d, c, rows, st_s[d], cm_s[rows, :]))
        reads = [_dot(cm_c, st.astype(BF16)) for _, _, _, st, cm_c in steps]
        for (d, c, rows, st, _), read in zip(steps, reads):
            y_s[rows, :] += read * e_s[d, c]
            st_s[d] = st * scl_s[d, c][0:1, :] + u_s[d, c]
        return carry

    lax.fori_loop(0, n_chunks, scan_body, 0, unroll=8)
    y_ref[0] = y_s[...].astype(y_ref.dtype)


def _ssd(proj3, conv_w, conv_b, dtt, alog_col, p64):
    b, seq, _ = proj3.shape
    gw, ns = GROUP_WIDTH, SSD_STATE
    slab0 = OFF_SSD // SSD_SLAB
    kern = functools.partial(_ssd_kernel, seq=seq)
    return pl.pallas_call(
        kern,
        grid=(b, SSD_GROUPS),
        in_specs=[
            pl.BlockSpec((1, seq, SSD_SLAB), lambda i, g: (i, 0, slab0 + g)),
            pl.BlockSpec((CONV_WIDTH, SSD_SLAB), lambda i, g: (0, g)),
            pl.BlockSpec((1, SSD_SLAB), lambda i, g: (0, g)),
            pl.BlockSpec((2 * HEADS_PER_GROUP, seq), lambda i, g: (g, i)),
            pl.BlockSpec((2 * HEADS_PER_GROUP, 1), lambda i, g: (g, 0)),
            pl.BlockSpec((1, SUBLANE, gw), lambda i, g: (g, 0, 0)),
        ],
        out_specs=pl.BlockSpec((1, seq, gw), lambda i, g: (i, 0, g)),
        out_shape=jax.ShapeDtypeStruct((b, seq, SSD_INNER), BF16),
        scratch_shapes=[
            pltpu.VMEM((SSD_SLAB // LANE, seq + 2 * SUBLANE, LANE), F32),
            pltpu.VMEM((seq, gw), BF16),
            pltpu.VMEM((seq, ns), BF16),
            pltpu.VMEM((seq, ns), BF16),
            pltpu.VMEM((seq // SSD_CHUNK, ns, SSD_CHUNK), BF16),
            pltpu.VMEM((seq // SSD_CHUNK, 2 * HEADS_PER_GROUP, SSD_CHUNK), F32),
            pltpu.VMEM((seq, gw), F32),
            pltpu.VMEM((2, ns, gw), F32),
            pltpu.VMEM((SSD_CHUNK, 2 * SSD_CHUNK), BF16),
            pltpu.VMEM((2, seq // SSD_CHUNK, ns, gw), F32),
            pltpu.VMEM((2, seq // SSD_CHUNK, SSD_CHUNK, gw), F32),
            pltpu.VMEM((2, seq // SSD_CHUNK, SUBLANE, gw), F32),
        ],
        compiler_params=pltpu.CompilerParams(
            dimension_semantics=("parallel", "arbitrary"), vmem_limit_bytes=VMEM_LIMIT),
        name="ssd",
    )(proj3, conv_w, conv_b, dtt, alog_col, p64)


def _win_kernel(sink_ref, qkv_ref, o_ref, bias_s, vext_s, *, seq):
    q_cols = slice(0, WIN_REP * HEAD_DIM)
    k_cols = slice(WIN_REP * HEAD_DIM, (WIN_REP + 1) * HEAD_DIM)
    v_cols = slice((WIN_REP + 1) * HEAD_DIM, (WIN_REP + 2) * HEAD_DIM)
    blk = WIN_HALF
    kwin = 3 * blk
    n_blk = seq // blk
    kv = pl.program_id(1)
    scale2 = (HEAD_DIM ** -0.5) * LOG2E

    ri = lax.broadcasted_iota(jnp.int32, (blk, kwin), 0)
    ci = lax.broadcasted_iota(jnp.int32, (blk, kwin), 1)
    for w, shift in enumerate((0, -blk, -2 * blk)):
        bias_s[w] = jnp.where(jnp.abs(ci + shift - ri) <= WIN_HALF, 0.0, NEG_INF).astype(F32)
    vext_s[:, 0:HEAD_DIM] = qkv_ref[0, :, v_cols]
    vext_s[:, HEAD_DIM:2 * HEAD_DIM] = jnp.ones((seq, HEAD_DIM), BF16)
    sinks = [sink_ref[kv * WIN_REP + r] * LOG2E for r in range(WIN_REP)]

    def blocks(specs):
        loaded = []
        for qi, which in specs:
            q0 = qi * blk if isinstance(qi, int) else pl.multiple_of(qi * blk, blk)
            k0 = (0, q0 - blk, seq - kwin)[which]
            if not isinstance(k0, int):
                k0 = pl.multiple_of(k0, blk)
            loaded.append((q0, qkv_ref[0, pl.ds(q0, blk), q_cols], qkv_ref[0, pl.ds(k0, kwin), k_cols],
                           vext_s[pl.ds(k0, kwin), :], bias_s[which]))
        scores = [[_dot_nt(qt[:, r * HEAD_DIM:(r + 1) * HEAD_DIM], kt) for r in range(WIN_REP)]
                  for _, qt, kt, _, _ in loaded]
        maxes, probs = [], []
        for heads, (_, _, _, _, bias) in zip(scores, loaded):
            for r, s in enumerate(heads):
                s = s * scale2 + bias
                m = jnp.maximum(jnp.max(s, axis=-1, keepdims=True), sinks[r])
                maxes.append(m)
                probs.append(jnp.exp2(s - m).astype(BF16))
        accs = [_dot(probs[i * WIN_REP + r], vt)
                for i, (_, _, _, vt, _) in enumerate(loaded) for r in range(WIN_REP)]
        for i, (q0, _, _, _, _) in enumerate(loaded):
            for r in range(WIN_REP):
                acc, m = accs[i * WIN_REP + r], maxes[i * WIN_REP + r]
                den = acc[:, HEAD_DIM:] + jnp.exp2(sinks[r] - m)
                o_ref[0, pl.ds(q0, blk), r * HEAD_DIM:(r + 1) * HEAD_DIM] = (
                    acc[:, :HEAD_DIM] / den).astype(o_ref.dtype)

    blocks([(0, 0), (n_blk - 1, 2)])
    pairs = (n_blk - 2) // 2

    def body(pi, carry):
        blocks([(1 + 2 * pi, 1), (2 + 2 * pi, 1)])
        return carry

    lax.fori_loop(0, pairs, body, 0)
    if (n_blk - 2) % 2:
        blocks([(n_blk - 2, 1)])


def _win_attn(proj3, sink):
    b, seq, _ = proj3.shape
    assert seq >= 3 * WIN_HALF and seq % WIN_HALF == 0
    qw = WIN_REP * HEAD_DIM
    kern = functools.partial(_win_kernel, seq=seq)
    return pl.pallas_call(
        kern,
        grid=(b, WIN_KV_HEADS),
        in_specs=[
            pl.BlockSpec(memory_space=pltpu.SMEM),
            pl.BlockSpec((1, seq, WIN_SLAB), lambda i, g: (i, 0, OFF_WIN // WIN_SLAB + g)),
        ],
        out_specs=pl.BlockSpec((1, seq, qw), lambda i, g: (i, 0, g)),
        out_shape=jax.ShapeDtypeStruct((b, seq, WIN_Q_HEADS * HEAD_DIM), BF16),
        scratch_shapes=[pltpu.VMEM((3, WIN_HALF, 3 * WIN_HALF), F32),
                        pltpu.VMEM((seq, 2 * HEAD_DIM), BF16)],
        compiler_params=pltpu.CompilerParams(
            dimension_semantics=("parallel", "parallel"), vmem_limit_bytes=VMEM_LIMIT),
        name="win_attn",
    )(sink, proj3)


def _dil_kernel(qkv_ref, o_ref, stage_s, qd_s, kd_s, vext_s, od_s, ld_s, og_s, lg_s, bias_s, bias1_s,
                *, seq):
    def cols(part, gi):
        c0 = (part * DIL_GROUPS + gi) * HEAD_DIM
        return slice(c0, c0 + HEAD_DIM)

    tq = 2 * DIL_HALF
    kw = tq + 2 * DIL_HALF
    scale2 = (HEAD_DIM ** -0.5) * LOG2E

    ri = lax.broadcasted_iota(jnp.int32, (tq, kw), 0)
    ci = lax.broadcasted_iota(jnp.int32, (tq, kw), 1)
    for w, shift in enumerate((0, -DIL_HALF, -2 * DIL_HALF)):
        bias_s[w] = jnp.where(jnp.abs(ci + shift - ri) <= DIL_HALF, 0.0, NEG_INF).astype(F32)
    ri1 = lax.broadcasted_iota(jnp.int32, (tq, tq), 0)
    ci1 = lax.broadcasted_iota(jnp.int32, (tq, tq), 1)
    bias1_s[...] = jnp.where(jnp.abs(ci1 - ri1) <= DIL_HALF, 0.0, NEG_INF).astype(F32)
    vext_s[:, HEAD_DIM:2 * HEAD_DIM] = jnp.ones((seq, HEAD_DIM), BF16)

    def deinterleave(src_cols, dst_ref, dil):
        n_sub = seq // dil
        stage_s[...] = qkv_ref[0, :, src_cols].astype(F32)
        for r in range(dil):
            dst_ref[r * n_sub:(r + 1) * n_sub, 0:HEAD_DIM] = (
                stage_s[pl.ds(r, n_sub, stride=dil), :].astype(BF16))

    def aligned(x, mult):
        return x if isinstance(x, int) else pl.multiple_of(x, mult)

    def run_group(q_at, k_at, n_sub):
        def tiles(specs):
            loaded = []
            for q0, k0, width, bias in specs:
                q0, k0 = aligned(q0, tq), aligned(k0, DIL_HALF)
                loaded.append((q0, q_at(pl.ds(q0, tq)), k_at(pl.ds(k0, width)),
                               vext_s[pl.ds(k0, width), :], bias))
            scores = [_dot_nt(qt, kt) for _, qt, kt, _, _ in loaded]
            maxes, probs = [], []
            for s, (_, _, _, _, bias) in zip(scores, loaded):
                s = s * scale2 + bias
                m = jnp.max(s, axis=-1, keepdims=True)
                maxes.append(m)
                probs.append(jnp.exp2(s - m).astype(BF16))
            accs = [_dot(p, vt) for p, (_, _, _, vt, _) in zip(probs, loaded)]
            for acc, m, (q0, _, _, _, _) in zip(accs, maxes, loaded):
                den = acc[:, HEAD_DIM:]
                od_s[pl.ds(q0, tq), :] = acc[:, :HEAD_DIM] / den
                ld_s[pl.ds(q0, tq), :] = m + jnp.log2(den)

        def first(seg0):
            return (seg0, seg0, kw, bias_s[0])

        def inner(q0):
            return (q0, q0 - DIL_HALF, kw, bias_s[1])

        def last(seg0):
            return (seg0 + n_sub - tq, seg0 + n_sub - kw, kw, bias_s[2])

        tiles_per_seg, n_seg, nb = n_sub // tq, seq // n_sub, TILE_BATCH
        if tiles_per_seg == 1:
            wide = SEGMENT_TILE_BATCH
            def body1(bi, carry):
                tiles([((bi * wide + u) * tq, (bi * wide + u) * tq, tq, bias1_s[...]) for u in range(wide)])
                return carry
            lax.fori_loop(0, seq // (tq * wide), body1, 0)
        elif SEGMENT_TILE_BATCH % tiles_per_seg == 0:
            segs = SEGMENT_TILE_BATCH // tiles_per_seg
            def seg_body(si, carry):
                specs = []
                for u in range(segs):
                    seg0 = (si * segs + u) * n_sub
                    specs += ([first(seg0)] + [inner(seg0 + ti * tq) for ti in range(1, tiles_per_seg - 1)]
                              + [last(seg0)])
                tiles(specs)
                return carry
            lax.fori_loop(0, n_seg // segs, seg_body, 0)
        else:
            assert n_seg == 1
            loops = (tiles_per_seg - 2) // nb
            tiles([first(0)])
            def inner_body(bi, carry):
                tiles([inner((1 + bi * nb + u) * tq) for u in range(nb)])
                return carry
            lax.fori_loop(0, loops, inner_body, 0)
            tiles([inner(ti * tq) for ti in range(1 + loops * nb, tiles_per_seg - 1)] + [last(0)])

    for gi, (_, dil) in enumerate(DIL_PATTERNS):
        n_sub = seq // dil
        if dil == 1:
            vext_s[:, 0:HEAD_DIM] = qkv_ref[0, :, cols(2, gi)]
            run_group(lambda sl: qkv_ref[0, sl, cols(0, gi)], lambda sl: qkv_ref[0, sl, cols(1, gi)], n_sub)
            og_s[gi] = od_s[...]
            lg_s[gi] = ld_s[...]
        else:
            deinterleave(cols(0, gi), qd_s, dil)
            deinterleave(cols(1, gi), kd_s, dil)
            deinterleave(cols(2, gi), vext_s, dil)
            run_group(lambda sl: qd_s[sl, :], lambda sl: kd_s[sl, :], n_sub)
            for r in range(dil):
                og_s[gi, pl.ds(r, n_sub, stride=dil), :] = od_s[r * n_sub:(r + 1) * n_sub, :]
                lg_s[gi, pl.ds(r, n_sub, stride=dil), :] = ld_s[r * n_sub:(r + 1) * n_sub, :]

    m = jnp.maximum(jnp.maximum(lg_s[0], lg_s[1]), lg_s[2])
    num = jnp.zeros((seq, HEAD_DIM), F32)
    den = jnp.zeros((seq, HEAD_DIM), F32)
    for gi in range(DIL_GROUPS):
        w = jnp.exp2(lg_s[gi] - m)
        num = num + w * og_s[gi]
        den = den + w
    o_ref[0] = (num / den).astype(o_ref.dtype)


def _dil_attn(proj3):
    b, seq, _ = proj3.shape
    tq = 2 * DIL_HALF
    for _, dil in DIL_PATTERNS:
        assert (seq // dil) % tq == 0
    kern = functools.partial(_dil_kernel, seq=seq)
    return pl.pallas_call(
        kern,
        grid=(b, DIL_HEADS),
        in_specs=[pl.BlockSpec((1, seq, DIL_SLAB), lambda i, h: (i, 0, OFF_DIL // DIL_SLAB + h))],
        out_specs=pl.BlockSpec((1, seq, HEAD_DIM), lambda i, h: (i, 0, h)),
        out_shape=jax.ShapeDtypeStruct((b, seq, DIL_WIDTH), BF16),
        scratch_shapes=[
            pltpu.VMEM((seq, HEAD_DIM), F32),
            pltpu.VMEM((seq, HEAD_DIM), BF16),
            pltpu.VMEM((seq, HEAD_DIM), BF16),
            pltpu.VMEM((seq, 2 * HEAD_DIM), BF16),
            pltpu.VMEM((seq, HEAD_DIM), F32),
            pltpu.VMEM((seq, HEAD_DIM), F32),
            pltpu.VMEM((DIL_GROUPS, seq, HEAD_DIM), F32),
            pltpu.VMEM((DIL_GROUPS, seq, HEAD_DIM), F32),
            pltpu.VMEM((3, tq, tq + 2 * DIL_HALF), F32),
            pltpu.VMEM((tq, tq), F32),
        ],
        compiler_params=pltpu.CompilerParams(
            dimension_semantics=("parallel", "parallel"), vmem_limit_bytes=VMEM_LIMIT),
        name="dil_attn",
    )(proj3)


def _rope_tables(seq):
    inv = ROPE_THETA ** (-jnp.arange(0, ROPE_DIM, 2, dtype=F32) / ROPE_DIM)
    ang = jnp.arange(seq, dtype=F32)[:, None] * inv[None, :]
    cos, sin = jnp.cos(ang), jnp.sin(ang)
    gap = jnp.zeros((seq, ROPE_PARTNER - ROPE_HALF), F32)
    ct = jnp.concatenate([cos, gap + 1.0, cos, gap + 1.0], axis=1)
    st = jnp.concatenate([-sin, gap, sin, gap], axis=1)
    return ct, st


def _group_major(v):
    return v.reshape(2, SSD_GROUPS, HEADS_PER_GROUP).transpose(1, 0, 2).reshape(DT_COLS)


def _ssd_slabs(a):
    rows = a.shape[0]
    x = a[:, :SSD_INNER].reshape(rows, SSD_GROUPS, GROUP_WIDTH)
    bm = a[:, SSD_INNER:SSD_INNER + SSD_GROUPS * SSD_STATE].reshape(rows, SSD_GROUPS, SSD_STATE)
    cm = a[:, SSD_INNER + SSD_GROUPS * SSD_STATE:].reshape(rows, SSD_GROUPS, SSD_STATE)
    return jnp.concatenate([x, bm, cm], axis=2).reshape(rows, XBC_WIDTH)


def _rope_lanes(a, axis):
    idx = lambda lo, hi: lax.slice_in_dim(a, lo, hi, axis=axis)
    return jnp.concatenate([idx(0, ROPE_HALF), idx(ROPE_DIM, ROPE_DIM + ROPE_PARTNER - ROPE_HALF),
                            idx(ROPE_HALF, ROPE_DIM), idx(ROPE_DIM + ROPE_PARTNER - ROPE_HALF, HEAD_DIM)],
                           axis=axis)


def _main_weight(w_in):
    d = w_in.shape[0]
    o = DT_IN_OFF + DT_COLS
    n_dil = 3 * DIL_GROUPS * DIL_WIDTH
    n_q, n_kv = WIN_Q_HEADS * HEAD_DIM, WIN_KV_HEADS * HEAD_DIM
    z = w_in[:, :SSD_INNER]
    ssd = _ssd_slabs(w_in[:, SSD_INNER:DT_IN_OFF])
    dil = _rope_lanes(w_in[:, o:o + n_dil].reshape(d, 3 * DIL_GROUPS, DIL_HEADS, HEAD_DIM), 3)
    dil = dil.transpose(0, 2, 1, 3).reshape(d, n_dil)
    q = _rope_lanes(w_in[:, o + n_dil:o + n_dil + n_q].reshape(d, WIN_Q_HEADS, HEAD_DIM), 2)
    q = q.reshape(d, WIN_KV_HEADS, WIN_REP * HEAD_DIM)
    k = _rope_lanes(w_in[:, o + n_dil + n_q:o + n_dil + n_q + n_kv].reshape(d, WIN_KV_HEADS, HEAD_DIM), 2)
    v = w_in[:, o + n_dil + n_q + n_kv:o + n_dil + n_q + 2 * n_kv].reshape(d, WIN_KV_HEADS, HEAD_DIM)
    win = jnp.concatenate([q, k, v], axis=2).reshape(d, WIN_KV_HEADS * WIN_SLAB)
    gates = w_in[:, o + n_dil + n_q + 2 * n_kv:]
    return jnp.concatenate([z, ssd, win, dil, gates], axis=1).astype(BF16)


def _layer_params(w_in, dt_bias, a_log, d_skip):
    w_main = _main_weight(w_in)
    w_dt = w_in[:, DT_IN_OFF:DT_IN_OFF + DT_COLS]
    w_dt = w_dt.reshape(D_MODEL, 2, SSD_GROUPS, HEADS_PER_GROUP).transpose(0, 2, 1, 3).reshape(D_MODEL, DT_COLS)
    w_dtt = jnp.pad(w_dt, ((0, 0), (0, DT_LANES - DT_COLS))).astype(BF16).T
    b_dt = jnp.pad(_group_major(dt_bias.astype(F32)), (0, DT_LANES - DT_COLS))
    alog = jnp.pad(_group_major(a_log.astype(F32)), (0, DT_LANES - DT_COLS))
    p64 = jnp.repeat(d_skip.astype(F32).reshape(SSD_GROUPS, 1, HEADS_PER_GROUP), SSD_HEAD_DIM, axis=2)
    p64 = jnp.pad(p64, ((0, 0), (0, SUBLANE - 1), (0, 0)))
    return w_main, w_dtt, b_dt.reshape(DT_LANES, 1), alog.reshape(DT_LANES, 1), p64


def kernel(x, g_mix, w_in, conv_w, conv_b, dt_bias, a_log, d_skip, ssd_norm, w_a, w_b, w_c, sink,
           w_out, g_mlp, w_up, w_down, g_final):
    b, seq, _ = x.shape
    m = b * seq
    tm = min(1024, seq)
    ct, st = _rope_tables(seq)
    xf = x.reshape(m, D_MODEL).astype(F32)
    for i in range(DEPTH):
        w_main, w_dtt, b_dtt, alog_col, p64 = _layer_params(w_in[i], dt_bias[i], a_log[i], d_skip[i])
        proj, dtt = _proj(xf, g_mix[i].reshape(1, D_MODEL), w_main, w_dtt, b_dtt, ct, st, seq,
                          tm=tm, tn=1024)
        proj3 = proj.reshape(b, seq, N_MAIN)
        y_ssd = _ssd(proj3, _ssd_slabs(conv_w[i].astype(F32)),
                     _ssd_slabs(conv_b[i].astype(F32).reshape(1, XBC_WIDTH)), dtt, alog_col, p64)
        y_b = _dil_attn(proj3)
        w_b_heads = _rope_lanes(w_b[i].reshape(DIL_HEADS, HEAD_DIM, D_MODEL), 1)
        y_c = _win_attn(proj3, sink[i].astype(F32))
        merged = _merge(y_ssd.reshape(m, SSD_INNER), proj, ssd_norm[i].reshape(1, SSD_INNER),
                        y_b.reshape(m, DIL_WIDTH), y_c.reshape(m, WIN_Q_HEADS * HEAD_DIM),
                        w_a[i].astype(BF16), w_b_heads.reshape(DIL_WIDTH, D_MODEL).astype(BF16),
                        w_c[i].astype(BF16), tm=tm, tn=256)
        xf = _matmul_res(merged, w_out[i].astype(BF16), xf, tm=tm, tn=1024, tk=D_MODEL)
        u = _mlp_up(xf, g_mlp[i].reshape(1, D_MODEL), w_up[i].astype(BF16), tm=tm, tn=1024)
        xf = _matmul_res(u, w_down[i].astype(BF16), xf, tm=tm, tn=1024, tk=2048)
    out = _final_norm(xf, g_final.reshape(1, D_MODEL), tm=min(512, seq))
    return out.reshape(b, seq, D_MODEL).astype(x.dtype)
```

```python
import functools

import jax
import jax.numpy as jnp
from jax import lax
from jax.experimental import pallas as pl
from jax.experimental.pallas import tpu as pltpu

F32 = jnp.float32
BF16 = jnp.bfloat16

D_MODEL = 2048
DEPTH = 2
SSD_HEADS = 32
SSD_HEAD_DIM = 64
SSD_INNER = SSD_HEADS * SSD_HEAD_DIM
SSD_GROUPS = 8
SSD_STATE = 128
SSD_CHUNK = 128
HEADS_PER_GROUP = SSD_HEADS // SSD_GROUPS
GROUP_WIDTH = HEADS_PER_GROUP * SSD_HEAD_DIM
CONV_WIDTH = 5
CONV_PAD = (CONV_WIDTH - 1) // 2
XBC_WIDTH = SSD_INNER + 2 * SSD_GROUPS * SSD_STATE
HEAD_DIM = 128
ROPE_DIM = HEAD_DIM // 4
ROPE_HALF = ROPE_DIM // 2
ROPE_PARTNER = HEAD_DIM // 2
ROPE_THETA = 500000.0
DIL_PATTERNS = ((128, 1), (512, 4), (2048, 16))
DIL_GROUPS = len(DIL_PATTERNS)
DIL_HEADS = 8
DIL_WIDTH = DIL_HEADS * HEAD_DIM
DIL_HALF = 64
TILE_BATCH = 7
SEGMENT_TILE_BATCH = 8
WIN_Q_HEADS = 16
WIN_KV_HEADS = 4
WIN_REP = WIN_Q_HEADS // WIN_KV_HEADS
WIN_HALF = 128
D_FF = 4 * D_MODEL
N_BRANCH = 3
EPS = 1e-6
NEG_INF = -1e30
LOG2E = 1.4426950408889634

SSD_SLAB = GROUP_WIDTH + 2 * SSD_STATE
WIN_SLAB = (WIN_REP + 2) * HEAD_DIM
DIL_SLAB = 3 * DIL_GROUPS * HEAD_DIM
OFF_Z = 0
OFF_SSD = OFF_Z + SSD_INNER
OFF_WIN = OFF_SSD + SSD_GROUPS * SSD_SLAB
OFF_DIL = OFF_WIN + WIN_KV_HEADS * WIN_SLAB
OFF_GATE = OFF_DIL + DIL_HEADS * DIL_SLAB
N_MAIN = OFF_GATE + N_BRANCH * D_MODEL
assert OFF_SSD % SSD_SLAB == 0 and OFF_WIN % WIN_SLAB == 0 and OFF_DIL % DIL_SLAB == 0
DT_COLS = 2 * SSD_HEADS
DT_IN_OFF = SSD_INNER + XBC_WIDTH
DT_LANES = 128

LANE = 128
SUBLANE = 8
VMEM_LIMIT = 56 * 1024 * 1024


def _split3(x):
    hi = x.astype(BF16)
    r1 = x - hi.astype(F32)
    mid = r1.astype(BF16)
    r2 = r1 - mid.astype(F32)
    return hi, mid, r2.astype(BF16)


def _dot(a, b):
    return jnp.dot(a, b, preferred_element_type=F32)


def _dot_nt(a, b):
    return lax.dot_general(a, b, (((1,), (1,)), ((), ())), preferred_element_type=F32)


def _exact_right(x, mat):
    hi, mid, lo = _split3(x)
    return _dot(hi, mat) + _dot(mid, mat) + _dot(lo, mat)


def _softplus(x):
    return jnp.maximum(x, 0.0) + jnp.log1p(jnp.exp(-jnp.abs(x)))


def _norm_rows(x, g):
    ms = jnp.mean(x * x, axis=-1, keepdims=True)
    return x * lax.rsqrt(ms + EPS) * g


def _rope_tile(a, ct, st):
    return a * ct + pltpu.roll(a, ROPE_PARTNER, 1) * st


def _proj_kernel(x_ref, g_ref, w_ref, wdtt_ref, bdtt_ref, ct_ref, st_ref,
                 o_ref, dtt_ref, h_ref, *, tn, tile_kinds):
    j = pl.program_id(1)

    @pl.when(j == 0)
    def _():
        hb = _norm_rows(x_ref[...], g_ref[...]).astype(BF16)
        h_ref[...] = hb
        dtt_ref[...] = _softplus(_dot_nt(wdtt_ref[...], hb) + bdtt_ref[...])

    def tile_body(kind):
        acc = _dot(h_ref[...], w_ref[...])
        if 'r' in kind:
            ct, st = ct_ref[...], st_ref[...]
        for c, k in enumerate(kind):
            sl = slice(c * HEAD_DIM, (c + 1) * HEAD_DIM)
            a = acc[:, sl]
            if k == 'r':
                a = _rope_tile(a, ct, st)
            elif k == 'g':
                a = 0.5 * jnp.tanh(0.5 * a) + 0.5
            o_ref[:, sl] = a.astype(o_ref.dtype)

    for kind in sorted(set(tile_kinds)):
        idx = [t for t, k in enumerate(tile_kinds) if k == kind]
        runs, start = [], idx[0]
        for a, b in zip(idx, idx[1:] + [None]):
            if b != a + 1:
                runs.append((start, a))
                start = b
        cond = None
        for lo, hi in runs:
            c = jnp.logical_and(j >= lo, j <= hi)
            cond = c if cond is None else jnp.logical_or(cond, c)
        pl.when(cond)(functools.partial(tile_body, kind))


def _proj_tile_kinds(tn):
    def head_kind(col):
        if col >= OFF_GATE:
            return 'g'
        if col >= OFF_DIL:
            return 'r'
        if col >= OFF_WIN:
            return 'r' if (col - OFF_WIN) % WIN_SLAB < (WIN_REP + 1) * HEAD_DIM else 'p'
        return 'p'
    return tuple(''.join(head_kind(t * tn + c * HEAD_DIM) for c in range(tn // HEAD_DIM))
                 for t in range(N_MAIN // tn))


def _proj(x2d, g, w_main, w_dtt, b_dtt, ct, st, seq, *, tm, tn):
    m = x2d.shape[0]
    assert seq % tm == 0 and m % tm == 0 and N_MAIN % tn == 0 and tn % HEAD_DIM == 0
    seq_tiles = seq // tm
    kern = functools.partial(_proj_kernel, tn=tn, tile_kinds=_proj_tile_kinds(tn))
    return pl.pallas_call(
        kern,
        grid=(m // tm, N_MAIN // tn),
        in_specs=[
            pl.BlockSpec((tm, D_MODEL), lambda i, j: (i, 0)),
            pl.BlockSpec((1, D_MODEL), lambda i, j: (0, 0)),
            pl.BlockSpec((D_MODEL, tn), lambda i, j: (0, j)),
            pl.BlockSpec((DT_LANES, D_MODEL), lambda i, j: (0, 0)),
            pl.BlockSpec((DT_LANES, 1), lambda i, j: (0, 0)),
            pl.BlockSpec((tm, HEAD_DIM), lambda i, j: (i % seq_tiles, 0)),
            pl.BlockSpec((tm, HEAD_DIM), lambda i, j: (i % seq_tiles, 0)),
        ],
        out_specs=[
            pl.BlockSpec((tm, tn), lambda i, j: (i, j)),
            pl.BlockSpec((DT_LANES, tm), lambda i, j: (0, i)),
        ],
        out_shape=[
            jax.ShapeDtypeStruct((m, N_MAIN), BF16),
            jax.ShapeDtypeStruct((DT_LANES, m), F32),
        ],
        scratch_shapes=[pltpu.VMEM((tm, D_MODEL), BF16)],
        compiler_params=pltpu.CompilerParams(
            dimension_semantics=("parallel", "arbitrary"), vmem_limit_bytes=VMEM_LIMIT),
        name="proj",
    )(x2d, g, w_main, w_dtt, b_dtt, ct, st)


def _up_kernel(x_ref, g_ref, w_ref, o_ref, h_ref):
    @pl.when(pl.program_id(1) == 0)
    def _():
        h_ref[...] = _norm_rows(x_ref[...], g_ref[...]).astype(BF16)

    acc = jnp.maximum(_dot(h_ref[...], w_ref[...]), 0.0)
    o_ref[...] = (acc * acc).astype(o_ref.dtype)


def _mlp_up(x2d, g, w, *, tm, tn):
    m, n = x2d.shape[0], w.shape[1]
    return pl.pallas_call(
        _up_kernel,
        grid=(m // tm, n // tn),
        in_specs=[
            pl.BlockSpec((tm, D_MODEL), lambda i, j: (i, 0)),
            pl.BlockSpec((1, D_MODEL), lambda i, j: (0, 0)),
            pl.BlockSpec((D_MODEL, tn), lambda i, j: (0, j)),
        ],
        out_specs=pl.BlockSpec((tm, tn), lambda i, j: (i, j)),
        out_shape=jax.ShapeDtypeStruct((m, n), BF16),
        scratch_shapes=[pltpu.VMEM((tm, D_MODEL), BF16)],
        compiler_params=pltpu.CompilerParams(
            dimension_semantics=("parallel", "arbitrary"), vmem_limit_bytes=VMEM_LIMIT),
        name="mlp_up",
    )(x2d, g, w)


def _matmul_res_kernel(a_ref, w_ref, r_ref, o_ref, acc_ref):
    k = pl.program_id(2)

    @pl.when(k == 0)
    def _():
        acc_ref[...] = jnp.zeros_like(acc_ref)

    acc_ref[...] += _dot(a_ref[...], w_ref[...])

    @pl.when(k == pl.num_programs(2) - 1)
    def _():
        o_ref[...] = r_ref[...] + acc_ref[...]


def _matmul_res(a, w, res, *, tm, tn, tk):
    m, kdim = a.shape
    n = w.shape[1]
    return pl.pallas_call(
        _matmul_res_kernel,
        grid=(m // tm, n // tn, kdim // tk),
        in_specs=[
            pl.BlockSpec((tm, tk), lambda i, j, k: (i, k)),
            pl.BlockSpec((tk, tn), lambda i, j, k: (k, j)),
            pl.BlockSpec((tm, tn), lambda i, j, k: (i, j)),
        ],
        out_specs=pl.BlockSpec((tm, tn), lambda i, j, k: (i, j)),
        out_shape=jax.ShapeDtypeStruct((m, n), F32),
        scratch_shapes=[pltpu.VMEM((tm, tn), F32)],
        compiler_params=pltpu.CompilerParams(
            dimension_semantics=("parallel", "parallel", "arbitrary"), vmem_limit_bytes=VMEM_LIMIT),
        name="matmul_res",
    )(a, w, res)


def _gated_norm(ys, z, nw):
    z = z.astype(F32)
    y = ys.astype(F32) * (z / (1.0 + jnp.exp(-z)))
    return _norm_rows(y, nw).astype(BF16)


def _merge_kernel(ys_ref, z_ref, nw_ref, yb_ref, yc_ref, g0_ref, g1_ref, g2_ref,
                  wa_ref, wb_ref, wc_ref, o_ref, ya_ref):
    @pl.when(pl.program_id(1) == 0)
    def _():
        ya_ref[...] = _gated_norm(ys_ref[...], z_ref[...], nw_ref[...])

    acc = g0_ref[...].astype(F32) * _dot(ya_ref[...], wa_ref[...])
    acc += g1_ref[...].astype(F32) * _dot(yb_ref[...], wb_ref[...])
    acc += g2_ref[...].astype(F32) * _dot(yc_ref[...], wc_ref[...])
    o_ref[...] = acc.astype(o_ref.dtype)


def _merge(y_ssd, proj, norm_w, y_b, y_c, w_a, w_b, w_c, *, tm, tn):
    m = y_ssd.shape[0]
    gate_blk = OFF_GATE // tn
    per_gate = D_MODEL // tn
    return pl.pallas_call(
        _merge_kernel,
        grid=(m // tm, D_MODEL // tn),
        in_specs=[
            pl.BlockSpec((tm, SSD_INNER), lambda i, j: (i, 0)),
            pl.BlockSpec((tm, SSD_INNER), lambda i, j: (i, OFF_Z // SSD_INNER)),
            pl.BlockSpec((1, SSD_INNER), lambda i, j: (0, 0)),
            pl.BlockSpec((tm, DIL_WIDTH), lambda i, j: (i, 0)),
            pl.BlockSpec((tm, WIN_Q_HEADS * HEAD_DIM), lambda i, j: (i, 0)),
            pl.BlockSpec((tm, tn), lambda i, j: (i, gate_blk + j)),
            pl.BlockSpec((tm, tn), lambda i, j: (i, gate_blk + per_gate + j)),
            pl.BlockSpec((tm, tn), lambda i, j: (i, gate_blk + 2 * per_gate + j)),
            pl.BlockSpec((SSD_INNER, tn), lambda i, j: (0, j)),
            pl.BlockSpec((DIL_WIDTH, tn), lambda i, j: (0, j)),
            pl.BlockSpec((WIN_Q_HEADS * HEAD_DIM, tn), lambda i, j: (0, j)),
        ],
        out_specs=pl.BlockSpec((tm, tn), lambda i, j: (i, j)),
        out_shape=jax.ShapeDtypeStruct((m, D_MODEL), BF16),
        scratch_shapes=[pltpu.VMEM((tm, SSD_INNER), BF16)],
        compiler_params=pltpu.CompilerParams(
            dimension_semantics=("parallel", "arbitrary"), vmem_limit_bytes=VMEM_LIMIT),
        name="merge",
    )(y_ssd, proj, norm_w, y_b, y_c, proj, proj, proj, w_a, w_b, w_c)


def _final_norm_kernel(x_ref, g_ref, o_ref):
    o_ref[...] = _norm_rows(x_ref[...], g_ref[...])


def _final_norm(x2d, g, *, tm):
    m = x2d.shape[0]
    return pl.pallas_call(
        _final_norm_kernel,
        grid=(m // tm,),
        in_specs=[pl.BlockSpec((tm, D_MODEL), lambda i: (i, 0)),
                  pl.BlockSpec((1, D_MODEL), lambda i: (0, 0))],
        out_specs=pl.BlockSpec((tm, D_MODEL), lambda i: (i, 0)),
        out_shape=jax.ShapeDtypeStruct((m, D_MODEL), F32),
        compiler_params=pltpu.CompilerParams(dimension_semantics=("parallel",)),
        name="final_norm",
    )(x2d, g)


def _ssd_kernel(xbc_ref, cw_ref, cb_ref, dtt_ref, alt_ref, p64_ref,
                y_ref,
                pad_s, xs_s, bm_s, cm_s, bmt_s, dtr_s, y_s, st_s, tri_s, u_s, e_s, scl_s, *, seq):
    t = SSD_CHUNK
    n_chunks = seq // t
    for c in range(n_chunks):
        dtr_s[c] = dtt_ref[:, c * t:(c + 1) * t]

    halo = SUBLANE
    for lane_tile in range(SSD_SLAB // LANE):
        lanes = slice(lane_tile * LANE, (lane_tile + 1) * LANE)
        pad_s[lane_tile, 0:halo, :] = jnp.zeros((halo, LANE), F32)
        pad_s[lane_tile, halo + seq:2 * halo + seq, :] = jnp.zeros((halo, LANE), F32)
        pad_s[lane_tile, halo:halo + seq, :] = xbc_ref[0, :, lanes].astype(F32)

    def conv_silu(r0, lane_tile):
        lanes = slice(lane_tile * LANE, (lane_tile + 1) * LANE)
        acc = cb_ref[:, lanes]
        for k in range(CONV_WIDTH):
            lo = halo - CONV_PAD + k + r0
            acc = acc + cw_ref[k:k + 1, lanes] * pad_s[lane_tile, lo:lo + t, :]
        return acc / (1.0 + jnp.exp(-acc))

    x_tiles = GROUP_WIDTH // LANE
    for c in range(n_chunks):
        r0 = c * t
        rows = slice(r0, r0 + t)
        for lane_tile in range(x_tiles):
            lanes = slice(lane_tile * LANE, (lane_tile + 1) * LANE)
            xv = conv_silu(r0, lane_tile)
            xs_s[rows, lanes] = xv.astype(BF16)
            y_s[rows, lanes] = xv * p64_ref[0, 0:1, lanes]
        bv = conv_silu(r0, x_tiles)
        bm_s[rows, :] = bv.astype(BF16)
        bmt_s[c] = bv.T.astype(BF16)
        cm_s[rows, :] = conv_silu(r0, x_tiles + 1).astype(BF16)

    ri = lax.broadcasted_iota(jnp.int32, (t, t), 0)
    ci = lax.broadcasted_iota(jnp.int32, (t, t), 1)
    lower, upper = ri >= ci, ri <= ci
    tri_s[:, 0:t] = upper.astype(BF16)
    tri_s[:, t:2 * t] = lower.astype(BF16)
    st_s[...] = jnp.zeros_like(st_s)

    nh = HEADS_PER_GROUP
    a_rows = -jnp.exp(alt_ref[...])
    head_of_lane = lax.broadcasted_iota(jnp.int32, (1, GROUP_WIDTH), 1) // SSD_HEAD_DIM
    low_half = lax.broadcasted_iota(jnp.int32, (1, LANE), 1) < SSD_HEAD_DIM

    def lanes_from_row(row):
        return jnp.broadcast_to(row, (t, t)).T

    def expand64(cols):
        return jnp.concatenate([jnp.where(low_half, cols[0], cols[1]),
                                jnp.where(low_half, cols[2], cols[3])], axis=1)

    def prepare(chunks):
        loaded = []
        for c in chunks:
            rows = pl.ds(pl.multiple_of(c * t, t), t)
            loaded.append((c, rows, dtr_s[c], xs_s[rows, :], cm_s[rows, :], bm_s[rows, :], bmt_s[c]))
        cums = [_exact_right(dt_all, tri_s[...]) for _, _, dt_all, _, _, _, _ in loaded]
        cbs = [_dot_nt(cm_c, bm_c) for _, _, _, _, cm_c, bm_c, _ in loaded]
        staged = []
        for (c, rows, dt_all, xs_c, _, _, bmt_c), cum, cb in zip(loaded, cums, cbs):
            cs = (cum[0:nh, 0:t] * a_rows[0:nh], cum[nh:2 * nh, t:2 * t] * a_rows[nh:2 * nh])
            dts = (dt_all[0:nh, :], dt_all[nh:2 * nh, :])
            tots = (cs[0][:, t - 1:t], cs[1][:, 0:1])
            w_rows = [dts[d] * jnp.exp(tots[d] - cs[d]) for d in range(2)]
            grow = [jnp.exp(tots[d]) for d in range(2)]
            bmt_f = bmt_c.astype(F32)
            l_parts, bw, seen, scale = [], ([], []), ([], []), [0.0, 0.0]
            for h in range(nh):
                mix = jnp.zeros((t, t), F32)
                for d, valid in ((0, lower), (1, upper)):
                    row = cs[d][h:h + 1, :]
                    col = lanes_from_row(row)
                    mix = mix + jnp.exp(jnp.where(valid, col - row, NEG_INF)) * dts[d][h:h + 1, :]
                    seen[d].append(jnp.exp(col))
                    bw[d].append((bmt_f * w_rows[d][h:h + 1, :]).astype(BF16))
                    scale[d] = scale[d] + jnp.where(head_of_lane == h, grow[d][h:h + 1, :], 0.0)
                l_parts.append((cb * mix).astype(BF16))
            lhs = jnp.concatenate([jnp.concatenate(l_parts, axis=1), jnp.concatenate(bw[0], axis=1),
                                   jnp.concatenate(bw[1], axis=1)], axis=0)
            xs_stack = jnp.concatenate(
                [jnp.where(head_of_lane == h, xs_c, jnp.zeros_like(xs_c)) for h in range(nh)], axis=0)
            staged.append((c, rows, lhs, xs_stack, seen, scale))
        outs = [_dot(lhs, xs_stack) for _, _, lhs, xs_stack, _, _ in staged]
        for (c, rows, _, _, seen, scale), out in zip(staged, outs):
            y_s[rows, :] += out[0:t, :]
            for d in range(2):
                u_s[d, c] = out[(d + 1) * t:(d + 2) * t, :]
                e_s[d, c] = expand64(seen[d])
                scl_s[d, c] = jnp.broadcast_to(scale[d], (SUBLANE, GROUP_WIDTH))

    batch = 8
    def prep_body(i, carry):
        prepare([i * batch + u for u in range(batch)])
        return carry

    lax.fori_loop(0, n_chunks // batch, prep_body, 0)

    def scan_body(i, carry):
        steps = []
        for d, c in ((0, i), (1, n_chunks - 1 - i)):
            rows = pl.ds(pl.multiple_of(c * t, t), t)
            steps.append((d, c, rows, st_s[d], cm_s[rows, :]))
        reads = [_dot(cm_c, st.astype(BF16)) for _, _, _, st, cm_c in steps]
        for (d, c, rows, st, _), read in zip(steps, reads):
            y_s[rows, :] += read * e_s[d, c]
            st_s[d] = st * scl_s[d, c][0:1, :] + u_s[d, c]
        return carry

    lax.fori_loop(0, n_chunks, scan_body, 0, unroll=8)
    y_ref[0] = y_s[...].astype(y_ref.dtype)


def _ssd(proj3, conv_w, conv_b, dtt, alog_col, p64):
    b, seq, _ = proj3.shape
    gw, ns = GROUP_WIDTH, SSD_STATE
    slab0 = OFF_SSD // SSD_SLAB
    kern = functools.partial(_ssd_kernel, seq=seq)
    return pl.pallas_call(
        kern,
        grid=(b, SSD_GROUPS),
        in_specs=[
            pl.BlockSpec((1, seq, SSD_SLAB), lambda i, g: (i, 0, slab0 + g)),
            pl.BlockSpec((CONV_WIDTH, SSD_SLAB), lambda i, g: (0, g)),
            pl.BlockSpec((1, SSD_SLAB), lambda i, g: (0, g)),
            pl.BlockSpec((2 * HEADS_PER_GROUP, seq), lambda i, g: (g, i)),
            pl.BlockSpec((2 * HEADS_PER_GROUP, 1), lambda i, g: (g, 0)),
            pl.BlockSpec((1, SUBLANE, gw), lambda i, g: (g, 0, 0)),
        ],
        out_specs=pl.BlockSpec((1, seq, gw), lambda i, g: (i, 0, g)),
        out_shape=jax.ShapeDtypeStruct((b, seq, SSD_INNER), BF16),
        scratch_shapes=[
            pltpu.VMEM((SSD_SLAB // LANE, seq + 2 * SUBLANE, LANE), F32),
            pltpu.VMEM((seq, gw), BF16),
            pltpu.VMEM((seq, ns), BF16),
            pltpu.VMEM((seq, ns), BF16),
            pltpu.VMEM((seq // SSD_CHUNK, ns, SSD_CHUNK), BF16),
            pltpu.VMEM((seq // SSD_CHUNK, 2 * HEADS_PER_GROUP, SSD_CHUNK), F32),
            pltpu.VMEM((seq, gw), F32),
            pltpu.VMEM((2, ns, gw), F32),
            pltpu.VMEM((SSD_CHUNK, 2 * SSD_CHUNK), BF16),
            pltpu.VMEM((2, seq // SSD_CHUNK, ns, gw), F32),
            pltpu.VMEM((2, seq // SSD_CHUNK, SSD_CHUNK, gw), F32),
            pltpu.VMEM((2, seq // SSD_CHUNK, SUBLANE, gw), F32),
        ],
        compiler_params=pltpu.CompilerParams(
            dimension_semantics=("parallel", "arbitrary"), vmem_limit_bytes=VMEM_LIMIT),
        name="ssd",
    )(proj3, conv_w, conv_b, dtt, alog_col, p64)


def _win_kernel(sink_ref, qkv_ref, o_ref, bias_s, vext_s, *, seq):
    q_cols = slice(0, WIN_REP * HEAD_DIM)
    k_cols = slice(WIN_REP * HEAD_DIM, (WIN_REP + 1) * HEAD_DIM)
    v_cols = slice((WIN_REP + 1) * HEAD_DIM, (WIN_REP + 2) * HEAD_DIM)
    blk = WIN_HALF
    kwin = 3 * blk
    n_blk = seq // blk
    kv = pl.program_id(1)
    scale2 = (HEAD_DIM ** -0.5) * LOG2E

    ri = lax.broadcasted_iota(jnp.int32, (blk, kwin), 0)
    ci = lax.broadcasted_iota(jnp.int32, (blk, kwin), 1)
    for w, shift in enumerate((0, -blk, -2 * blk)):
        bias_s[w] = jnp.where(jnp.abs(ci + shift - ri) <= WIN_HALF, 0.0, NEG_INF).astype(F32)
    vext_s[:, 0:HEAD_DIM] = qkv_ref[0, :, v_cols]
    vext_s[:, HEAD_DIM:2 * HEAD_DIM] = jnp.ones((seq, HEAD_DIM), BF16)
    sinks = [sink_ref[kv * WIN_REP + r] * LOG2E for r in range(WIN_REP)]

    def blocks(specs):
        loaded = []
        for qi, which in specs:
            q0 = qi * blk if isinstance(qi, int) else pl.multiple_of(qi * blk, blk)
            k0 = (0, q0 - blk, seq - kwin)[which]
            if not isinstance(k0, int):
                k0 = pl.multiple_of(k0, blk)
            loaded.append((q0, qkv_ref[0, pl.ds(q0, blk), q_cols], qkv_ref[0, pl.ds(k0, kwin), k_cols],
                           vext_s[pl.ds(k0, kwin), :], bias_s[which]))
        scores = [[_dot_nt(qt[:, r * HEAD_DIM:(r + 1) * HEAD_DIM], kt) for r in range(WIN_REP)]
                  for _, qt, kt, _, _ in loaded]
        maxes, probs = [], []
        for heads, (_, _, _, _, bias) in zip(scores, loaded):
            for r, s in enumerate(heads):
                s = s * scale2 + bias
                m = jnp.maximum(jnp.max(s, axis=-1, keepdims=True), sinks[r])
                maxes.append(m)
                probs.append(jnp.exp2(s - m).astype(BF16))
        accs = [_dot(probs[i * WIN_REP + r], vt)
                for i, (_, _, _, vt, _) in enumerate(loaded) for r in range(WIN_REP)]
        for i, (q0, _, _, _, _) in enumerate(loaded):
            for r in range(WIN_REP):
                acc, m = accs[i * WIN_REP + r], maxes[i * WIN_REP + r]
                den = acc[:, HEAD_DIM:] + jnp.exp2(sinks[r] - m)
                o_ref[0, pl.ds(q0, blk), r * HEAD_DIM:(r + 1) * HEAD_DIM] = (
                    acc[:, :HEAD_DIM] / den).astype(o_ref.dtype)

    blocks([(0, 0), (n_blk - 1, 2)])
    pairs = (n_blk - 2) // 2

    def body(pi, carry):
        blocks([(1 + 2 * pi, 1), (2 + 2 * pi, 1)])
        return carry

    lax.fori_loop(0, pairs, body, 0)
    if (n_blk - 2) % 2:
        blocks([(n_blk - 2, 1)])


def _win_attn(proj3, sink):
    b, seq, _ = proj3.shape
    assert seq >= 3 * WIN_HALF and seq % WIN_HALF == 0
    qw = WIN_REP * HEAD_DIM
    kern = functools.partial(_win_kernel, seq=seq)
    return pl.pallas_call(
        kern,
        grid=(b, WIN_KV_HEADS),
        in_specs=[
            pl.BlockSpec(memory_space=pltpu.SMEM),
            pl.BlockSpec((1, seq, WIN_SLAB), lambda i, g: (i, 0, OFF_WIN // WIN_SLAB + g)),
        ],
        out_specs=pl.BlockSpec((1, seq, qw), lambda i, g: (i, 0, g)),
        out_shape=jax.ShapeDtypeStruct((b, seq, WIN_Q_HEADS * HEAD_DIM), BF16),
        scratch_shapes=[pltpu.VMEM((3, WIN_HALF, 3 * WIN_HALF), F32),
                        pltpu.VMEM((seq, 2 * HEAD_DIM), BF16)],
        compiler_params=pltpu.CompilerParams(
            dimension_semantics=("parallel", "parallel"), vmem_limit_bytes=VMEM_LIMIT),
        name="win_attn",
    )(sink, proj3)


def _dil_kernel(qkv_ref, o_ref, stage_s, qd_s, kd_s, vext_s, od_s, ld_s, og_s, lg_s, bias_s, bias1_s,
                *, seq):
    def cols(part, gi):
        c0 = (part * DIL_GROUPS + gi) * HEAD_DIM
        return slice(c0, c0 + HEAD_DIM)

    tq = 2 * DIL_HALF
    kw = tq + 2 * DIL_HALF
    scale2 = (HEAD_DIM ** -0.5) * LOG2E

    ri = lax.broadcasted_iota(jnp.int32, (tq, kw), 0)
    ci = lax.broadcasted_iota(jnp.int32, (tq, kw), 1)
    for w, shift in enumerate((0, -DIL_HALF, -2 * DIL_HALF)):
        bias_s[w] = jnp.where(jnp.abs(ci + shift - ri) <= DIL_HALF, 0.0, NEG_INF).astype(F32)
    ri1 = lax.broadcasted_iota(jnp.int32, (tq, tq), 0)
    ci1 = lax.broadcasted_iota(jnp.int32, (tq, tq), 1)
    bias1_s[...] = jnp.where(jnp.abs(ci1 - ri1) <= DIL_HALF, 0.0, NEG_INF).astype(F32)
    vext_s[:, HEAD_DIM:2 * HEAD_DIM] = jnp.ones((seq, HEAD_DIM), BF16)

    def deinterleave(src_cols, dst_ref, dil):
        n_sub = seq // dil
        stage_s[...] = qkv_ref[0, :, src_cols].astype(F32)
        for r in range(dil):
            dst_ref[r * n_sub:(r + 1) * n_sub, 0:HEAD_DIM] = (
                stage_s[pl.ds(r, n_sub, stride=dil), :].astype(BF16))

    def aligned(x, mult):
        return x if isinstance(x, int) else pl.multiple_of(x, mult)

    def run_group(q_at, k_at, n_sub):
        def tiles(specs):
            loaded = []
            for q0, k0, width, bias in specs:
                q0, k0 = aligned(q0, tq), aligned(k0, DIL_HALF)
                loaded.append((q0, q_at(pl.ds(q0, tq)), k_at(pl.ds(k0, width)),
                               vext_s[pl.ds(k0, width), :], bias))
            scores = [_dot_nt(qt, kt) for _, qt, kt, _, _ in loaded]
            maxes, probs = [], []
            for s, (_, _, _, _, bias) in zip(scores, loaded):
                s = s * scale2 + bias
                m = jnp.max(s, axis=-1, keepdims=True)
                maxes.append(m)
                probs.append(jnp.exp2(s - m).astype(BF16))
            accs = [_dot(p, vt) for p, (_, _, _, vt, _) in zip(probs, loaded)]
            for acc, m, (q0, _, _, _, _) in zip(accs, maxes, loaded):
                den = acc[:, HEAD_DIM:]
                od_s[pl.ds(q0, tq), :] = acc[:, :HEAD_DIM] / den
                ld_s[pl.ds(q0, tq), :] = m + jnp.log2(den)

        def first(seg0):
            return (seg0, seg0, kw, bias_s[0])

        def inner(q0):
            return (q0, q0 - DIL_HALF, kw, bias_s[1])

        def last(seg0):
            return (seg0 + n_sub - tq, seg0 + n_sub - kw, kw, bias_s[2])

        tiles_per_seg, n_seg, nb = n_sub // tq, seq // n_sub, TILE_BATCH
        if tiles_per_seg == 1:
            wide = SEGMENT_TILE_BATCH
            def body1(bi, carry):
                tiles([((bi * wide + u) * tq, (bi * wide + u) * tq, tq, bias1_s[...]) for u in range(wide)])
                return carry
            lax.fori_loop(0, seq // (tq * wide), body1, 0)
        elif SEGMENT_TILE_BATCH % tiles_per_seg == 0:
            segs = SEGMENT_TILE_BATCH // tiles_per_seg
            def seg_body(si, carry):
                specs = []
                for u in range(segs):
                    seg0 = (si * segs + u) * n_sub
                    specs += ([first(seg0)] + [inner(seg0 + ti * tq) for ti in range(1, tiles_per_seg - 1)]
                              + [last(seg0)])
                tiles(specs)
                return carry
            lax.fori_loop(0, n_seg // segs, seg_body, 0)
        else:
            assert n_seg == 1
            loops = (tiles_per_seg - 2) // nb
            tiles([first(0)])
            def inner_body(bi, carry):
                tiles([inner((1 + bi * nb + u) * tq) for u in range(nb)])
                return carry
            lax.fori_loop(0, loops, inner_body, 0)
            tiles([inner(ti * tq) for ti in range(1 + loops * nb, tiles_per_seg - 1)] + [last(0)])

    for gi, (_, dil) in enumerate(DIL_PATTERNS):
        n_sub = seq // dil
        if dil == 1:
            vext_s[:, 0:HEAD_DIM] = qkv_ref[0, :, cols(2, gi)]
            run_group(lambda sl: qkv_ref[0, sl, cols(0, gi)], lambda sl: qkv_ref[0, sl, cols(1, gi)], n_sub)
            og_s[gi] = od_s[...]
            lg_s[gi] = ld_s[...]
        else:
            deinterleave(cols(0, gi), qd_s, dil)
            deinterleave(cols(1, gi), kd_s, dil)
            deinterleave(cols(2, gi), vext_s, dil)
            run_group(lambda sl: qd_s[sl, :], lambda sl: kd_s[sl, :], n_sub)
            for r in range(dil):
                og_s[gi, pl.ds(r, n_sub, stride=dil), :] = od_s[r * n_sub:(r + 1) * n_sub, :]
                lg_s[gi, pl.ds(r, n_sub, stride=dil), :] = ld_s[r * n_sub:(r + 1) * n_sub, :]

    m = jnp.maximum(jnp.maximum(lg_s[0], lg_s[1]), lg_s[2])
    num = jnp.zeros((seq, HEAD_DIM), F32)
    den = jnp.zeros((seq, HEAD_DIM), F32)
    for gi in range(DIL_GROUPS):
        w = jnp.exp2(lg_s[gi] - m)
        num = num + w * og_s[gi]
        den = den + w
    o_ref[0] = (num / den).astype(o_ref.dtype)


def _dil_attn(proj3):
    b, seq, _ = proj3.shape
    tq = 2 * DIL_HALF
    for _, dil in DIL_PATTERNS:
        assert (seq // dil) % tq == 0
    kern = functools.partial(_dil_kernel, seq=seq)
    return pl.pallas_call(
        kern,
        grid=(b, DIL_HEADS),
        in_specs=[pl.BlockSpec((1, seq, DIL_SLAB), lambda i, h: (i, 0, OFF_DIL // DIL_SLAB + h))],
        out_specs=pl.BlockSpec((1, seq, HEAD_DIM), lambda i, h: (i, 0, h)),
        out_shape=jax.ShapeDtypeStruct((b, seq, DIL_WIDTH), BF16),
        scratch_shapes=[
            pltpu.VMEM((seq, HEAD_DIM), F32),
            pltpu.VMEM((seq, HEAD_DIM), BF16),
            pltpu.VMEM((seq, HEAD_DIM), BF16),
            pltpu.VMEM((seq, 2 * HEAD_DIM), BF16),
            pltpu.VMEM((seq, HEAD_DIM), F32),
            pltpu.VMEM((seq, HEAD_DIM), F32),
            pltpu.VMEM((DIL_GROUPS, seq, HEAD_DIM), F32),
            pltpu.VMEM((DIL_GROUPS, seq, HEAD_DIM), F32),
            pltpu.VMEM((3, tq, tq + 2 * DIL_HALF), F32),
            pltpu.VMEM((tq, tq), F32),
        ],
        compiler_params=pltpu.CompilerParams(
            dimension_semantics=("parallel", "parallel"), vmem_limit_bytes=VMEM_LIMIT),
        name="dil_attn",
    )(proj3)


def _rope_tables(seq):
    inv = ROPE_THETA ** (-jnp.arange(0, ROPE_DIM, 2, dtype=F32) / ROPE_DIM)
    ang = jnp.arange(seq, dtype=F32)[:, None] * inv[None, :]
    cos, sin = jnp.cos(ang), jnp.sin(ang)
    gap = jnp.zeros((seq, ROPE_PARTNER - ROPE_HALF), F32)
    ct = jnp.concatenate([cos, gap + 1.0, cos, gap + 1.0], axis=1)
    st = jnp.concatenate([-sin, gap, sin, gap], axis=1)
    return ct, st


def _group_major(v):
    return v.reshape(2, SSD_GROUPS, HEADS_PER_GROUP).transpose(1, 0, 2).reshape(DT_COLS)


def _ssd_slabs(a):
    rows = a.shape[0]
    x = a[:, :SSD_INNER].reshape(rows, SSD_GROUPS, GROUP_WIDTH)
    bm = a[:, SSD_INNER:SSD_INNER + SSD_GROUPS * SSD_STATE].reshape(rows, SSD_GROUPS, SSD_STATE)
    cm = a[:, SSD_INNER + SSD_GROUPS * SSD_STATE:].reshape(rows, SSD_GROUPS, SSD_STATE)
    return jnp.concatenate([x, bm, cm], axis=2).reshape(rows, XBC_WIDTH)


def _rope_lanes(a, axis):
    idx = lambda lo, hi: lax.slice_in_dim(a, lo, hi, axis=axis)
    return jnp.concatenate([idx(0, ROPE_HALF), idx(ROPE_DIM, ROPE_DIM + ROPE_PARTNER - ROPE_HALF),
                            idx(ROPE_HALF, ROPE_DIM), idx(ROPE_DIM + ROPE_PARTNER - ROPE_HALF, HEAD_DIM)],
                           axis=axis)


def _main_weight(w_in):
    d = w_in.shape[0]
    o = DT_IN_OFF + DT_COLS
    n_dil = 3 * DIL_GROUPS * DIL_WIDTH
    n_q, n_kv = WIN_Q_HEADS * HEAD_DIM, WIN_KV_HEADS * HEAD_DIM
    w_in = w_in.astype(BF16)
    lane_perm = _rope_lanes(jnp.eye(HEAD_DIM, dtype=BF16), 1)

    def rope_lanes(heads):
        return jnp.einsum('dhk,kj->dhj', heads, lane_perm, preferred_element_type=BF16)

    z = w_in[:, :SSD_INNER]
    ssd = _ssd_slabs(w_in[:, SSD_INNER:DT_IN_OFF])
    dil = rope_lanes(w_in[:, o:o + n_dil].reshape(d, 3 * DIL_GROUPS * DIL_HEADS, HEAD_DIM))
    dil = dil.reshape(d, 3 * DIL_GROUPS, DIL_HEADS, HEAD_DIM)
    dil = dil.transpose(0, 2, 1, 3).reshape(d, n_dil)
    qk = rope_lanes(w_in[:, o + n_dil:o + n_dil + n_q + n_kv].reshape(d, WIN_Q_HEADS + WIN_KV_HEADS, HEAD_DIM))
    q = qk[:, :WIN_Q_HEADS].reshape(d, WIN_KV_HEADS, WIN_REP * HEAD_DIM)
    k = qk[:, WIN_Q_HEADS:]
    v = w_in[:, o + n_dil + n_q + n_kv:o + n_dil + n_q + 2 * n_kv].reshape(d, WIN_KV_HEADS, HEAD_DIM)
    win = jnp.concatenate([q, k, v], axis=2).reshape(d, WIN_KV_HEADS * WIN_SLAB)
    gates = w_in[:, o + n_dil + n_q + 2 * n_kv:]
    return jnp.concatenate([z, ssd, win, dil, gates], axis=1)


def _layer_params(w_in, dt_bias, a_log, d_skip):
    w_main = _main_weight(w_in)
    w_dt = w_in[:, DT_IN_OFF:DT_IN_OFF + DT_COLS]
    w_dt = w_dt.reshape(D_MODEL, 2, SSD_GROUPS, HEADS_PER_GROUP).transpose(0, 2, 1, 3).reshape(D_MODEL, DT_COLS)
    w_dtt = jnp.pad(w_dt, ((0, 0), (0, DT_LANES - DT_COLS))).astype(BF16).T
    b_dt = jnp.pad(_group_major(dt_bias.astype(F32)), (0, DT_LANES - DT_COLS))
    alog = jnp.pad(_group_major(a_log.astype(F32)), (0, DT_LANES - DT_COLS))
    p64 = jnp.repeat(d_skip.astype(F32).reshape(SSD_GROUPS, 1, HEADS_PER_GROUP), SSD_HEAD_DIM, axis=2)
    p64 = jnp.pad(p64, ((0, 0), (0, SUBLANE - 1), (0, 0)))
    return w_main, w_dtt, b_dt.reshape(DT_LANES, 1), alog.reshape(DT_LANES, 1), p64


def kernel(x, g_mix, w_in, conv_w, conv_b, dt_bias, a_log, d_skip, ssd_norm, w_a, w_b, w_c, sink,
           w_out, g_mlp, w_up, w_down, g_final):
    b, seq, _ = x.shape
    m = b * seq
    tm = min(1024, seq)
    ct, st = _rope_tables(seq)
    xf = x.reshape(m, D_MODEL).astype(F32)
    for i in range(DEPTH):
        w_main, w_dtt, b_dtt, alog_col, p64 = _layer_params(w_in[i], dt_bias[i], a_log[i], d_skip[i])
        proj, dtt = _proj(xf, g_mix[i].reshape(1, D_MODEL), w_main, w_dtt, b_dtt, ct, st, seq,
                          tm=tm, tn=1024)
        proj3 = proj.reshape(b, seq, N_MAIN)
        y_ssd = _ssd(proj3, _ssd_slabs(conv_w[i].astype(F32)),
                     _ssd_slabs(conv_b[i].astype(F32).reshape(1, XBC_WIDTH)), dtt, alog_col, p64)
        y_b = _dil_attn(proj3)
        w_b_heads = _rope_lanes(w_b[i].reshape(DIL_HEADS, HEAD_DIM, D_MODEL), 1)
        y_c = _win_attn(proj3, sink[i].astype(F32))
        merged = _merge(y_ssd.reshape(m, SSD_INNER), proj, ssd_norm[i].reshape(1, SSD_INNER),
                        y_b.reshape(m, DIL_WIDTH), y_c.reshape(m, WIN_Q_HEADS * HEAD_DIM),
                        w_a[i].astype(BF16), w_b_heads.reshape(DIL_WIDTH, D_MODEL).astype(BF16),
                        w_c[i].astype(BF16), tm=tm, tn=256)
        xf = _matmul_res(merged, w_out[i].astype(BF16), xf, tm=tm, tn=1024, tk=D_MODEL)
        u = _mlp_up(xf, g_mlp[i].reshape(1, D_MODEL), w_up[i].astype(BF16), tm=tm, tn=1024)
        xf = _matmul_res(u, w_down[i].astype(BF16), xf, tm=tm, tn=1024, tk=2048)
    out = _final_norm(xf, g_final.reshape(1, D_MODEL), tm=min(512, seq))
    return out.reshape(b, seq, D_MODEL).astype(x.dtype)
```

```python
import functools

import jax
import jax.numpy as jnp
from jax import lax
from jax.experimental import pallas as pl
from jax.experimental.pallas import tpu as pltpu

F32 = jnp.float32
BF16 = jnp.bfloat16

D_MODEL = 2048
DEPTH = 2
SSD_HEADS = 32
SSD_HEAD_DIM = 64
SSD_INNER = SSD_HEADS * SSD_HEAD_DIM
SSD_GROUPS = 8
SSD_STATE = 128
SSD_CHUNK = 128
HEADS_PER_GROUP = SSD_HEADS // SSD_GROUPS
GROUP_WIDTH = HEADS_PER_GROUP * SSD_HEAD_DIM
CONV_WIDTH = 5
CONV_PAD = (CONV_WIDTH - 1) // 2
XBC_WIDTH = SSD_INNER + 2 * SSD_GROUPS * SSD_STATE
HEAD_DIM = 128
ROPE_DIM = HEAD_DIM // 4
ROPE_HALF = ROPE_DIM // 2
ROPE_PARTNER = HEAD_DIM // 2
ROPE_THETA = 500000.0
DIL_PATTERNS = ((128, 1), (512, 4), (2048, 16))
DIL_GROUPS = len(DIL_PATTERNS)
DIL_HEADS = 8
DIL_WIDTH = DIL_HEADS * HEAD_DIM
DIL_HALF = 64
TILE_BATCH = 7
SEGMENT_TILE_BATCH = 8
WIN_Q_HEADS = 16
WIN_KV_HEADS = 4
WIN_REP = WIN_Q_HEADS // WIN_KV_HEADS
WIN_HALF = 128
D_FF = 4 * D_MODEL
N_BRANCH = 3
EPS = 1e-6
NEG_INF = -1e30
LOG2E = 1.4426950408889634

SSD_SLAB = GROUP_WIDTH + 2 * SSD_STATE
WIN_SLAB = (WIN_REP + 2) * HEAD_DIM
DIL_SLAB = 3 * DIL_GROUPS * HEAD_DIM
OFF_Z = 0
OFF_SSD = OFF_Z + SSD_INNER
OFF_WIN = OFF_SSD + SSD_GROUPS * SSD_SLAB
OFF_DIL = OFF_WIN + WIN_KV_HEADS * WIN_SLAB
OFF_GATE = OFF_DIL + DIL_HEADS * DIL_SLAB
N_MAIN = OFF_GATE + N_BRANCH * D_MODEL
assert OFF_SSD % SSD_SLAB == 0 and OFF_WIN % WIN_SLAB == 0 and OFF_DIL % DIL_SLAB == 0
DT_COLS = 2 * SSD_HEADS
DT_IN_OFF = SSD_INNER + XBC_WIDTH
DT_LANES = 128

LANE = 128
SUBLANE = 8
VMEM_LIMIT = 56 * 1024 * 1024


def _split3(x):
    hi = x.astype(BF16)
    r1 = x - hi.astype(F32)
    mid = r1.astype(BF16)
    r2 = r1 - mid.astype(F32)
    return hi, mid, r2.astype(BF16)


def _dot(a, b):
    return jnp.dot(a, b, preferred_element_type=F32)


def _dot_nt(a, b):
    return lax.dot_general(a, b, (((1,), (1,)), ((), ())), preferred_element_type=F32)


def _exact_right(x, mat):
    hi, mid, lo = _split3(x)
    return _dot(hi, mat) + _dot(mid, mat) + _dot(lo, mat)


def _silu(x):
    h = 0.5 * x
    return h * jnp.tanh(h) + h


def _softplus(x):
    return jnp.maximum(x, 0.0) + jnp.log1p(jnp.exp(-jnp.abs(x)))


def _norm_rows(x, g):
    ms = jnp.mean(x * x, axis=-1, keepdims=True)
    return x * lax.rsqrt(ms + EPS) * g


def _rope_tile(a, ct, st):
    return a * ct + pltpu.roll(a, ROPE_PARTNER, 1) * st


def _proj_kernel(x_ref, g_ref, w_ref, wdtt_ref, bdtt_ref, ct_ref, st_ref,
                 o_ref, dtt_ref, h_ref, *, tn, tile_kinds):
    j = pl.program_id(1)

    @pl.when(j == 0)
    def _():
        hb = _norm_rows(x_ref[...], g_ref[...]).astype(BF16)
        h_ref[...] = hb
        dtt_ref[...] = _softplus(_dot_nt(wdtt_ref[...], hb) + bdtt_ref[...])

    def tile_body(kind):
        acc = _dot(h_ref[...], w_ref[...])
        if 'r' in kind:
            ct, st = ct_ref[...], st_ref[...]
        for c, k in enumerate(kind):
            sl = slice(c * HEAD_DIM, (c + 1) * HEAD_DIM)
            a = acc[:, sl]
            if k == 'r':
                a = _rope_tile(a, ct, st)
            elif k == 'g':
                a = 0.5 * jnp.tanh(0.5 * a) + 0.5
            o_ref[:, sl] = a.astype(o_ref.dtype)

    for kind in sorted(set(tile_kinds)):
        idx = [t for t, k in enumerate(tile_kinds) if k == kind]
        runs, start = [], idx[0]
        for a, b in zip(idx, idx[1:] + [None]):
            if b != a + 1:
                runs.append((start, a))
                start = b
        cond = None
        for lo, hi in runs:
            c = jnp.logical_and(j >= lo, j <= hi)
            cond = c if cond is None else jnp.logical_or(cond, c)
        pl.when(cond)(functools.partial(tile_body, kind))


def _proj_tile_kinds(tn):
    def head_kind(col):
        if col >= OFF_GATE:
            return 'g'
        if col >= OFF_DIL:
            return 'r'
        if col >= OFF_WIN:
            return 'r' if (col - OFF_WIN) % WIN_SLAB < (WIN_REP + 1) * HEAD_DIM else 'p'
        return 'p'
    return tuple(''.join(head_kind(t * tn + c * HEAD_DIM) for c in range(tn // HEAD_DIM))
                 for t in range(N_MAIN // tn))


def _proj(x2d, g, w_main, w_dtt, b_dtt, ct, st, seq, *, tm, tn):
    m = x2d.shape[0]
    assert seq % tm == 0 and m % tm == 0 and N_MAIN % tn == 0 and tn % HEAD_DIM == 0
    seq_tiles = seq // tm
    kern = functools.partial(_proj_kernel, tn=tn, tile_kinds=_proj_tile_kinds(tn))
    return pl.pallas_call(
        kern,
        grid=(m // tm, N_MAIN // tn),
        in_specs=[
            pl.BlockSpec((tm, D_MODEL), lambda i, j: (i, 0)),
            pl.BlockSpec((1, D_MODEL), lambda i, j: (0, 0)),
            pl.BlockSpec((D_MODEL, tn), lambda i, j: (0, j)),
            pl.BlockSpec((DT_LANES, D_MODEL), lambda i, j: (0, 0)),
            pl.BlockSpec((DT_LANES, 1), lambda i, j: (0, 0)),
            pl.BlockSpec((tm, HEAD_DIM), lambda i, j: (i % seq_tiles, 0)),
            pl.BlockSpec((tm, HEAD_DIM), lambda i, j: (i % seq_tiles, 0)),
        ],
        out_specs=[
            pl.BlockSpec((tm, tn), lambda i, j: (i, j)),
            pl.BlockSpec((DT_LANES, tm), lambda i, j: (0, i)),
        ],
        out_shape=[
            jax.ShapeDtypeStruct((m, N_MAIN), BF16),
            jax.ShapeDtypeStruct((DT_LANES, m), F32),
        ],
        scratch_shapes=[pltpu.VMEM((tm, D_MODEL), BF16)],
        compiler_params=pltpu.CompilerParams(
            dimension_semantics=("parallel", "arbitrary"), vmem_limit_bytes=VMEM_LIMIT),
        name="proj",
    )(x2d, g, w_main, w_dtt, b_dtt, ct, st)


def _up_kernel(x_ref, g_ref, w_ref, o_ref, h_ref):
    @pl.when(pl.program_id(1) == 0)
    def _():
        h_ref[...] = _norm_rows(x_ref[...], g_ref[...]).astype(BF16)

    acc = jnp.maximum(_dot(h_ref[...], w_ref[...]), 0.0)
    o_ref[...] = (acc * acc).astype(o_ref.dtype)


def _mlp_up(x2d, g, w, *, tm, tn):
    m, n = x2d.shape[0], w.shape[1]
    return pl.pallas_call(
        _up_kernel,
        grid=(m // tm, n // tn),
        in_specs=[
            pl.BlockSpec((tm, D_MODEL), lambda i, j: (i, 0)),
            pl.BlockSpec((1, D_MODEL), lambda i, j: (0, 0)),
            pl.BlockSpec((D_MODEL, tn), lambda i, j: (0, j)),
        ],
        out_specs=pl.BlockSpec((tm, tn), lambda i, j: (i, j)),
        out_shape=jax.ShapeDtypeStruct((m, n), BF16),
        scratch_shapes=[pltpu.VMEM((tm, D_MODEL), BF16)],
        compiler_params=pltpu.CompilerParams(
            dimension_semantics=("parallel", "arbitrary"), vmem_limit_bytes=VMEM_LIMIT),
        name="mlp_up",
    )(x2d, g, w)


def _matmul_res_kernel(a_ref, w_ref, r_ref, o_ref, acc_ref):
    k = pl.program_id(2)

    @pl.when(k == 0)
    def _():
        acc_ref[...] = jnp.zeros_like(acc_ref)

    acc_ref[...] += _dot(a_ref[...], w_ref[...])

    @pl.when(k == pl.num_programs(2) - 1)
    def _():
        o_ref[...] = r_ref[...] + acc_ref[...]


def _matmul_res(a, w, res, *, tm, tn, tk):
    m, kdim = a.shape
    n = w.shape[1]
    return pl.pallas_call(
        _matmul_res_kernel,
        grid=(m // tm, n // tn, kdim // tk),
        in_specs=[
            pl.BlockSpec((tm, tk), lambda i, j, k: (i, k)),
            pl.BlockSpec((tk, tn), lambda i, j, k: (k, j)),
            pl.BlockSpec((tm, tn), lambda i, j, k: (i, j)),
        ],
        out_specs=pl.BlockSpec((tm, tn), lambda i, j, k: (i, j)),
        out_shape=jax.ShapeDtypeStruct((m, n), F32),
        scratch_shapes=[pltpu.VMEM((tm, tn), F32)],
        compiler_params=pltpu.CompilerParams(
            dimension_semantics=("parallel", "parallel", "arbitrary"), vmem_limit_bytes=VMEM_LIMIT),
        name="matmul_res",
    )(a, w, res)


def _gated_norm(ys, z, nw):
    z = z.astype(F32)
    y = ys.astype(F32) * _silu(z)
    return _norm_rows(y, nw).astype(BF16)


def _merge_kernel(ys_ref, z_ref, nw_ref, yb_ref, yc_ref, g0_ref, g1_ref, g2_ref,
                  wa_ref, wb_ref, wc_ref, o_ref, ya_ref):
    @pl.when(pl.program_id(1) == 0)
    def _():
        ya_ref[...] = _gated_norm(ys_ref[...], z_ref[...], nw_ref[...])

    acc = g0_ref[...].astype(F32) * _dot(ya_ref[...], wa_ref[...])
    acc += g1_ref[...].astype(F32) * _dot(yb_ref[...], wb_ref[...])
    acc += g2_ref[...].astype(F32) * _dot(yc_ref[...], wc_ref[...])
    o_ref[...] = acc.astype(o_ref.dtype)


def _merge(y_ssd, proj, norm_w, y_b, y_c, w_a, w_b, w_c, *, tm, tn):
    m = y_ssd.shape[0]
    gate_blk = OFF_GATE // tn
    per_gate = D_MODEL // tn
    return pl.pallas_call(
        _merge_kernel,
        grid=(m // tm, D_MODEL // tn),
        in_specs=[
            pl.BlockSpec((tm, SSD_INNER), lambda i, j: (i, 0)),
            pl.BlockSpec((tm, SSD_INNER), lambda i, j: (i, OFF_Z // SSD_INNER)),
            pl.BlockSpec((1, SSD_INNER), lambda i, j: (0, 0)),
            pl.BlockSpec((tm, DIL_WIDTH), lambda i, j: (i, 0)),
            pl.BlockSpec((tm, WIN_Q_HEADS * HEAD_DIM), lambda i, j: (i, 0)),
            pl.BlockSpec((tm, tn), lambda i, j: (i, gate_blk + j)),
            pl.BlockSpec((tm, tn), lambda i, j: (i, gate_blk + per_gate + j)),
            pl.BlockSpec((tm, tn), lambda i, j: (i, gate_blk + 2 * per_gate + j)),
            pl.BlockSpec((SSD_INNER, tn), lambda i, j: (0, j)),
            pl.BlockSpec((DIL_WIDTH, tn), lambda i, j: (0, j)),
            pl.BlockSpec((WIN_Q_HEADS * HEAD_DIM, tn), lambda i, j: (0, j)),
        ],
        out_specs=pl.BlockSpec((tm, tn), lambda i, j: (i, j)),
        out_shape=jax.ShapeDtypeStruct((m, D_MODEL), BF16),
        scratch_shapes=[pltpu.VMEM((tm, SSD_INNER), BF16)],
        compiler_params=pltpu.CompilerParams(
            dimension_semantics=("parallel", "arbitrary"), vmem_limit_bytes=VMEM_LIMIT),
        name="merge",
    )(y_ssd, proj, norm_w, y_b, y_c, proj, proj, proj, w_a, w_b, w_c)


def _final_norm_kernel(x_ref, g_ref, o_ref):
    o_ref[...] = _norm_rows(x_ref[...], g_ref[...])


def _final_norm(x2d, g, *, tm):
    m = x2d.shape[0]
    return pl.pallas_call(
        _final_norm_kernel,
        grid=(m // tm,),
        in_specs=[pl.BlockSpec((tm, D_MODEL), lambda i: (i, 0)),
                  pl.BlockSpec((1, D_MODEL), lambda i: (0, 0))],
        out_specs=pl.BlockSpec((tm, D_MODEL), lambda i: (i, 0)),
        out_shape=jax.ShapeDtypeStruct((m, D_MODEL), F32),
        compiler_params=pltpu.CompilerParams(dimension_semantics=("parallel",)),
        name="final_norm",
    )(x2d, g)


def _ssd_kernel(xbc_ref, cw_ref, cb_ref, dtt_ref, alt_ref, p64_ref,
                y_ref,
                pad_s, xs_s, bm_s, cm_s, bmt_s, dtr_s, y_s, st_s, tri_s, u_s, e_s, scl_s, *, seq):
    t = SSD_CHUNK
    n_chunks = seq // t
    for c in range(n_chunks):
        dtr_s[c] = dtt_ref[:, c * t:(c + 1) * t]

    halo = SUBLANE
    for lane_tile in range(SSD_SLAB // LANE):
        lanes = slice(lane_tile * LANE, (lane_tile + 1) * LANE)
        pad_s[lane_tile, 0:halo, :] = jnp.zeros((halo, LANE), F32)
        pad_s[lane_tile, halo + seq:2 * halo + seq, :] = jnp.zeros((halo, LANE), F32)
        pad_s[lane_tile, halo:halo + seq, :] = xbc_ref[0, :, lanes].astype(F32)

    def conv_silu(r0, lane_tile):
        lanes = slice(lane_tile * LANE, (lane_tile + 1) * LANE)
        acc = cb_ref[:, lanes]
        for k in range(CONV_WIDTH):
            lo = halo - CONV_PAD + k + r0
            acc = acc + cw_ref[k:k + 1, lanes] * pad_s[lane_tile, lo:lo + t, :]
        return _silu(acc)

    x_tiles = GROUP_WIDTH // LANE
    for c in range(n_chunks):
        r0 = c * t
        rows = slice(r0, r0 + t)
        for lane_tile in range(x_tiles):
            lanes = slice(lane_tile * LANE, (lane_tile + 1) * LANE)
            xv = conv_silu(r0, lane_tile)
            xs_s[rows, lanes] = xv.astype(BF16)
            y_s[rows, lanes] = xv * p64_ref[0, 0:1, lanes]
        bv = conv_silu(r0, x_tiles)
        bm_s[rows, :] = bv.astype(BF16)
        bmt_s[c] = bv.T.astype(BF16)
        cm_s[rows, :] = conv_silu(r0, x_tiles + 1).astype(BF16)

    ri = lax.broadcasted_iota(jnp.int32, (t, t), 0)
    ci = lax.broadcasted_iota(jnp.int32, (t, t), 1)
    lower, upper = ri >= ci, ri <= ci
    tri_s[:, 0:t] = upper.astype(BF16)
    tri_s[:, t:2 * t] = lower.astype(BF16)
    st_s[...] = jnp.zeros_like(st_s)

    nh = HEADS_PER_GROUP
    a_rows = -jnp.exp(alt_ref[...])
    head_of_lane = lax.broadcasted_iota(jnp.int32, (1, GROUP_WIDTH), 1) // SSD_HEAD_DIM
    low_half = lax.broadcasted_iota(jnp.int32, (1, LANE), 1) < SSD_HEAD_DIM

    def lanes_from_row(row):
        return jnp.broadcast_to(row, (t, t)).T

    def expand64(cols):
        return jnp.concatenate([jnp.where(low_half, cols[0], cols[1]),
                                jnp.where(low_half, cols[2], cols[3])], axis=1)

    def prepare(chunks):
        loaded = []
        for c in chunks:
            rows = pl.ds(pl.multiple_of(c * t, t), t)
            loaded.append((c, rows, dtr_s[c], xs_s[rows, :], cm_s[rows, :], bm_s[rows, :], bmt_s[c]))
        cums = [_exact_right(dt_all, tri_s[...]) for _, _, dt_all, _, _, _, _ in loaded]
        cbs = [_dot_nt(cm_c, bm_c) for _, _, _, _, cm_c, bm_c, _ in loaded]
        staged = []
        for (c, rows, dt_all, xs_c, _, _, bmt_c), cum, cb in zip(loaded, cums, cbs):
            cs = (cum[0:nh, 0:t] * a_rows[0:nh], cum[nh:2 * nh, t:2 * t] * a_rows[nh:2 * nh])
            dts = (dt_all[0:nh, :], dt_all[nh:2 * nh, :])
            tots = (cs[0][:, t - 1:t], cs[1][:, 0:1])
            w_rows = [dts[d] * jnp.exp(tots[d] - cs[d]) for d in range(2)]
            grow = [jnp.exp(tots[d]) for d in range(2)]
            bmt_f = bmt_c.astype(F32)
            l_parts, bw, seen, scale = [], ([], []), ([], []), [0.0, 0.0]
            for h in range(nh):
                mix = jnp.zeros((t, t), F32)
                for d, valid in ((0, lower), (1, upper)):
                    row = cs[d][h:h + 1, :]
                    col = lanes_from_row(row)
                    mix = mix + jnp.exp(jnp.where(valid, col - row, NEG_INF)) * dts[d][h:h + 1, :]
                    seen[d].append(jnp.exp(col))
                    bw[d].append((bmt_f * w_rows[d][h:h + 1, :]).astype(BF16))
                    scale[d] = scale[d] + jnp.where(head_of_lane == h, grow[d][h:h + 1, :], 0.0)
                l_parts.append((cb * mix).astype(BF16))
            lhs = jnp.concatenate([jnp.concatenate(l_parts, axis=1), jnp.concatenate(bw[0], axis=1),
                                   jnp.concatenate(bw[1], axis=1)], axis=0)
            xs_stack = jnp.concatenate(
                [jnp.where(head_of_lane == h, xs_c, jnp.zeros_like(xs_c)) for h in range(nh)], axis=0)
            staged.append((c, rows, lhs, xs_stack, seen, scale))
        outs = [_dot(lhs, xs_stack) for _, _, lhs, xs_stack, _, _ in staged]
        for (c, rows, _, _, seen, scale), out in zip(staged, outs):
            y_s[rows, :] += out[0:t, :]
            for d in range(2):
                u_s[d, c] = out[(d + 1) * t:(d + 2) * t, :]
                e_s[d, c] = expand64(seen[d])
                scl_s[d, c] = jnp.broadcast_to(scale[d], (SUBLANE, GROUP_WIDTH))

    batch = 16
    def prep_body(i, carry):
        prepare([i * batch + u for u in range(batch)])
        return carry

    lax.fori_loop(0, n_chunks // batch, prep_body, 0)

    def scan_body(i, carry):
        steps = []
        for d, c in ((0, i), (1, n_chunks - 1 - i)):
            rows = pl.ds(pl.multiple_of(c * t, t), t)
            steps.append((d, c, rows, st_s[d], cm_s[rows, :]))
        reads = [_dot(cm_c, st.astype(BF16)) for _, _, _, st, cm_c in steps]
        for (d, c, rows, st, _), read in zip(steps, reads):
            y_s[rows, :] += read * e_s[d, c]
            st_s[d] = st * scl_s[d, c][0:1, :] + u_s[d, c]
        return carry

    lax.fori_loop(0, n_chunks, scan_body, 0, unroll=8)
    y_ref[0] = y_s[...].astype(y_ref.dtype)


def _ssd(proj3, conv_w, conv_b, dtt, alog_col, p64):
    b, seq, _ = proj3.shape
    gw, ns = GROUP_WIDTH, SSD_STATE
    slab0 = OFF_SSD // SSD_SLAB
    kern = functools.partial(_ssd_kernel, seq=seq)
    return pl.pallas_call(
        kern,
        grid=(b, SSD_GROUPS),
        in_specs=[
            pl.BlockSpec((1, seq, SSD_SLAB), lambda i, g: (i, 0, slab0 + g)),
            pl.BlockSpec((CONV_WIDTH, SSD_SLAB), lambda i, g: (0, g)),
            pl.BlockSpec((1, SSD_SLAB), lambda i, g: (0, g)),
            pl.BlockSpec((2 * HEADS_PER_GROUP, seq), lambda i, g: (g, i)),
            pl.BlockSpec((2 * HEADS_PER_GROUP, 1), lambda i, g: (g, 0)),
            pl.BlockSpec((1, SUBLANE, gw), lambda i, g: (g, 0, 0)),
        ],
        out_specs=pl.BlockSpec((1, seq, gw), lambda i, g: (i, 0, g)),
        out_shape=jax.ShapeDtypeStruct((b, seq, SSD_INNER), BF16),
        scratch_shapes=[
            pltpu.VMEM((SSD_SLAB // LANE, seq + 2 * SUBLANE, LANE), F32),
            pltpu.VMEM((seq, gw), BF16),
            pltpu.VMEM((seq, ns), BF16),
            pltpu.VMEM((seq, ns), BF16),
            pltpu.VMEM((seq // SSD_CHUNK, ns, SSD_CHUNK), BF16),
            pltpu.VMEM((seq // SSD_CHUNK, 2 * HEADS_PER_GROUP, SSD_CHUNK), F32),
            pltpu.VMEM((seq, gw), F32),
            pltpu.VMEM((2, ns, gw), F32),
            pltpu.VMEM((SSD_CHUNK, 2 * SSD_CHUNK), BF16),
            pltpu.VMEM((2, seq // SSD_CHUNK, ns, gw), F32),
            pltpu.VMEM((2, seq // SSD_CHUNK, SSD_CHUNK, gw), F32),
            pltpu.VMEM((2, seq // SSD_CHUNK, SUBLANE, gw), F32),
        ],
        compiler_params=pltpu.CompilerParams(
            dimension_semantics=("parallel", "arbitrary"), vmem_limit_bytes=VMEM_LIMIT),
        name="ssd",
    )(proj3, conv_w, conv_b, dtt, alog_col, p64)


def _win_kernel(sink_ref, qkv_ref, o_ref, bias_s, vext_s, *, seq):
    q_cols = slice(0, WIN_REP * HEAD_DIM)
    k_cols = slice(WIN_REP * HEAD_DIM, (WIN_REP + 1) * HEAD_DIM)
    v_cols = slice((WIN_REP + 1) * HEAD_DIM, (WIN_REP + 2) * HEAD_DIM)
    blk = WIN_HALF
    kwin = 3 * blk
    n_blk = seq // blk
    kv = pl.program_id(1)
    scale2 = (HEAD_DIM ** -0.5) * LOG2E

    ri = lax.broadcasted_iota(jnp.int32, (blk, kwin), 0)
    ci = lax.broadcasted_iota(jnp.int32, (blk, kwin), 1)
    for w, shift in enumerate((0, -blk, -2 * blk)):
        bias_s[w] = jnp.where(jnp.abs(ci + shift - ri) <= WIN_HALF, 0.0, NEG_INF).astype(F32)
    vext_s[:, 0:HEAD_DIM] = qkv_ref[0, :, v_cols]
    vext_s[:, HEAD_DIM:2 * HEAD_DIM] = jnp.ones((seq, HEAD_DIM), BF16)
    sinks = [sink_ref[kv * WIN_REP + r] * LOG2E for r in range(WIN_REP)]

    def blocks(specs):
        loaded = []
        for qi, which in specs:
            q0 = qi * blk if isinstance(qi, int) else pl.multiple_of(qi * blk, blk)
            k0 = (0, q0 - blk, seq - kwin)[which]
            if not isinstance(k0, int):
                k0 = pl.multiple_of(k0, blk)
            loaded.append((q0, qkv_ref[0, pl.ds(q0, blk), q_cols], qkv_ref[0, pl.ds(k0, kwin), k_cols],
                           vext_s[pl.ds(k0, kwin), :], bias_s[which]))
        scores = [[_dot_nt(qt[:, r * HEAD_DIM:(r + 1) * HEAD_DIM], kt) for r in range(WIN_REP)]
                  for _, qt, kt, _, _ in loaded]
        maxes, probs = [], []
        for heads, (_, _, _, _, bias) in zip(scores, loaded):
            for r, s in enumerate(heads):
                s = s * scale2 + bias
                m = jnp.maximum(jnp.max(s, axis=-1, keepdims=True), sinks[r])
                maxes.append(m)
                probs.append(jnp.exp2(s - m).astype(BF16))
        accs = [_dot(probs[i * WIN_REP + r], vt)
                for i, (_, _, _, vt, _) in enumerate(loaded) for r in range(WIN_REP)]
        for i, (q0, _, _, _, _) in enumerate(loaded):
            for r in range(WIN_REP):
                acc, m = accs[i * WIN_REP + r], maxes[i * WIN_REP + r]
                den = acc[:, HEAD_DIM:] + jnp.exp2(sinks[r] - m)
                o_ref[0, pl.ds(q0, blk), r * HEAD_DIM:(r + 1) * HEAD_DIM] = (
                    acc[:, :HEAD_DIM] / den).astype(o_ref.dtype)

    blocks([(0, 0), (n_blk - 1, 2)])
    pairs = (n_blk - 2) // 2

    def body(pi, carry):
        blocks([(1 + 2 * pi, 1), (2 + 2 * pi, 1)])
        return carry

    lax.fori_loop(0, pairs, body, 0)
    if (n_blk - 2) % 2:
        blocks([(n_blk - 2, 1)])


def _win_attn(proj3, sink):
    b, seq, _ = proj3.shape
    assert seq >= 3 * WIN_HALF and seq % WIN_HALF == 0
    qw = WIN_REP * HEAD_DIM
    kern = functools.partial(_win_kernel, seq=seq)
    return pl.pallas_call(
        kern,
        grid=(b, WIN_KV_HEADS),
        in_specs=[
            pl.BlockSpec(memory_space=pltpu.SMEM),
            pl.BlockSpec((1, seq, WIN_SLAB), lambda i, g: (i, 0, OFF_WIN // WIN_SLAB + g)),
        ],
        out_specs=pl.BlockSpec((1, seq, qw), lambda i, g: (i, 0, g)),
        out_shape=jax.ShapeDtypeStruct((b, seq, WIN_Q_HEADS * HEAD_DIM), BF16),
        scratch_shapes=[pltpu.VMEM((3, WIN_HALF, 3 * WIN_HALF), F32),
                        pltpu.VMEM((seq, 2 * HEAD_DIM), BF16)],
        compiler_params=pltpu.CompilerParams(
            dimension_semantics=("parallel", "parallel"), vmem_limit_bytes=VMEM_LIMIT),
        name="win_attn",
    )(sink, proj3)


def _dil_kernel(qkv_ref, o_ref, stage_s, qd_s, kd_s, vext_s, od_s, ld_s, og_s, lg_s, bias_s, bias1_s,
                *, seq):
    def cols(part, gi):
        c0 = (part * DIL_GROUPS + gi) * HEAD_DIM
        return slice(c0, c0 + HEAD_DIM)

    tq = 2 * DIL_HALF
    kw = tq + 2 * DIL_HALF
    scale2 = (HEAD_DIM ** -0.5) * LOG2E

    ri = lax.broadcasted_iota(jnp.int32, (tq, kw), 0)
    ci = lax.broadcasted_iota(jnp.int32, (tq, kw), 1)
    for w, shift in enumerate((0, -DIL_HALF, -2 * DIL_HALF)):
        bias_s[w] = jnp.where(jnp.abs(ci + shift - ri) <= DIL_HALF, 0.0, NEG_INF).astype(F32)
    ri1 = lax.broadcasted_iota(jnp.int32, (tq, tq), 0)
    ci1 = lax.broadcasted_iota(jnp.int32, (tq, tq), 1)
    bias1_s[...] = jnp.where(jnp.abs(ci1 - ri1) <= DIL_HALF, 0.0, NEG_INF).astype(F32)
    vext_s[:, HEAD_DIM:2 * HEAD_DIM] = jnp.ones((seq, HEAD_DIM), BF16)

    def deinterleave(src_cols, dst_ref, dil):
        n_sub = seq // dil
        stage_s[...] = qkv_ref[0, :, src_cols].astype(F32)
        for r in range(dil):
            dst_ref[r * n_sub:(r + 1) * n_sub, 0:HEAD_DIM] = (
                stage_s[pl.ds(r, n_sub, stride=dil), :].astype(BF16))

    def aligned(x, mult):
        return x if isinstance(x, int) else pl.multiple_of(x, mult)

    def run_group(q_at, k_at, n_sub):
        def tiles(specs):
            loaded = []
            for q0, k0, width, bias in specs:
                q0, k0 = aligned(q0, tq), aligned(k0, DIL_HALF)
                loaded.append((q0, q_at(pl.ds(q0, tq)), k_at(pl.ds(k0, width)),
                               vext_s[pl.ds(k0, width), :], bias))
            scores = [_dot_nt(qt, kt) for _, qt, kt, _, _ in loaded]
            maxes, probs = [], []
            for s, (_, _, _, _, bias) in zip(scores, loaded):
                s = s * scale2 + bias
                m = jnp.max(s, axis=-1, keepdims=True)
                maxes.append(m)
                probs.append(jnp.exp2(s - m).astype(BF16))
            accs = [_dot(p, vt) for p, (_, _, _, vt, _) in zip(probs, loaded)]
            for acc, m, (q0, _, _, _, _) in zip(accs, maxes, loaded):
                den = acc[:, HEAD_DIM:]
                od_s[pl.ds(q0, tq), :] = acc[:, :HEAD_DIM] / den
                ld_s[pl.ds(q0, tq), :] = m + jnp.log2(den)

        def first(seg0):
            return (seg0, seg0, kw, bias_s[0])

        def inner(q0):
            return (q0, q0 - DIL_HALF, kw, bias_s[1])

        def last(seg0):
            return (seg0 + n_sub - tq, seg0 + n_sub - kw, kw, bias_s[2])

        tiles_per_seg, n_seg, nb = n_sub // tq, seq // n_sub, TILE_BATCH
        if tiles_per_seg == 1:
            wide = SEGMENT_TILE_BATCH
            def body1(bi, carry):
                tiles([((bi * wide + u) * tq, (bi * wide + u) * tq, tq, bias1_s[...]) for u in range(wide)])
                return carry
            lax.fori_loop(0, seq // (tq * wide), body1, 0)
        elif SEGMENT_TILE_BATCH % tiles_per_seg == 0:
            segs = SEGMENT_TILE_BATCH // tiles_per_seg
            def seg_body(si, carry):
                specs = []
                for u in range(segs):
                    seg0 = (si * segs + u) * n_sub
                    specs += ([first(seg0)] + [inner(seg0 + ti * tq) for ti in range(1, tiles_per_seg - 1)]
                              + [last(seg0)])
                tiles(specs)
                return carry
            lax.fori_loop(0, n_seg // segs, seg_body, 0)
        else:
            assert n_seg == 1
            loops = (tiles_per_seg - 2) // nb
            tiles([first(0)])
            def inner_body(bi, carry):
                tiles([inner((1 + bi * nb + u) * tq) for u in range(nb)])
                return carry
            lax.fori_loop(0, loops, inner_body, 0)
            tiles([inner(ti * tq) for ti in range(1 + loops * nb, tiles_per_seg - 1)] + [last(0)])

    for gi, (_, dil) in enumerate(DIL_PATTERNS):
        n_sub = seq // dil
        if dil == 1:
            vext_s[:, 0:HEAD_DIM] = qkv_ref[0, :, cols(2, gi)]
            run_group(lambda sl: qkv_ref[0, sl, cols(0, gi)], lambda sl: qkv_ref[0, sl, cols(1, gi)], n_sub)
            og_s[gi] = od_s[...]
            lg_s[gi] = ld_s[...]
        else:
            deinterleave(cols(0, gi), qd_s, dil)
            deinterleave(cols(1, gi), kd_s, dil)
            deinterleave(cols(2, gi), vext_s, dil)
            run_group(lambda sl: qd_s[sl, :], lambda sl: kd_s[sl, :], n_sub)
            for r in range(dil):
                og_s[gi, pl.ds(r, n_sub, stride=dil), :] = od_s[r * n_sub:(r + 1) * n_sub, :]
                lg_s[gi, pl.ds(r, n_sub, stride=dil), :] = ld_s[r * n_sub:(r + 1) * n_sub, :]

    m = jnp.maximum(jnp.maximum(lg_s[0], lg_s[1]), lg_s[2])
    num = jnp.zeros((seq, HEAD_DIM), F32)
    den = jnp.zeros((seq, HEAD_DIM), F32)
    for gi in range(DIL_GROUPS):
        w = jnp.exp2(lg_s[gi] - m)
        num = num + w * og_s[gi]
        den = den + w
    o_ref[0] = (num / den).astype(o_ref.dtype)


def _dil_attn(proj3):
    b, seq, _ = proj3.shape
    tq = 2 * DIL_HALF
    for _, dil in DIL_PATTERNS:
        assert (seq // dil) % tq == 0
    kern = functools.partial(_dil_kernel, seq=seq)
    return pl.pallas_call(
        kern,
        grid=(b, DIL_HEADS),
        in_specs=[pl.BlockSpec((1, seq, DIL_SLAB), lambda i, h: (i, 0, OFF_DIL // DIL_SLAB + h))],
        out_specs=pl.BlockSpec((1, seq, HEAD_DIM), lambda i, h: (i, 0, h)),
        out_shape=jax.ShapeDtypeStruct((b, seq, DIL_WIDTH), BF16),
        scratch_shapes=[
            pltpu.VMEM((seq, HEAD_DIM), F32),
            pltpu.VMEM((seq, HEAD_DIM), BF16),
            pltpu.VMEM((seq, HEAD_DIM), BF16),
            pltpu.VMEM((seq, 2 * HEAD_DIM), BF16),
            pltpu.VMEM((seq, HEAD_DIM), F32),
            pltpu.VMEM((seq, HEAD_DIM), F32),
            pltpu.VMEM((DIL_GROUPS, seq, HEAD_DIM), F32),
            pltpu.VMEM((DIL_GROUPS, seq, HEAD_DIM), F32),
            pltpu.VMEM((3, tq, tq + 2 * DIL_HALF), F32),
            pltpu.VMEM((tq, tq), F32),
        ],
        compiler_params=pltpu.CompilerParams(
            dimension_semantics=("parallel", "parallel"), vmem_limit_bytes=VMEM_LIMIT),
        name="dil_attn",
    )(proj3)


def _rope_tables(seq):
    inv = ROPE_THETA ** (-jnp.arange(0, ROPE_DIM, 2, dtype=F32) / ROPE_DIM)
    ang = jnp.arange(seq, dtype=F32)[:, None] * inv[None, :]
    cos, sin = jnp.cos(ang), jnp.sin(ang)
    gap = jnp.zeros((seq, ROPE_PARTNER - ROPE_HALF), F32)
    ct = jnp.concatenate([cos, gap + 1.0, cos, gap + 1.0], axis=1)
    st = jnp.concatenate([-sin, gap, sin, gap], axis=1)
    return ct, st


def _group_major(v):
    return v.reshape(2, SSD_GROUPS, HEADS_PER_GROUP).transpose(1, 0, 2).reshape(DT_COLS)


def _ssd_slabs(a):
    rows = a.shape[0]
    x = a[:, :SSD_INNER].reshape(rows, SSD_GROUPS, GROUP_WIDTH)
    bm = a[:, SSD_INNER:SSD_INNER + SSD_GROUPS * SSD_STATE].reshape(rows, SSD_GROUPS, SSD_STATE)
    cm = a[:, SSD_INNER + SSD_GROUPS * SSD_STATE:].reshape(rows, SSD_GROUPS, SSD_STATE)
    return jnp.concatenate([x, bm, cm], axis=2).reshape(rows, XBC_WIDTH)


def _rope_lanes(a, axis):
    idx = lambda lo, hi: lax.slice_in_dim(a, lo, hi, axis=axis)
    return jnp.concatenate([idx(0, ROPE_HALF), idx(ROPE_DIM, ROPE_DIM + ROPE_PARTNER - ROPE_HALF),
                            idx(ROPE_HALF, ROPE_DIM), idx(ROPE_DIM + ROPE_PARTNER - ROPE_HALF, HEAD_DIM)],
                           axis=axis)


def _main_weight(w_in):
    d = w_in.shape[0]
    o = DT_IN_OFF + DT_COLS
    n_dil = 3 * DIL_GROUPS * DIL_WIDTH
    n_q, n_kv = WIN_Q_HEADS * HEAD_DIM, WIN_KV_HEADS * HEAD_DIM
    w_in = w_in.astype(BF16)
    lane_perm = _rope_lanes(jnp.eye(HEAD_DIM, dtype=BF16), 1)

    def rope_lanes(heads):
        return jnp.einsum('dhk,kj->dhj', heads, lane_perm, preferred_element_type=BF16)

    z = w_in[:, :SSD_INNER]
    ssd = _ssd_slabs(w_in[:, SSD_INNER:DT_IN_OFF])
    dil = rope_lanes(w_in[:, o:o + n_dil].reshape(d, 3 * DIL_GROUPS * DIL_HEADS, HEAD_DIM))
    dil = dil.reshape(d, 3 * DIL_GROUPS, DIL_HEADS, HEAD_DIM)
    dil = dil.transpose(0, 2, 1, 3).reshape(d, n_dil)
    qk = rope_lanes(w_in[:, o + n_dil:o + n_dil + n_q + n_kv].reshape(d, WIN_Q_HEADS + WIN_KV_HEADS, HEAD_DIM))
    q = qk[:, :WIN_Q_HEADS].reshape(d, WIN_KV_HEADS, WIN_REP * HEAD_DIM)
    k = qk[:, WIN_Q_HEADS:]
    v = w_in[:, o + n_dil + n_q + n_kv:o + n_dil + n_q + 2 * n_kv].reshape(d, WIN_KV_HEADS, HEAD_DIM)
    win = jnp.concatenate([q, k, v], axis=2).reshape(d, WIN_KV_HEADS * WIN_SLAB)
    gates = w_in[:, o + n_dil + n_q + 2 * n_kv:]
    return jnp.concatenate([z, ssd, win, dil, gates], axis=1)


def _layer_params(w_in, dt_bias, a_log, d_skip):
    w_main = _main_weight(w_in)
    w_dt = w_in[:, DT_IN_OFF:DT_IN_OFF + DT_COLS]
    w_dt = w_dt.reshape(D_MODEL, 2, SSD_GROUPS, HEADS_PER_GROUP).transpose(0, 2, 1, 3).reshape(D_MODEL, DT_COLS)
    w_dtt = jnp.pad(w_dt, ((0, 0), (0, DT_LANES - DT_COLS))).astype(BF16).T
    b_dt = jnp.pad(_group_major(dt_bias.astype(F32)), (0, DT_LANES - DT_COLS))
    alog = jnp.pad(_group_major(a_log.astype(F32)), (0, DT_LANES - DT_COLS))
    p64 = jnp.repeat(d_skip.astype(F32).reshape(SSD_GROUPS, 1, HEADS_PER_GROUP), SSD_HEAD_DIM, axis=2)
    p64 = jnp.pad(p64, ((0, 0), (0, SUBLANE - 1), (0, 0)))
    return w_main, w_dtt, b_dt.reshape(DT_LANES, 1), alog.reshape(DT_LANES, 1), p64


def kernel(x, g_mix, w_in, conv_w, conv_b, dt_bias, a_log, d_skip, ssd_norm, w_a, w_b, w_c, sink,
           w_out, g_mlp, w_up, w_down, g_final):
    b, seq, _ = x.shape
    m = b * seq
    tm = min(1024, seq)
    ct, st = _rope_tables(seq)
    xf = x.reshape(m, D_MODEL).astype(F32)
    for i in range(DEPTH):
        w_main, w_dtt, b_dtt, alog_col, p64 = _layer_params(w_in[i], dt_bias[i], a_log[i], d_skip[i])
        proj, dtt = _proj(xf, g_mix[i].reshape(1, D_MODEL), w_main, w_dtt, b_dtt, ct, st, seq,
                          tm=tm, tn=1024)
        proj3 = proj.reshape(b, seq, N_MAIN)
        y_ssd = _ssd(proj3, _ssd_slabs(conv_w[i].astype(F32)),
                     _ssd_slabs(conv_b[i].astype(F32).reshape(1, XBC_WIDTH)), dtt, alog_col, p64)
        y_b = _dil_attn(proj3)
        w_b_heads = _rope_lanes(w_b[i].reshape(DIL_HEADS, HEAD_DIM, D_MODEL), 1)
        y_c = _win_attn(proj3, sink[i].astype(F32))
        merged = _merge(y_ssd.reshape(m, SSD_INNER), proj, ssd_norm[i].reshape(1, SSD_INNER),
                        y_b.reshape(m, DIL_WIDTH), y_c.reshape(m, WIN_Q_HEADS * HEAD_DIM),
                        w_a[i].astype(BF16), w_b_heads.reshape(DIL_WIDTH, D_MODEL).astype(BF16),
                        w_c[i].astype(BF16), tm=tm, tn=256)
        xf = _matmul_res(merged, w_out[i].astype(BF16), xf, tm=tm, tn=1024, tk=D_MODEL)
        u = _mlp_up(xf, g_mlp[i].reshape(1, D_MODEL), w_up[i].astype(BF16), tm=tm, tn=1024)
        xf = _matmul_res(u, w_down[i].astype(BF16), xf, tm=tm, tn=1024, tk=2048)
    out = _final_norm(xf, g_final.reshape(1, D_MODEL), tm=min(512, seq))
    return out.reshape(b, seq, D_MODEL).astype(x.dtype)
```

```python
import functools

import jax
import jax.numpy as jnp
from jax import lax
from jax.experimental import pallas as pl
from jax.experimental.pallas import tpu as pltpu

F32 = jnp.float32
BF16 = jnp.bfloat16

D_MODEL = 2048
DEPTH = 2
SSD_HEADS = 32
SSD_HEAD_DIM = 64
SSD_INNER = SSD_HEADS * SSD_HEAD_DIM
SSD_GROUPS = 8
SSD_STATE = 128
SSD_CHUNK = 128
HEADS_PER_GROUP = SSD_HEADS // SSD_GROUPS
GROUP_WIDTH = HEADS_PER_GROUP * SSD_HEAD_DIM
CONV_WIDTH = 5
CONV_PAD = (CONV_WIDTH - 1) // 2
XBC_WIDTH = SSD_INNER + 2 * SSD_GROUPS * SSD_STATE
HEAD_DIM = 128
ROPE_DIM = HEAD_DIM // 4
ROPE_HALF = ROPE_DIM // 2
ROPE_PARTNER = HEAD_DIM // 2
ROPE_THETA = 500000.0
DIL_PATTERNS = ((128, 1), (512, 4), (2048, 16))
DIL_GROUPS = len(DIL_PATTERNS)
DIL_HEADS = 8
DIL_WIDTH = DIL_HEADS * HEAD_DIM
DIL_HALF = 64
TILE_BATCH = 7
SEGMENT_TILE_BATCH = 8
WIN_Q_HEADS = 16
WIN_KV_HEADS = 4
WIN_REP = WIN_Q_HEADS // WIN_KV_HEADS
WIN_HALF = 128
D_FF = 4 * D_MODEL
N_BRANCH = 3
EPS = 1e-6
NEG_INF = -1e30
LOG2E = 1.4426950408889634

SSD_SLAB = GROUP_WIDTH + 2 * SSD_STATE
WIN_SLAB = (WIN_REP + 2) * HEAD_DIM
DIL_SLAB = 3 * DIL_GROUPS * HEAD_DIM
OFF_Z = 0
OFF_SSD = OFF_Z + SSD_INNER
OFF_WIN = OFF_SSD + SSD_GROUPS * SSD_SLAB
OFF_DIL = OFF_WIN + WIN_KV_HEADS * WIN_SLAB
OFF_GATE = OFF_DIL + DIL_HEADS * DIL_SLAB
N_MAIN = OFF_GATE + N_BRANCH * D_MODEL
assert OFF_SSD % SSD_SLAB == 0 and OFF_WIN % WIN_SLAB == 0 and OFF_DIL % DIL_SLAB == 0
DT_COLS = 2 * SSD_HEADS
DT_IN_OFF = SSD_INNER + XBC_WIDTH
DT_LANES = 128

LANE = 128
SUBLANE = 8
VMEM_LIMIT = 56 * 1024 * 1024


def _split3(x):
    hi = x.astype(BF16)
    r1 = x - hi.astype(F32)
    mid = r1.astype(BF16)
    r2 = r1 - mid.astype(F32)
    return hi, mid, r2.astype(BF16)


def _dot(a, b):
    return jnp.dot(a, b, preferred_element_type=F32)


def _dot_nt(a, b):
    return lax.dot_general(a, b, (((1,), (1,)), ((), ())), preferred_element_type=F32)


def _exact_right(x, mat):
    hi, mid, lo = _split3(x)
    return _dot(hi, mat) + _dot(mid, mat) + _dot(lo, mat)


def _silu(x):
    h = 0.5 * x
    return h * jnp.tanh(h) + h


def _softplus(x):
    return jnp.maximum(x, 0.0) + jnp.log1p(jnp.exp(-jnp.abs(x)))


def _norm_rows(x, g):
    ms = jnp.mean(x * x, axis=-1, keepdims=True)
    return x * lax.rsqrt(ms + EPS) * g


def _rope_tile(a, ct, st):
    return a * ct + pltpu.roll(a, ROPE_PARTNER, 1) * st


def _proj_kernel(x_ref, g_ref, w_ref, wdtt_ref, bdtt_ref, ct_ref, st_ref,
                 o_ref, dtt_ref, h_ref, *, tn, tile_kinds):
    j = pl.program_id(1)

    @pl.when(j == 0)
    def _():
        hb = _norm_rows(x_ref[...], g_ref[...]).astype(BF16)
        h_ref[...] = hb
        dtt_ref[...] = _softplus(_dot_nt(wdtt_ref[...], hb) + bdtt_ref[...])

    def tile_body(kind):
        acc = _dot(h_ref[...], w_ref[...])
        if 'r' in kind:
            ct, st = ct_ref[...], st_ref[...]
        for c, k in enumerate(kind):
            sl = slice(c * HEAD_DIM, (c + 1) * HEAD_DIM)
            a = acc[:, sl]
            if k == 'r':
                a = _rope_tile(a, ct, st)
            elif k == 'g':
                a = 0.5 * jnp.tanh(0.5 * a) + 0.5
            o_ref[:, sl] = a.astype(o_ref.dtype)

    for kind in sorted(set(tile_kinds)):
        idx = [t for t, k in enumerate(tile_kinds) if k == kind]
        runs, start = [], idx[0]
        for a, b in zip(idx, idx[1:] + [None]):
            if b != a + 1:
                runs.append((start, a))
                start = b
        cond = None
        for lo, hi in runs:
            c = jnp.logical_and(j >= lo, j <= hi)
            cond = c if cond is None else jnp.logical_or(cond, c)
        pl.when(cond)(functools.partial(tile_body, kind))


def _proj_tile_kinds(tn):
    def head_kind(col):
        if col >= OFF_GATE:
            return 'g'
        if col >= OFF_DIL:
            return 'r'
        if col >= OFF_WIN:
            return 'r' if (col - OFF_WIN) % WIN_SLAB < (WIN_REP + 1) * HEAD_DIM else 'p'
        return 'p'
    return tuple(''.join(head_kind(t * tn + c * HEAD_DIM) for c in range(tn // HEAD_DIM))
                 for t in range(N_MAIN // tn))


def _proj(x2d, g, w_main, w_dtt, b_dtt, ct, st, seq, *, tm, tn):
    m = x2d.shape[0]
    assert seq % tm == 0 and m % tm == 0 and N_MAIN % tn == 0 and tn % HEAD_DIM == 0
    seq_tiles = seq // tm
    kern = functools.partial(_proj_kernel, tn=tn, tile_kinds=_proj_tile_kinds(tn))
    return pl.pallas_call(
        kern,
        grid=(m // tm, N_MAIN // tn),
        in_specs=[
            pl.BlockSpec((tm, D_MODEL), lambda i, j: (i, 0)),
            pl.BlockSpec((1, D_MODEL), lambda i, j: (0, 0)),
            pl.BlockSpec((D_MODEL, tn), lambda i, j: (0, j)),
            pl.BlockSpec((DT_LANES, D_MODEL), lambda i, j: (0, 0)),
            pl.BlockSpec((DT_LANES, 1), lambda i, j: (0, 0)),
            pl.BlockSpec((tm, HEAD_DIM), lambda i, j: (i % seq_tiles, 0)),
            pl.BlockSpec((tm, HEAD_DIM), lambda i, j: (i % seq_tiles, 0)),
        ],
        out_specs=[
            pl.BlockSpec((tm, tn), lambda i, j: (i, j)),
            pl.BlockSpec((DT_LANES, tm), lambda i, j: (0, i)),
        ],
        out_shape=[
            jax.ShapeDtypeStruct((m, N_MAIN), BF16),
            jax.ShapeDtypeStruct((DT_LANES, m), F32),
        ],
        scratch_shapes=[pltpu.VMEM((tm, D_MODEL), BF16)],
        compiler_params=pltpu.CompilerParams(
            dimension_semantics=("parallel", "arbitrary"), vmem_limit_bytes=VMEM_LIMIT),
        name="proj",
    )(x2d, g, w_main, w_dtt, b_dtt, ct, st)


def _up_kernel(x_ref, g_ref, w_ref, o_ref, h_ref):
    @pl.when(pl.program_id(1) == 0)
    def _():
        h_ref[...] = _norm_rows(x_ref[...], g_ref[...]).astype(BF16)

    acc = jnp.maximum(_dot(h_ref[...], w_ref[...]), 0.0)
    o_ref[...] = (acc * acc).astype(o_ref.dtype)


def _mlp_up(x2d, g, w, *, tm, tn):
    m, n = x2d.shape[0], w.shape[1]
    return pl.pallas_call(
        _up_kernel,
        grid=(m // tm, n // tn),
        in_specs=[
            pl.BlockSpec((tm, D_MODEL), lambda i, j: (i, 0)),
            pl.BlockSpec((1, D_MODEL), lambda i, j: (0, 0)),
            pl.BlockSpec((D_MODEL, tn), lambda i, j: (0, j)),
        ],
        out_specs=pl.BlockSpec((tm, tn), lambda i, j: (i, j)),
        out_shape=jax.ShapeDtypeStruct((m, n), BF16),
        scratch_shapes=[pltpu.VMEM((tm, D_MODEL), BF16)],
        compiler_params=pltpu.CompilerParams(
            dimension_semantics=("parallel", "arbitrary"), vmem_limit_bytes=VMEM_LIMIT),
        name="mlp_up",
    )(x2d, g, w)


def _matmul_res_kernel(a_ref, w_ref, r_ref, o_ref, acc_ref):
    k = pl.program_id(2)

    @pl.when(k == 0)
    def _():
        acc_ref[...] = jnp.zeros_like(acc_ref)

    acc_ref[...] += _dot(a_ref[...], w_ref[...])

    @pl.when(k == pl.num_programs(2) - 1)
    def _():
        o_ref[...] = r_ref[...] + acc_ref[...]


def _matmul_res(a, w, res, *, tm, tn, tk):
    m, kdim = a.shape
    n = w.shape[1]
    return pl.pallas_call(
        _matmul_res_kernel,
        grid=(m // tm, n // tn, kdim // tk),
        in_specs=[
            pl.BlockSpec((tm, tk), lambda i, j, k: (i, k)),
            pl.BlockSpec((tk, tn), lambda i, j, k: (k, j)),
            pl.BlockSpec((tm, tn), lambda i, j, k: (i, j)),
        ],
        out_specs=pl.BlockSpec((tm, tn), lambda i, j, k: (i, j)),
        out_shape=jax.ShapeDtypeStruct((m, n), F32),
        scratch_shapes=[pltpu.VMEM((tm, tn), F32)],
        compiler_params=pltpu.CompilerParams(
            dimension_semantics=("parallel", "parallel", "arbitrary"), vmem_limit_bytes=VMEM_LIMIT),
        name="matmul_res",
    )(a, w, res)


def _gated_norm(ys, z, nw):
    z = z.astype(F32)
    y = ys.astype(F32) * _silu(z)
    return _norm_rows(y, nw).astype(BF16)


def _merge_kernel(ys_ref, z_ref, nw_ref, yb_ref, yc_ref, g0_ref, g1_ref, g2_ref,
                  wa_ref, wb_ref, wc_ref, o_ref, ya_ref):
    @pl.when(pl.program_id(1) == 0)
    def _():
        ya_ref[...] = _gated_norm(ys_ref[...], z_ref[...], nw_ref[...])

    acc = g0_ref[...].astype(F32) * _dot(ya_ref[...], wa_ref[...])
    acc += g1_ref[...].astype(F32) * _dot(yb_ref[...], wb_ref[...])
    acc += g2_ref[...].astype(F32) * _dot(yc_ref[...], wc_ref[...])
    o_ref[...] = acc.astype(o_ref.dtype)


def _merge(y_ssd, proj, norm_w, y_b, y_c, w_a, w_b, w_c, *, tm, tn):
    m = y_ssd.shape[0]
    gate_blk = OFF_GATE // tn
    per_gate = D_MODEL // tn
    return pl.pallas_call(
        _merge_kernel,
        grid=(m // tm, D_MODEL // tn),
        in_specs=[
            pl.BlockSpec((tm, SSD_INNER), lambda i, j: (i, 0)),
            pl.BlockSpec((tm, SSD_INNER), lambda i, j: (i, OFF_Z // SSD_INNER)),
            pl.BlockSpec((1, SSD_INNER), lambda i, j: (0, 0)),
            pl.BlockSpec((tm, DIL_WIDTH), lambda i, j: (i, 0)),
            pl.BlockSpec((tm, WIN_Q_HEADS * HEAD_DIM), lambda i, j: (i, 0)),
            pl.BlockSpec((tm, tn), lambda i, j: (i, gate_blk + j)),
            pl.BlockSpec((tm, tn), lambda i, j: (i, gate_blk + per_gate + j)),
            pl.BlockSpec((tm, tn), lambda i, j: (i, gate_blk + 2 * per_gate + j)),
            pl.BlockSpec((SSD_INNER, tn), lambda i, j: (0, j)),
            pl.BlockSpec((DIL_WIDTH, tn), lambda i, j: (0, j)),
            pl.BlockSpec((WIN_Q_HEADS * HEAD_DIM, tn), lambda i, j: (0, j)),
        ],
        out_specs=pl.BlockSpec((tm, tn), lambda i, j: (i, j)),
        out_shape=jax.ShapeDtypeStruct((m, D_MODEL), BF16),
        scratch_shapes=[pltpu.VMEM((tm, SSD_INNER), BF16)],
        compiler_params=pltpu.CompilerParams(
            dimension_semantics=("parallel", "arbitrary"), vmem_limit_bytes=VMEM_LIMIT),
        name="merge",
    )(y_ssd, proj, norm_w, y_b, y_c, proj, proj, proj, w_a, w_b, w_c)


def _final_norm_kernel(x_ref, g_ref, o_ref):
    o_ref[...] = _norm_rows(x_ref[...], g_ref[...])


def _final_norm(x2d, g, *, tm):
    m = x2d.shape[0]
    return pl.pallas_call(
        _final_norm_kernel,
        grid=(m // tm,),
        in_specs=[pl.BlockSpec((tm, D_MODEL), lambda i: (i, 0)),
                  pl.BlockSpec((1, D_MODEL), lambda i: (0, 0))],
        out_specs=pl.BlockSpec((tm, D_MODEL), lambda i: (i, 0)),
        out_shape=jax.ShapeDtypeStruct((m, D_MODEL), F32),
        compiler_params=pltpu.CompilerParams(dimension_semantics=("parallel",)),
        name="final_norm",
    )(x2d, g)


def _ssd_kernel(xbc_ref, cw_ref, cb_ref, dtt_ref, alt_ref, p64_ref,
                y_ref,
                pad_s, xs_s, bm_s, cm_s, bmt_s, dtr_s, y_s, st_s, tri_s, u_s, e_s, scl_s, *, seq):
    t = SSD_CHUNK
    n_chunks = seq // t
    for c in range(n_chunks):
        dtr_s[c] = dtt_ref[:, c * t:(c + 1) * t]

    halo = SUBLANE
    for lane_tile in range(SSD_SLAB // LANE):
        lanes = slice(lane_tile * LANE, (lane_tile + 1) * LANE)
        pad_s[lane_tile, 0:halo, :] = jnp.zeros((halo, LANE), F32)
        pad_s[lane_tile, halo + seq:2 * halo + seq, :] = jnp.zeros((halo, LANE), F32)
        pad_s[lane_tile, halo:halo + seq, :] = xbc_ref[0, :, lanes].astype(F32)

    def conv_silu(r0, lane_tile):
        lanes = slice(lane_tile * LANE, (lane_tile + 1) * LANE)
        acc = cb_ref[:, lanes]
        for k in range(CONV_WIDTH):
            lo = halo - CONV_PAD + k + r0
            acc = acc + cw_ref[k:k + 1, lanes] * pad_s[lane_tile, lo:lo + t, :]
        return _silu(acc)

    x_tiles = GROUP_WIDTH // LANE
    for c in range(n_chunks):
        r0 = c * t
        rows = slice(r0, r0 + t)
        for lane_tile in range(x_tiles):
            lanes = slice(lane_tile * LANE, (lane_tile + 1) * LANE)
            xv = conv_silu(r0, lane_tile)
            xs_s[rows, lanes] = xv.astype(BF16)
            y_s[rows, lanes] = xv * p64_ref[0, 0:1, lanes]
        bv = conv_silu(r0, x_tiles)
        bm_s[rows, :] = bv.astype(BF16)
        bmt_s[c] = bv.T.astype(BF16)
        cm_s[rows, :] = conv_silu(r0, x_tiles + 1).astype(BF16)

    ri = lax.broadcasted_iota(jnp.int32, (t, t), 0)
    ci = lax.broadcasted_iota(jnp.int32, (t, t), 1)
    lower, upper = ri >= ci, ri <= ci
    tri_s[:, 0:t] = upper.astype(BF16)
    tri_s[:, t:2 * t] = lower.astype(BF16)
    st_s[...] = jnp.zeros_like(st_s)

    nh = HEADS_PER_GROUP
    a_rows = -jnp.exp(alt_ref[...])
    head_of_lane = lax.broadcasted_iota(jnp.int32, (1, GROUP_WIDTH), 1) // SSD_HEAD_DIM
    low_half = lax.broadcasted_iota(jnp.int32, (1, LANE), 1) < SSD_HEAD_DIM

    def lanes_from_row(row):
        return jnp.broadcast_to(row, (t, t)).T

    def expand64(cols):
        return jnp.concatenate([jnp.where(low_half, cols[0], cols[1]),
                                jnp.where(low_half, cols[2], cols[3])], axis=1)

    def prepare(chunks):
        loaded = []
        for c in chunks:
            rows = pl.ds(pl.multiple_of(c * t, t), t)
            loaded.append((c, rows, dtr_s[c], xs_s[rows, :], cm_s[rows, :], bm_s[rows, :], bmt_s[c]))
        cums = [_exact_right(dt_all, tri_s[...]) for _, _, dt_all, _, _, _, _ in loaded]
        cbs = [_dot_nt(cm_c, bm_c) for _, _, _, _, cm_c, bm_c, _ in loaded]
        staged = []
        for (c, rows, dt_all, xs_c, _, _, bmt_c), cum, cb in zip(loaded, cums, cbs):
            cs = (cum[0:nh, 0:t] * a_rows[0:nh], cum[nh:2 * nh, t:2 * t] * a_rows[nh:2 * nh])
            dts = (dt_all[0:nh, :], dt_all[nh:2 * nh, :])
            tots = (cs[0][:, t - 1:t], cs[1][:, 0:1])
            w_rows = [dts[d] * jnp.exp(tots[d] - cs[d]) for d in range(2)]
            grow = [jnp.exp(tots[d]) for d in range(2)]
            bmt_f = bmt_c.astype(F32)
            l_parts, bw, seen, scale = [], ([], []), ([], []), [0.0, 0.0]
            for h in range(nh):
                mix = jnp.zeros((t, t), F32)
                for d, valid in ((0, lower), (1, upper)):
                    row = cs[d][h:h + 1, :]
                    col = lanes_from_row(row)
                    mix = mix + jnp.exp(jnp.where(valid, col - row, NEG_INF)) * dts[d][h:h + 1, :]
                    seen[d].append(jnp.exp(col))
                    bw[d].append((bmt_f * w_rows[d][h:h + 1, :]).astype(BF16))
                    scale[d] = scale[d] + jnp.where(head_of_lane == h, grow[d][h:h + 1, :], 0.0)
                l_parts.append((cb * mix).astype(BF16))
            lhs = jnp.concatenate([jnp.concatenate(l_parts, axis=1), jnp.concatenate(bw[0], axis=1),
                                   jnp.concatenate(bw[1], axis=1)], axis=0)
            xs_stack = jnp.concatenate(
                [jnp.where(head_of_lane == h, xs_c, jnp.zeros_like(xs_c)) for h in range(nh)], axis=0)
            staged.append((c, rows, lhs, xs_stack, seen, scale))
        outs = [_dot(lhs, xs_stack) for _, _, lhs, xs_stack, _, _ in staged]
        for (c, rows, _, _, seen, scale), out in zip(staged, outs):
            y_s[rows, :] += out[0:t, :]
            for d in range(2):
                u_s[d, c] = out[(d + 1) * t:(d + 2) * t, :]
                e_s[d, c] = expand64(seen[d])
                scl_s[d, c] = jnp.broadcast_to(scale[d], (SUBLANE, GROUP_WIDTH))

    batch = 16
    def prep_body(i, carry):
        prepare([i * batch + u for u in range(batch)])
        return carry

    lax.fori_loop(0, n_chunks // batch, prep_body, 0)

    def scan_body(i, carry):
        steps = []
        for d, c in ((0, i), (1, n_chunks - 1 - i)):
            rows = pl.ds(pl.multiple_of(c * t, t), t)
            steps.append((d, c, rows, st_s[d], cm_s[rows, :]))
        reads = [_dot(cm_c, st.astype(BF16)) for _, _, _, st, cm_c in steps]
        for (d, c, rows, st, _), read in zip(steps, reads):
            y_s[rows, :] += read * e_s[d, c]
            st_s[d] = st * scl_s[d, c][0:1, :] + u_s[d, c]
        return carry

    lax.fori_loop(0, n_chunks, scan_body, 0, unroll=8)
    y_ref[0] = y_s[...].astype(y_ref.dtype)


def _ssd(proj3, conv_w, conv_b, dtt, alog_col, p64):
    b, seq, _ = proj3.shape
    gw, ns = GROUP_WIDTH, SSD_STATE
    slab0 = OFF_SSD // SSD_SLAB
    kern = functools.partial(_ssd_kernel, seq=seq)
    return pl.pallas_call(
        kern,
        grid=(b, SSD_GROUPS),
        in_specs=[
            pl.BlockSpec((1, seq, SSD_SLAB), lambda i, g: (i, 0, slab0 + g)),
            pl.BlockSpec((CONV_WIDTH, SSD_SLAB), lambda i, g: (0, g)),
            pl.BlockSpec((1, SSD_SLAB), lambda i, g: (0, g)),
            pl.BlockSpec((2 * HEADS_PER_GROUP, seq), lambda i, g: (g, i)),
            pl.BlockSpec((2 * HEADS_PER_GROUP, 1), lambda i, g: (g, 0)),
            pl.BlockSpec((1, SUBLANE, gw), lambda i, g: (g, 0, 0)),
        ],
        out_specs=pl.BlockSpec((1, seq, gw), lambda i, g: (i, 0, g)),
        out_shape=jax.ShapeDtypeStruct((b, seq, SSD_INNER), BF16),
        scratch_shapes=[
            pltpu.VMEM((SSD_SLAB // LANE, seq + 2 * SUBLANE, LANE), F32),
            pltpu.VMEM((seq, gw), BF16),
            pltpu.VMEM((seq, ns), BF16),
            pltpu.VMEM((seq, ns), BF16),
            pltpu.VMEM((seq // SSD_CHUNK, ns, SSD_CHUNK), BF16),
            pltpu.VMEM((seq // SSD_CHUNK, 2 * HEADS_PER_GROUP, SSD_CHUNK), F32),
            pltpu.VMEM((seq, gw), F32),
            pltpu.VMEM((2, ns, gw), F32),
            pltpu.VMEM((SSD_CHUNK, 2 * SSD_CHUNK), BF16),
            pltpu.VMEM((2, seq // SSD_CHUNK, ns, gw), F32),
            pltpu.VMEM((2, seq // SSD_CHUNK, SSD_CHUNK, gw), F32),
            pltpu.VMEM((2, seq // SSD_CHUNK, SUBLANE, gw), F32),
        ],
        compiler_params=pltpu.CompilerParams(
            dimension_semantics=("parallel", "arbitrary"), vmem_limit_bytes=VMEM_LIMIT),
        name="ssd",
    )(proj3, conv_w, conv_b, dtt, alog_col, p64)


def _win_kernel(sink_ref, qkv_ref, o_ref, bias_s, vext_s, *, seq):
    q_cols = slice(0, WIN_REP * HEAD_DIM)
    k_cols = slice(WIN_REP * HEAD_DIM, (WIN_REP + 1) * HEAD_DIM)
    v_cols = slice((WIN_REP + 1) * HEAD_DIM, (WIN_REP + 2) * HEAD_DIM)
    blk = WIN_HALF
    kwin = 3 * blk
    n_blk = seq // blk
    kv = pl.program_id(1)
    scale2 = (HEAD_DIM ** -0.5) * LOG2E

    ri = lax.broadcasted_iota(jnp.int32, (blk, kwin), 0)
    ci = lax.broadcasted_iota(jnp.int32, (blk, kwin), 1)
    for w, shift in enumerate((0, -blk, -2 * blk)):
        bias_s[w] = jnp.where(jnp.abs(ci + shift - ri) <= WIN_HALF, 0.0, NEG_INF).astype(F32)
    vext_s[:, 0:HEAD_DIM] = qkv_ref[0, :, v_cols]
    vext_s[:, HEAD_DIM:2 * HEAD_DIM] = jnp.ones((seq, HEAD_DIM), BF16)
    sinks = [sink_ref[kv * WIN_REP + r] * LOG2E for r in range(WIN_REP)]

    def blocks(specs):
        loaded = []
        for qi, which in specs:
            q0 = qi * blk if isinstance(qi, int) else pl.multiple_of(qi * blk, blk)
            k0 = (0, q0 - blk, seq - kwin)[which]
            if not isinstance(k0, int):
                k0 = pl.multiple_of(k0, blk)
            loaded.append((q0, qkv_ref[0, pl.ds(q0, blk), q_cols], qkv_ref[0, pl.ds(k0, kwin), k_cols],
                           vext_s[pl.ds(k0, kwin), :], bias_s[which]))
        scores = [[_dot_nt(qt[:, r * HEAD_DIM:(r + 1) * HEAD_DIM], kt) for r in range(WIN_REP)]
                  for _, qt, kt, _, _ in loaded]
        maxes, probs = [], []
        for heads, (_, _, _, _, bias) in zip(scores, loaded):
            for r, s in enumerate(heads):
                s = s * scale2 + bias
                m = jnp.maximum(jnp.max(s, axis=-1, keepdims=True), sinks[r])
                maxes.append(m)
                probs.append(jnp.exp2(s - m).astype(BF16))
        accs = [_dot(probs[i * WIN_REP + r], vt)
                for i, (_, _, _, vt, _) in enumerate(loaded) for r in range(WIN_REP)]
        for i, (q0, _, _, _, _) in enumerate(loaded):
            for r in range(WIN_REP):
                acc, m = accs[i * WIN_REP + r], maxes[i * WIN_REP + r]
                den = acc[:, HEAD_DIM:] + jnp.exp2(sinks[r] - m)
                o_ref[0, pl.ds(q0, blk), r * HEAD_DIM:(r + 1) * HEAD_DIM] = (
                    acc[:, :HEAD_DIM] / den).astype(o_ref.dtype)

    blocks([(0, 0), (n_blk - 1, 2)])
    pairs = (n_blk - 2) // 2

    def body(pi, carry):
        blocks([(1 + 2 * pi, 1), (2 + 2 * pi, 1)])
        return carry

    lax.fori_loop(0, pairs, body, 0)
    if (n_blk - 2) % 2:
        blocks([(n_blk - 2, 1)])


def _win_attn(proj3, sink):
    b, seq, _ = proj3.shape
    assert seq >= 3 * WIN_HALF and seq % WIN_HALF == 0
    qw = WIN_REP * HEAD_DIM
    kern = functools.partial(_win_kernel, seq=seq)
    return pl.pallas_call(
        kern,
        grid=(b, WIN_KV_HEADS),
        in_specs=[
            pl.BlockSpec(memory_space=pltpu.SMEM),
            pl.BlockSpec((1, seq, WIN_SLAB), lambda i, g: (i, 0, OFF_WIN // WIN_SLAB + g)),
        ],
        out_specs=pl.BlockSpec((1, seq, qw), lambda i, g: (i, 0, g)),
        out_shape=jax.ShapeDtypeStruct((b, seq, WIN_Q_HEADS * HEAD_DIM), BF16),
        scratch_shapes=[pltpu.VMEM((3, WIN_HALF, 3 * WIN_HALF), F32),
                        pltpu.VMEM((seq, 2 * HEAD_DIM), BF16)],
        compiler_params=pltpu.CompilerParams(
            dimension_semantics=("parallel", "parallel"), vmem_limit_bytes=VMEM_LIMIT),
        name="win_attn",
    )(sink, proj3)


def _dil_kernel(qkv_ref, o_ref, stage_s, qd_s, kd_s, vext_s, od_s, ld_s, og_s, lg_s, bias_s, bias1_s,
                *, seq):
    def cols(part, gi):
        c0 = (part * DIL_GROUPS + gi) * HEAD_DIM
        return slice(c0, c0 + HEAD_DIM)

    tq = 2 * DIL_HALF
    kw = tq + 2 * DIL_HALF
    scale2 = (HEAD_DIM ** -0.5) * LOG2E

    ri = lax.broadcasted_iota(jnp.int32, (tq, kw), 0)
    ci = lax.broadcasted_iota(jnp.int32, (tq, kw), 1)
    for w, shift in enumerate((0, -DIL_HALF, -2 * DIL_HALF)):
        bias_s[w] = jnp.where(jnp.abs(ci + shift - ri) <= DIL_HALF, 0.0, NEG_INF).astype(F32)
    ri1 = lax.broadcasted_iota(jnp.int32, (tq, tq), 0)
    ci1 = lax.broadcasted_iota(jnp.int32, (tq, tq), 1)
    bias1_s[...] = jnp.where(jnp.abs(ci1 - ri1) <= DIL_HALF, 0.0, NEG_INF).astype(F32)
    vext_s[:, HEAD_DIM:2 * HEAD_DIM] = jnp.ones((seq, HEAD_DIM), BF16)

    def deinterleave(src_cols, dst_ref, dil):
        n_sub = seq // dil
        stage_s[...] = qkv_ref[0, :, src_cols].astype(F32)
        for r in range(dil):
            dst_ref[r * n_sub:(r + 1) * n_sub, 0:HEAD_DIM] = (
                stage_s[pl.ds(r, n_sub, stride=dil), :].astype(BF16))

    def aligned(x, mult):
        return x if isinstance(x, int) else pl.multiple_of(x, mult)

    def run_group(q_at, k_at, n_sub):
        def tiles(specs):
            loaded = []
            for q0, k0, width, bias in specs:
                q0, k0 = aligned(q0, tq), aligned(k0, DIL_HALF)
                loaded.append((q0, q_at(pl.ds(q0, tq)), k_at(pl.ds(k0, width)),
                               vext_s[pl.ds(k0, width), :], bias))
            scores = [_dot_nt(qt, kt) for _, qt, kt, _, _ in loaded]
            maxes, probs = [], []
            for s, (_, _, _, _, bias) in zip(scores, loaded):
                s = s * scale2 + bias
                m = jnp.max(s, axis=-1, keepdims=True)
                maxes.append(m)
                probs.append(jnp.exp2(s - m).astype(BF16))
            accs = [_dot(p, vt) for p, (_, _, _, vt, _) in zip(probs, loaded)]
            for acc, m, (q0, _, _, _, _) in zip(accs, maxes, loaded):
                den = acc[:, HEAD_DIM:]
                od_s[pl.ds(q0, tq), :] = acc[:, :HEAD_DIM] / den
                ld_s[pl.ds(q0, tq), :] = m + jnp.log2(den)

        def first(seg0):
            return (seg0, seg0, kw, bias_s[0])

        def inner(q0):
            return (q0, q0 - DIL_HALF, kw, bias_s[1])

        def last(seg0):
            return (seg0 + n_sub - tq, seg0 + n_sub - kw, kw, bias_s[2])

        tiles_per_seg, n_seg, nb = n_sub // tq, seq // n_sub, TILE_BATCH
        if tiles_per_seg == 1:
            wide = SEGMENT_TILE_BATCH
            def body1(bi, carry):
                tiles([((bi * wide + u) * tq, (bi * wide + u) * tq, tq, bias1_s[...]) for u in range(wide)])
                return carry
            lax.fori_loop(0, seq // (tq * wide), body1, 0)
        elif SEGMENT_TILE_BATCH % tiles_per_seg == 0:
            segs = SEGMENT_TILE_BATCH // tiles_per_seg
            def seg_body(si, carry):
                specs = []
                for u in range(segs):
                    seg0 = (si * segs + u) * n_sub
                    specs += ([first(seg0)] + [inner(seg0 + ti * tq) for ti in range(1, tiles_per_seg - 1)]
                              + [last(seg0)])
                tiles(specs)
                return carry
            lax.fori_loop(0, n_seg // segs, seg_body, 0)
        else:
            assert n_seg == 1
            loops = (tiles_per_seg - 2) // nb
            tiles([first(0)])
            def inner_body(bi, carry):
                tiles([inner((1 + bi * nb + u) * tq) for u in range(nb)])
                return carry
            lax.fori_loop(0, loops, inner_body, 0)
            tiles([inner(ti * tq) for ti in range(1 + loops * nb, tiles_per_seg - 1)] + [last(0)])

    for gi, (_, dil) in enumerate(DIL_PATTERNS):
        n_sub = seq // dil
        if dil == 1:
            vext_s[:, 0:HEAD_DIM] = qkv_ref[0, :, cols(2, gi)]
            run_group(lambda sl: qkv_ref[0, sl, cols(0, gi)], lambda sl: qkv_ref[0, sl, cols(1, gi)], n_sub)
            og_s[gi] = od_s[...]
            lg_s[gi] = ld_s[...]
        else:
            deinterleave(cols(0, gi), qd_s, dil)
            deinterleave(cols(1, gi), kd_s, dil)
            deinterleave(cols(2, gi), vext_s, dil)
            run_group(lambda sl: qd_s[sl, :], lambda sl: kd_s[sl, :], n_sub)
            for r in range(dil):
                og_s[gi, pl.ds(r, n_sub, stride=dil), :] = od_s[r * n_sub:(r + 1) * n_sub, :]
                lg_s[gi, pl.ds(r, n_sub, stride=dil), :] = ld_s[r * n_sub:(r + 1) * n_sub, :]

    m = jnp.maximum(jnp.maximum(lg_s[0], lg_s[1]), lg_s[2])
    num = jnp.zeros((seq, HEAD_DIM), F32)
    den = jnp.zeros((seq, HEAD_DIM), F32)
    for gi in range(DIL_GROUPS):
        w = jnp.exp2(lg_s[gi] - m)
        num = num + w * og_s[gi]
        den = den + w
    o_ref[0] = (num / den).astype(o_ref.dtype)


def _dil_attn(proj3):
    b, seq, _ = proj3.shape
    tq = 2 * DIL_HALF
    for _, dil in DIL_PATTERNS:
        assert (seq // dil) % tq == 0
    kern = functools.partial(_dil_kernel, seq=seq)
    return pl.pallas_call(
        kern,
        grid=(b, DIL_HEADS),
        in_specs=[pl.BlockSpec((1, seq, DIL_SLAB), lambda i, h: (i, 0, OFF_DIL // DIL_SLAB + h))],
        out_specs=pl.BlockSpec((1, seq, HEAD_DIM), lambda i, h: (i, 0, h)),
        out_shape=jax.ShapeDtypeStruct((b, seq, DIL_WIDTH), BF16),
        scratch_shapes=[
            pltpu.VMEM((seq, HEAD_DIM), F32),
            pltpu.VMEM((seq, HEAD_DIM), BF16),
            pltpu.VMEM((seq, HEAD_DIM), BF16),
            pltpu.VMEM((seq, 2 * HEAD_DIM), BF16),
            pltpu.VMEM((seq, HEAD_DIM), F32),
            pltpu.VMEM((seq, HEAD_DIM), F32),
            pltpu.VMEM((DIL_GROUPS, seq, HEAD_DIM), F32),
            pltpu.VMEM((DIL_GROUPS, seq, HEAD_DIM), F32),
            pltpu.VMEM((3, tq, tq + 2 * DIL_HALF), F32),
            pltpu.VMEM((tq, tq), F32),
        ],
        compiler_params=pltpu.CompilerParams(
            dimension_semantics=("parallel", "parallel"), vmem_limit_bytes=VMEM_LIMIT),
        name="dil_attn",
    )(proj3)


def _rope_tables(seq):
    inv = ROPE_THETA ** (-jnp.arange(0, ROPE_DIM, 2, dtype=F32) / ROPE_DIM)
    ang = jnp.arange(seq, dtype=F32)[:, None] * inv[None, :]
    cos, sin = jnp.cos(ang), jnp.sin(ang)
    gap = jnp.zeros((seq, ROPE_PARTNER - ROPE_HALF), F32)
    ct = jnp.concatenate([cos, gap + 1.0, cos, gap + 1.0], axis=1)
    st = jnp.concatenate([-sin, gap, sin, gap], axis=1)
    return ct, st


def _group_major(v):
    return v.reshape(2, SSD_GROUPS, HEADS_PER_GROUP).transpose(1, 0, 2).reshape(DT_COLS)


def _ssd_slabs(a):
    rows = a.shape[0]
    x = a[:, :SSD_INNER].reshape(rows, SSD_GROUPS, GROUP_WIDTH)
    bm = a[:, SSD_INNER:SSD_INNER + SSD_GROUPS * SSD_STATE].reshape(rows, SSD_GROUPS, SSD_STATE)
    cm = a[:, SSD_INNER + SSD_GROUPS * SSD_STATE:].reshape(rows, SSD_GROUPS, SSD_STATE)
    return jnp.concatenate([x, bm, cm], axis=2).reshape(rows, XBC_WIDTH)


def _rope_lanes(a, axis):
    idx = lambda lo, hi: lax.slice_in_dim(a, lo, hi, axis=axis)
    return jnp.concatenate([idx(0, ROPE_HALF), idx(ROPE_DIM, ROPE_DIM + ROPE_PARTNER - ROPE_HALF),
                            idx(ROPE_HALF, ROPE_DIM), idx(ROPE_DIM + ROPE_PARTNER - ROPE_HALF, HEAD_DIM)],
                           axis=axis)


def _main_weight(w_in):
    d = w_in.shape[0]
    o = DT_IN_OFF + DT_COLS
    n_dil = 3 * DIL_GROUPS * DIL_WIDTH
    n_q, n_kv = WIN_Q_HEADS * HEAD_DIM, WIN_KV_HEADS * HEAD_DIM
    w_in = w_in.astype(BF16)
    lane_perm = _rope_lanes(jnp.eye(HEAD_DIM, dtype=BF16), 1)

    def rope_lanes(heads):
        return jnp.einsum('dhk,kj->dhj', heads, lane_perm, preferred_element_type=BF16)

    z = w_in[:, :SSD_INNER]
    ssd = _ssd_slabs(w_in[:, SSD_INNER:DT_IN_OFF])
    dil = rope_lanes(w_in[:, o:o + n_dil].reshape(d, 3 * DIL_GROUPS * DIL_HEADS, HEAD_DIM))
    dil = dil.reshape(d, 3 * DIL_GROUPS, DIL_HEADS, HEAD_DIM)
    dil = dil.transpose(0, 2, 1, 3).reshape(d, n_dil)
    qk = rope_lanes(w_in[:, o + n_dil:o + n_dil + n_q + n_kv].reshape(d, WIN_Q_HEADS + WIN_KV_HEADS, HEAD_DIM))
    q = qk[:, :WIN_Q_HEADS].reshape(d, WIN_KV_HEADS, WIN_REP * HEAD_DIM)
    k = qk[:, WIN_Q_HEADS:]
    v = w_in[:, o + n_dil + n_q + n_kv:o + n_dil + n_q + 2 * n_kv].reshape(d, WIN_KV_HEADS, HEAD_DIM)
    win = jnp.concatenate([q, k, v], axis=2).reshape(d, WIN_KV_HEADS * WIN_SLAB)
    gates = w_in[:, o + n_dil + n_q + 2 * n_kv:]
    return jnp.concatenate([z, ssd, win, dil, gates], axis=1)


def _layer_params(w_in, dt_bias, a_log, d_skip):
    w_main = _main_weight(w_in)
    w_dt = w_in[:, DT_IN_OFF:DT_IN_OFF + DT_COLS]
    w_dt = w_dt.reshape(D_MODEL, 2, SSD_GROUPS, HEADS_PER_GROUP).transpose(0, 2, 1, 3).reshape(D_MODEL, DT_COLS)
    w_dtt = jnp.pad(w_dt, ((0, 0), (0, DT_LANES - DT_COLS))).astype(BF16).T
    b_dt = jnp.pad(_group_major(dt_bias.astype(F32)), (0, DT_LANES - DT_COLS))
    alog = jnp.pad(_group_major(a_log.astype(F32)), (0, DT_LANES - DT_COLS))
    p64 = jnp.repeat(d_skip.astype(F32).reshape(SSD_GROUPS, 1, HEADS_PER_GROUP), SSD_HEAD_DIM, axis=2)
    p64 = jnp.pad(p64, ((0, 0), (0, SUBLANE - 1), (0, 0)))
    return w_main, w_dtt, b_dt.reshape(DT_LANES, 1), alog.reshape(DT_LANES, 1), p64


def kernel(x, g_mix, w_in, conv_w, conv_b, dt_bias, a_log, d_skip, ssd_norm, w_a, w_b, w_c, sink,
           w_out, g_mlp, w_up, w_down, g_final):
    b, seq, _ = x.shape
    m = b * seq
    tm = min(1024, seq)
    ct, st = _rope_tables(seq)
    xf = x.reshape(m, D_MODEL).astype(F32)
    for i in range(DEPTH):
        w_main, w_dtt, b_dtt, alog_col, p64 = _layer_params(w_in[i], dt_bias[i], a_log[i], d_skip[i])
        proj, dtt = _proj(xf, g_mix[i].reshape(1, D_MODEL), w_main, w_dtt, b_dtt, ct, st, seq,
                          tm=tm, tn=1024)
        proj3 = proj.reshape(b, seq, N_MAIN)
        y_ssd = _ssd(proj3, _ssd_slabs(conv_w[i].astype(F32)),
                     _ssd_slabs(conv_b[i].astype(F32).reshape(1, XBC_WIDTH)), dtt, alog_col, p64)
        y_b = _dil_attn(proj3)
        w_b_heads = _rope_lanes(w_b[i].reshape(DIL_HEADS, HEAD_DIM, D_MODEL), 1)
        y_c = _win_attn(proj3, sink[i].astype(F32))
        merged = _merge(y_ssd.reshape(m, SSD_INNER), proj, ssd_norm[i].reshape(1, SSD_INNER),
                        y_b.reshape(m, DIL_WIDTH), y_c.reshape(m, WIN_Q_HEADS * HEAD_DIM),
                        w_a[i].astype(BF16), w_b_heads.reshape(DIL_WIDTH, D_MODEL).astype(BF16),
                        w_c[i].astype(BF16), tm=tm, tn=256)
        xf = _matmul_res(merged, w_out[i].astype(BF16), xf, tm=min(512, seq), tn=D_MODEL, tk=D_MODEL)
        u = _mlp_up(xf, g_mlp[i].reshape(1, D_MODEL), w_up[i].astype(BF16), tm=tm, tn=1024)
        xf = _matmul_res(u, w_down[i].astype(BF16), xf, tm=tm, tn=1024, tk=2048)
    out = _final_norm(xf, g_final.reshape(1, D_MODEL), tm=min(512, seq))
    return out.reshape(b, seq, D_MODEL).astype(x.dtype)
```

```python
import functools

import jax
import jax.numpy as jnp
from jax import lax
from jax.experimental import pallas as pl
from jax.experimental.pallas import tpu as pltpu

F32 = jnp.float32
BF16 = jnp.bfloat16

D_MODEL = 2048
DEPTH = 2
SSD_HEADS = 32
SSD_HEAD_DIM = 64
SSD_INNER = SSD_HEADS * SSD_HEAD_DIM
SSD_GROUPS = 8
SSD_STATE = 128
SSD_CHUNK = 128
HEADS_PER_GROUP = SSD_HEADS // SSD_GROUPS
GROUP_WIDTH = HEADS_PER_GROUP * SSD_HEAD_DIM
CONV_WIDTH = 5
CONV_PAD = (CONV_WIDTH - 1) // 2
XBC_WIDTH = SSD_INNER + 2 * SSD_GROUPS * SSD_STATE
HEAD_DIM = 128
ROPE_DIM = HEAD_DIM // 4
ROPE_HALF = ROPE_DIM // 2
ROPE_PARTNER = HEAD_DIM // 2
ROPE_THETA = 500000.0
DIL_PATTERNS = ((128, 1), (512, 4), (2048, 16))
DIL_GROUPS = len(DIL_PATTERNS)
DIL_HEADS = 8
DIL_WIDTH = DIL_HEADS * HEAD_DIM
DIL_HALF = 64
TILE_BATCH = 7
SEGMENT_TILE_BATCH = 8
WIN_Q_HEADS = 16
WIN_KV_HEADS = 4
WIN_REP = WIN_Q_HEADS // WIN_KV_HEADS
WIN_HALF = 128
D_FF = 4 * D_MODEL
N_BRANCH = 3
EPS = 1e-6
NEG_INF = -1e30
LOG2E = 1.4426950408889634

SSD_SLAB = GROUP_WIDTH + 2 * SSD_STATE
WIN_SLAB = (WIN_REP + 2) * HEAD_DIM
DIL_SLAB = 3 * DIL_GROUPS * HEAD_DIM
OFF_Z = 0
OFF_SSD = OFF_Z + SSD_INNER
OFF_WIN = OFF_SSD + SSD_GROUPS * SSD_SLAB
OFF_DIL = OFF_WIN + WIN_KV_HEADS * WIN_SLAB
OFF_GATE = OFF_DIL + DIL_HEADS * DIL_SLAB
N_MAIN = OFF_GATE + N_BRANCH * D_MODEL
assert OFF_SSD % SSD_SLAB == 0 and OFF_WIN % WIN_SLAB == 0 and OFF_DIL % DIL_SLAB == 0
DT_COLS = 2 * SSD_HEADS
DT_IN_OFF = SSD_INNER + XBC_WIDTH
DT_LANES = 128

LANE = 128
SUBLANE = 8
VMEM_LIMIT = 56 * 1024 * 1024


def _split3(x):
    hi = x.astype(BF16)
    r1 = x - hi.astype(F32)
    mid = r1.astype(BF16)
    r2 = r1 - mid.astype(F32)
    return hi, mid, r2.astype(BF16)


def _dot(a, b):
    return jnp.dot(a, b, preferred_element_type=F32)


def _dot_nt(a, b):
    return lax.dot_general(a, b, (((1,), (1,)), ((), ())), preferred_element_type=F32)


def _exact_right(x, mat):
    hi, mid, lo = _split3(x)
    return _dot(hi, mat) + _dot(mid, mat) + _dot(lo, mat)


def _silu(x):
    h = 0.5 * x
    return h * jnp.tanh(h) + h


def _softplus(x):
    return jnp.maximum(x, 0.0) + jnp.log1p(jnp.exp(-jnp.abs(x)))


def _norm_rows(x, g):
    ms = jnp.mean(x * x, axis=-1, keepdims=True)
    return x * lax.rsqrt(ms + EPS) * g


def _rope_tile(a, ct, st):
    return a * ct + pltpu.roll(a, ROPE_PARTNER, 1) * st


def _proj_kernel(x_ref, g_ref, w_ref, wdtt_ref, bdtt_ref, ct_ref, st_ref,
                 o_ref, dtt_ref, h_ref, *, tn, tile_kinds):
    j = pl.program_id(1)

    @pl.when(j == 0)
    def _():
        hb = _norm_rows(x_ref[...], g_ref[...]).astype(BF16)
        h_ref[...] = hb
        dtt_ref[...] = _softplus(_dot_nt(wdtt_ref[...], hb) + bdtt_ref[...])

    def tile_body(kind):
        acc = _dot(h_ref[...], w_ref[...])
        if 'r' in kind:
            ct, st = ct_ref[...], st_ref[...]
        for c, k in enumerate(kind):
            sl = slice(c * HEAD_DIM, (c + 1) * HEAD_DIM)
            a = acc[:, sl]
            if k == 'r':
                a = _rope_tile(a, ct, st)
            elif k == 'g':
                a = 0.5 * jnp.tanh(0.5 * a) + 0.5
            o_ref[:, sl] = a.astype(o_ref.dtype)

    for kind in sorted(set(tile_kinds)):
        idx = [t for t, k in enumerate(tile_kinds) if k == kind]
        runs, start = [], idx[0]
        for a, b in zip(idx, idx[1:] + [None]):
            if b != a + 1:
                runs.append((start, a))
                start = b
        cond = None
        for lo, hi in runs:
            c = jnp.logical_and(j >= lo, j <= hi)
            cond = c if cond is None else jnp.logical_or(cond, c)
        pl.when(cond)(functools.partial(tile_body, kind))


def _proj_tile_kinds(tn):
    def head_kind(col):
        if col >= OFF_GATE:
            return 'g'
        if col >= OFF_DIL:
            return 'r'
        if col >= OFF_WIN:
            return 'r' if (col - OFF_WIN) % WIN_SLAB < (WIN_REP + 1) * HEAD_DIM else 'p'
        return 'p'
    return tuple(''.join(head_kind(t * tn + c * HEAD_DIM) for c in range(tn // HEAD_DIM))
                 for t in range(N_MAIN // tn))


def _proj(x2d, g, w_main, w_dtt, b_dtt, ct, st, seq, *, tm, tn):
    m = x2d.shape[0]
    assert seq % tm == 0 and m % tm == 0 and N_MAIN % tn == 0 and tn % HEAD_DIM == 0
    seq_tiles = seq // tm
    kern = functools.partial(_proj_kernel, tn=tn, tile_kinds=_proj_tile_kinds(tn))
    return pl.pallas_call(
        kern,
        grid=(m // tm, N_MAIN // tn),
        in_specs=[
            pl.BlockSpec((tm, D_MODEL), lambda i, j: (i, 0)),
            pl.BlockSpec((1, D_MODEL), lambda i, j: (0, 0)),
            pl.BlockSpec((D_MODEL, tn), lambda i, j: (0, j)),
            pl.BlockSpec((DT_LANES, D_MODEL), lambda i, j: (0, 0)),
            pl.BlockSpec((DT_LANES, 1), lambda i, j: (0, 0)),
            pl.BlockSpec((tm, HEAD_DIM), lambda i, j: (i % seq_tiles, 0)),
            pl.BlockSpec((tm, HEAD_DIM), lambda i, j: (i % seq_tiles, 0)),
        ],
        out_specs=[
            pl.BlockSpec((tm, tn), lambda i, j: (i, j)),
            pl.BlockSpec((DT_LANES, tm), lambda i, j: (0, i)),
        ],
        out_shape=[
            jax.ShapeDtypeStruct((m, N_MAIN), BF16),
            jax.ShapeDtypeStruct((DT_LANES, m), F32),
        ],
        scratch_shapes=[pltpu.VMEM((tm, D_MODEL), BF16)],
        compiler_params=pltpu.CompilerParams(
            dimension_semantics=("parallel", "arbitrary"), vmem_limit_bytes=VMEM_LIMIT),
        name="proj",
    )(x2d, g, w_main, w_dtt, b_dtt, ct, st)


def _up_kernel(x_ref, g_ref, w_ref, o_ref, h_ref):
    @pl.when(pl.program_id(1) == 0)
    def _():
        h_ref[...] = _norm_rows(x_ref[...], g_ref[...]).astype(BF16)

    acc = jnp.maximum(_dot(h_ref[...], w_ref[...]), 0.0)
    o_ref[...] = (acc * acc).astype(o_ref.dtype)


def _mlp_up(x2d, g, w, *, tm, tn):
    m, n = x2d.shape[0], w.shape[1]
    return pl.pallas_call(
        _up_kernel,
        grid=(m // tm, n // tn),
        in_specs=[
            pl.BlockSpec((tm, D_MODEL), lambda i, j: (i, 0)),
            pl.BlockSpec((1, D_MODEL), lambda i, j: (0, 0)),
            pl.BlockSpec((D_MODEL, tn), lambda i, j: (0, j)),
        ],
        out_specs=pl.BlockSpec((tm, tn), lambda i, j: (i, j)),
        out_shape=jax.ShapeDtypeStruct((m, n), BF16),
        scratch_shapes=[pltpu.VMEM((tm, D_MODEL), BF16)],
        compiler_params=pltpu.CompilerParams(
            dimension_semantics=("parallel", "arbitrary"), vmem_limit_bytes=VMEM_LIMIT),
        name="mlp_up",
    )(x2d, g, w)


def _matmul_res_kernel(a_ref, w_ref, r_ref, o_ref, acc_ref):
    k = pl.program_id(2)

    @pl.when(k == 0)
    def _():
        acc_ref[...] = jnp.zeros_like(acc_ref)

    acc_ref[...] += _dot(a_ref[...], w_ref[...])

    @pl.when(k == pl.num_programs(2) - 1)
    def _():
        o_ref[...] = r_ref[...] + acc_ref[...]


def _matmul_res(a, w, res, *, tm, tn, tk):
    m, kdim = a.shape
    n = w.shape[1]
    return pl.pallas_call(
        _matmul_res_kernel,
        grid=(m // tm, n // tn, kdim // tk),
        in_specs=[
            pl.BlockSpec((tm, tk), lambda i, j, k: (i, k)),
            pl.BlockSpec((tk, tn), lambda i, j, k: (k, j)),
            pl.BlockSpec((tm, tn), lambda i, j, k: (i, j)),
        ],
        out_specs=pl.BlockSpec((tm, tn), lambda i, j, k: (i, j)),
        out_shape=jax.ShapeDtypeStruct((m, n), F32),
        scratch_shapes=[pltpu.VMEM((tm, tn), F32)],
        compiler_params=pltpu.CompilerParams(
            dimension_semantics=("parallel", "parallel", "arbitrary"), vmem_limit_bytes=VMEM_LIMIT),
        name="matmul_res",
    )(a, w, res)


def _gated_norm(ys, z, nw):
    z = z.astype(F32)
    y = ys.astype(F32) * _silu(z)
    return _norm_rows(y, nw).astype(BF16)


def _merge_kernel(ys_ref, z_ref, nw_ref, yb_ref, yc_ref, g0_ref, g1_ref, g2_ref,
                  wa_ref, wb_ref, wc_ref, o_ref, ya_ref):
    @pl.when(pl.program_id(1) == 0)
    def _():
        ya_ref[...] = _gated_norm(ys_ref[...], z_ref[...], nw_ref[...])

    acc = g0_ref[...].astype(F32) * _dot(ya_ref[...], wa_ref[...])
    acc += g1_ref[...].astype(F32) * _dot(yb_ref[...], wb_ref[...])
    acc += g2_ref[...].astype(F32) * _dot(yc_ref[...], wc_ref[...])
    o_ref[...] = acc.astype(o_ref.dtype)


def _merge(y_ssd, proj, norm_w, y_b, y_c, w_a, w_b, w_c, *, tm, tn):
    m = y_ssd.shape[0]
    gate_blk = OFF_GATE // tn
    per_gate = D_MODEL // tn
    return pl.pallas_call(
        _merge_kernel,
        grid=(m // tm, D_MODEL // tn),
        in_specs=[
            pl.BlockSpec((tm, SSD_INNER), lambda i, j: (i, 0)),
            pl.BlockSpec((tm, SSD_INNER), lambda i, j: (i, OFF_Z // SSD_INNER)),
            pl.BlockSpec((1, SSD_INNER), lambda i, j: (0, 0)),
            pl.BlockSpec((tm, DIL_WIDTH), lambda i, j: (i, 0)),
            pl.BlockSpec((tm, WIN_Q_HEADS * HEAD_DIM), lambda i, j: (i, 0)),
            pl.BlockSpec((tm, tn), lambda i, j: (i, gate_blk + j)),
            pl.BlockSpec((tm, tn), lambda i, j: (i, gate_blk + per_gate + j)),
            pl.BlockSpec((tm, tn), lambda i, j: (i, gate_blk + 2 * per_gate + j)),
            pl.BlockSpec((SSD_INNER, tn), lambda i, j: (0, j)),
            pl.BlockSpec((DIL_WIDTH, tn), lambda i, j: (0, j)),
            pl.BlockSpec((WIN_Q_HEADS * HEAD_DIM, tn), lambda i, j: (0, j)),
        ],
        out_specs=pl.BlockSpec((tm, tn), lambda i, j: (i, j)),
        out_shape=jax.ShapeDtypeStruct((m, D_MODEL), BF16),
        scratch_shapes=[pltpu.VMEM((tm, SSD_INNER), BF16)],
        compiler_params=pltpu.CompilerParams(
            dimension_semantics=("parallel", "arbitrary"), vmem_limit_bytes=VMEM_LIMIT),
        name="merge",
    )(y_ssd, proj, norm_w, y_b, y_c, proj, proj, proj, w_a, w_b, w_c)


def _final_norm_kernel(x_ref, g_ref, o_ref):
    o_ref[...] = _norm_rows(x_ref[...], g_ref[...])


def _final_norm(x2d, g, *, tm):
    m = x2d.shape[0]
    return pl.pallas_call(
        _final_norm_kernel,
        grid=(m // tm,),
        in_specs=[pl.BlockSpec((tm, D_MODEL), lambda i: (i, 0)),
                  pl.BlockSpec((1, D_MODEL), lambda i: (0, 0))],
        out_specs=pl.BlockSpec((tm, D_MODEL), lambda i: (i, 0)),
        out_shape=jax.ShapeDtypeStruct((m, D_MODEL), F32),
        compiler_params=pltpu.CompilerParams(dimension_semantics=("parallel",)),
        name="final_norm",
    )(x2d, g)


def _ssd_kernel(xbc_ref, cw_ref, cb_ref, dtt_ref, alt_ref, p64_ref,
                y_ref,
                pad_s, xs_s, bm_s, cm_s, bmt_s, dtr_s, y_s, st_s, tri_s, u_s, e_s, scl_s, *, seq):
    t = SSD_CHUNK
    n_chunks = seq // t
    for c in range(n_chunks):
        dtr_s[c] = dtt_ref[:, c * t:(c + 1) * t]

    halo = SUBLANE
    for lane_tile in range(SSD_SLAB // LANE):
        lanes = slice(lane_tile * LANE, (lane_tile + 1) * LANE)
        pad_s[lane_tile, 0:halo, :] = jnp.zeros((halo, LANE), F32)
        pad_s[lane_tile, halo + seq:2 * halo + seq, :] = jnp.zeros((halo, LANE), F32)
        pad_s[lane_tile, halo:halo + seq, :] = xbc_ref[0, :, lanes].astype(F32)

    def conv_silu(r0, lane_tile):
        lanes = slice(lane_tile * LANE, (lane_tile + 1) * LANE)
        acc = cb_ref[:, lanes]
        for k in range(CONV_WIDTH):
            lo = halo - CONV_PAD + k + r0
            acc = acc + cw_ref[k:k + 1, lanes] * pad_s[lane_tile, lo:lo + t, :]
        return _silu(acc)

    x_tiles = GROUP_WIDTH // LANE
    for c in range(n_chunks):
        r0 = c * t
        rows = slice(r0, r0 + t)
        for lane_tile in range(x_tiles):
            lanes = slice(lane_tile * LANE, (lane_tile + 1) * LANE)
            xv = conv_silu(r0, lane_tile)
            xs_s[rows, lanes] = xv.astype(BF16)
            y_s[rows, lanes] = xv * p64_ref[0, 0:1, lanes]
        bv = conv_silu(r0, x_tiles)
        bm_s[rows, :] = bv.astype(BF16)
        bmt_s[c] = bv.T.astype(BF16)
        cm_s[rows, :] = conv_silu(r0, x_tiles + 1).astype(BF16)

    ri = lax.broadcasted_iota(jnp.int32, (t, t), 0)
    ci = lax.broadcasted_iota(jnp.int32, (t, t), 1)
    lower, upper = ri >= ci, ri <= ci
    tri_s[:, 0:t] = upper.astype(BF16)
    tri_s[:, t:2 * t] = lower.astype(BF16)
    st_s[...] = jnp.zeros_like(st_s)

    nh = HEADS_PER_GROUP
    a_rows = -jnp.exp(alt_ref[...])
    head_of_lane = lax.broadcasted_iota(jnp.int32, (1, GROUP_WIDTH), 1) // SSD_HEAD_DIM
    low_half = lax.broadcasted_iota(jnp.int32, (1, LANE), 1) < SSD_HEAD_DIM

    def lanes_from_row(row):
        return jnp.broadcast_to(row, (t, t)).T

    def expand64(cols):
        return jnp.concatenate([jnp.where(low_half, cols[0], cols[1]),
                                jnp.where(low_half, cols[2], cols[3])], axis=1)

    def prepare(chunks):
        loaded = []
        for c in chunks:
            rows = pl.ds(pl.multiple_of(c * t, t), t)
            loaded.append((c, rows, dtr_s[c], xs_s[rows, :], cm_s[rows, :], bm_s[rows, :], bmt_s[c]))
        cums = [_exact_right(dt_all, tri_s[...]) for _, _, dt_all, _, _, _, _ in loaded]
        cbs = [_dot_nt(cm_c, bm_c) for _, _, _, _, cm_c, bm_c, _ in loaded]
        staged = []
        for (c, rows, dt_all, xs_c, _, _, bmt_c), cum, cb in zip(loaded, cums, cbs):
            cs = (cum[0:nh, 0:t] * a_rows[0:nh], cum[nh:2 * nh, t:2 * t] * a_rows[nh:2 * nh])
            dts = (dt_all[0:nh, :], dt_all[nh:2 * nh, :])
            tots = (cs[0][:, t - 1:t], cs[1][:, 0:1])
            w_rows = [dts[d] * jnp.exp(tots[d] - cs[d]) for d in range(2)]
            grow = [jnp.exp(tots[d]) for d in range(2)]
            bmt_f = bmt_c.astype(F32)
            l_parts, bw, seen, scale = [], ([], []), ([], []), [0.0, 0.0]
            for h in range(nh):
                mix = jnp.zeros((t, t), F32)
                for d, valid in ((0, lower), (1, upper)):
                    row = cs[d][h:h + 1, :]
                    col = lanes_from_row(row)
                    mix = mix + jnp.exp(jnp.where(valid, col - row, NEG_INF)) * dts[d][h:h + 1, :]
                    seen[d].append(jnp.exp(col))
                    bw[d].append((bmt_f * w_rows[d][h:h + 1, :]).astype(BF16))
                    scale[d] = scale[d] + jnp.where(head_of_lane == h, grow[d][h:h + 1, :], 0.0)
                l_parts.append((cb * mix).astype(BF16))
            lhs = jnp.concatenate([jnp.concatenate(l_parts, axis=1), jnp.concatenate(bw[0], axis=1),
                                   jnp.concatenate(bw[1], axis=1)], axis=0)
            xs_stack = jnp.concatenate(
                [jnp.where(head_of_lane == h, xs_c, jnp.zeros_like(xs_c)) for h in range(nh)], axis=0)
            staged.append((c, rows, lhs, xs_stack, seen, scale))
        outs = [_dot(lhs, xs_stack) for _, _, lhs, xs_stack, _, _ in staged]
        for (c, rows, _, _, seen, scale), out in zip(staged, outs):
            y_s[rows, :] += out[0:t, :]
            for d in range(2):
                u_s[d, c] = out[(d + 1) * t:(d + 2) * t, :]
                e_s[d, c] = expand64(seen[d])
                scl_s[d, c] = jnp.broadcast_to(scale[d], (SUBLANE, GROUP_WIDTH))

    batch = 16
    def prep_body(i, carry):
        prepare([i * batch + u for u in range(batch)])
        return carry

    lax.fori_loop(0, n_chunks // batch, prep_body, 0)

    def scan_body(i, carry):
        steps = []
        for d, c in ((0, i), (1, n_chunks - 1 - i)):
            rows = pl.ds(pl.multiple_of(c * t, t), t)
            steps.append((d, c, rows, st_s[d], cm_s[rows, :]))
        reads = [_dot(cm_c, st.astype(BF16)) for _, _, _, st, cm_c in steps]
        for (d, c, rows, st, _), read in zip(steps, reads):
            y_s[rows, :] += read * e_s[d, c]
            st_s[d] = st * scl_s[d, c][0:1, :] + u_s[d, c]
        return carry

    lax.fori_loop(0, n_chunks, scan_body, 0, unroll=8)
    y_ref[0] = y_s[...].astype(y_ref.dtype)


def _ssd(proj3, conv_w, conv_b, dtt, alog_col, p64):
    b, seq, _ = proj3.shape
    gw, ns = GROUP_WIDTH, SSD_STATE
    slab0 = OFF_SSD // SSD_SLAB
    kern = functools.partial(_ssd_kernel, seq=seq)
    return pl.pallas_call(
        kern,
        grid=(b, SSD_GROUPS),
        in_specs=[
            pl.BlockSpec((1, seq, SSD_SLAB), lambda i, g: (i, 0, slab0 + g)),
            pl.BlockSpec((CONV_WIDTH, SSD_SLAB), lambda i, g: (0, g)),
            pl.BlockSpec((1, SSD_SLAB), lambda i, g: (0, g)),
            pl.BlockSpec((2 * HEADS_PER_GROUP, seq), lambda i, g: (g, i)),
            pl.BlockSpec((2 * HEADS_PER_GROUP, 1), lambda i, g: (g, 0)),
            pl.BlockSpec((1, SUBLANE, gw), lambda i, g: (g, 0, 0)),
        ],
        out_specs=pl.BlockSpec((1, seq, gw), lambda i, g: (i, 0, g)),
        out_shape=jax.ShapeDtypeStruct((b, seq, SSD_INNER), BF16),
        scratch_shapes=[
            pltpu.VMEM((SSD_SLAB // LANE, seq + 2 * SUBLANE, LANE), F32),
            pltpu.VMEM((seq, gw), BF16),
            pltpu.VMEM((seq, ns), BF16),
            pltpu.VMEM((seq, ns), BF16),
            pltpu.VMEM((seq // SSD_CHUNK, ns, SSD_CHUNK), BF16),
            pltpu.VMEM((seq // SSD_CHUNK, 2 * HEADS_PER_GROUP, SSD_CHUNK), F32),
            pltpu.VMEM((seq, gw), F32),
            pltpu.VMEM((2, ns, gw), F32),
            pltpu.VMEM((SSD_CHUNK, 2 * SSD_CHUNK), BF16),
            pltpu.VMEM((2, seq // SSD_CHUNK, ns, gw), F32),
            pltpu.VMEM((2, seq // SSD_CHUNK, SSD_CHUNK, gw), F32),
            pltpu.VMEM((2, seq // SSD_CHUNK, SUBLANE, gw), F32),
        ],
        compiler_params=pltpu.CompilerParams(
            dimension_semantics=("parallel", "arbitrary"), vmem_limit_bytes=VMEM_LIMIT),
        name="ssd",
    )(proj3, conv_w, conv_b, dtt, alog_col, p64)


def _win_kernel(sink_ref, qkv_ref, o_ref, bias_s, vext_s, *, seq):
    q_cols = slice(0, WIN_REP * HEAD_DIM)
    k_cols = slice(WIN_REP * HEAD_DIM, (WIN_REP + 1) * HEAD_DIM)
    v_cols = slice((WIN_REP + 1) * HEAD_DIM, (WIN_REP + 2) * HEAD_DIM)
    blk = WIN_HALF
    kwin = 3 * blk
    n_blk = seq // blk
    kv = pl.program_id(1)
    scale2 = (HEAD_DIM ** -0.5) * LOG2E

    ri = lax.broadcasted_iota(jnp.int32, (blk, kwin), 0)
    ci = lax.broadcasted_iota(jnp.int32, (blk, kwin), 1)
    for w, shift in enumerate((0, -blk, -2 * blk)):
        bias_s[w] = jnp.where(jnp.abs(ci + shift - ri) <= WIN_HALF, 0.0, NEG_INF).astype(F32)
    vext_s[:, 0:HEAD_DIM] = qkv_ref[0, :, v_cols]
    vext_s[:, HEAD_DIM:2 * HEAD_DIM] = jnp.ones((seq, HEAD_DIM), BF16)
    sinks = [sink_ref[kv * WIN_REP + r] * LOG2E for r in range(WIN_REP)]

    def blocks(specs):
        loaded = []
        for qi, which in specs:
            q0 = qi * blk if isinstance(qi, int) else pl.multiple_of(qi * blk, blk)
            k0 = (0, q0 - blk, seq - kwin)[which]
            if not isinstance(k0, int):
                k0 = pl.multiple_of(k0, blk)
            loaded.append((q0, qkv_ref[0, pl.ds(q0, blk), q_cols], qkv_ref[0, pl.ds(k0, kwin), k_cols],
                           vext_s[pl.ds(k0, kwin), :], bias_s[which]))
        scores = [[_dot_nt(qt[:, r * HEAD_DIM:(r + 1) * HEAD_DIM], kt) for r in range(WIN_REP)]
                  for _, qt, kt, _, _ in loaded]
        maxes, probs = [], []
        for heads, (_, _, _, _, bias) in zip(scores, loaded):
            for r, s in enumerate(heads):
                s = s * scale2 + bias
                m = jnp.maximum(jnp.max(s, axis=-1, keepdims=True), sinks[r])
                maxes.append(m)
                probs.append(jnp.exp2(s - m).astype(BF16))
        accs = [_dot(probs[i * WIN_REP + r], vt)
                for i, (_, _, _, vt, _) in enumerate(loaded) for r in range(WIN_REP)]
        for i, (q0, _, _, _, _) in enumerate(loaded):
            for r in range(WIN_REP):
                acc, m = accs[i * WIN_REP + r], maxes[i * WIN_REP + r]
                den = acc[:, HEAD_DIM:] + jnp.exp2(sinks[r] - m)
                o_ref[0, pl.ds(q0, blk), r * HEAD_DIM:(r + 1) * HEAD_DIM] = (
                    acc[:, :HEAD_DIM] / den).astype(o_ref.dtype)

    blocks([(0, 0), (n_blk - 1, 2)])
    per_trip = 4
    trips = (n_blk - 2) // per_trip

    def body(ti, carry):
        blocks([(1 + per_trip * ti + u, 1) for u in range(per_trip)])
        return carry

    lax.fori_loop(0, trips, body, 0)
    if (n_blk - 2) % per_trip:
        blocks([(qi, 1) for qi in range(1 + trips * per_trip, n_blk - 1)])


def _win_attn(proj3, sink):
    b, seq, _ = proj3.shape
    assert seq >= 3 * WIN_HALF and seq % WIN_HALF == 0
    qw = WIN_REP * HEAD_DIM
    kern = functools.partial(_win_kernel, seq=seq)
    return pl.pallas_call(
        kern,
        grid=(b, WIN_KV_HEADS),
        in_specs=[
            pl.BlockSpec(memory_space=pltpu.SMEM),
            pl.BlockSpec((1, seq, WIN_SLAB), lambda i, g: (i, 0, OFF_WIN // WIN_SLAB + g)),
        ],
        out_specs=pl.BlockSpec((1, seq, qw), lambda i, g: (i, 0, g)),
        out_shape=jax.ShapeDtypeStruct((b, seq, WIN_Q_HEADS * HEAD_DIM), BF16),
        scratch_shapes=[pltpu.VMEM((3, WIN_HALF, 3 * WIN_HALF), F32),
                        pltpu.VMEM((seq, 2 * HEAD_DIM), BF16)],
        compiler_params=pltpu.CompilerParams(
            dimension_semantics=("parallel", "parallel"), vmem_limit_bytes=VMEM_LIMIT),
        name="win_attn",
    )(sink, proj3)


def _dil_kernel(qkv_ref, o_ref, stage_s, qd_s, kd_s, vext_s, od_s, ld_s, og_s, lg_s, bias_s, bias1_s,
                *, seq):
    def cols(part, gi):
        c0 = (part * DIL_GROUPS + gi) * HEAD_DIM
        return slice(c0, c0 + HEAD_DIM)

    tq = 2 * DIL_HALF
    kw = tq + 2 * DIL_HALF
    scale2 = (HEAD_DIM ** -0.5) * LOG2E

    ri = lax.broadcasted_iota(jnp.int32, (tq, kw), 0)
    ci = lax.broadcasted_iota(jnp.int32, (tq, kw), 1)
    for w, shift in enumerate((0, -DIL_HALF, -2 * DIL_HALF)):
        bias_s[w] = jnp.where(jnp.abs(ci + shift - ri) <= DIL_HALF, 0.0, NEG_INF).astype(F32)
    ri1 = lax.broadcasted_iota(jnp.int32, (tq, tq), 0)
    ci1 = lax.broadcasted_iota(jnp.int32, (tq, tq), 1)
    bias1_s[...] = jnp.where(jnp.abs(ci1 - ri1) <= DIL_HALF, 0.0, NEG_INF).astype(F32)
    vext_s[:, HEAD_DIM:2 * HEAD_DIM] = jnp.ones((seq, HEAD_DIM), BF16)

    def deinterleave(src_cols, dst_ref, dil):
        n_sub = seq // dil
        stage_s[...] = qkv_ref[0, :, src_cols].astype(F32)
        for r in range(dil):
            dst_ref[r * n_sub:(r + 1) * n_sub, 0:HEAD_DIM] = (
                stage_s[pl.ds(r, n_sub, stride=dil), :].astype(BF16))

    def aligned(x, mult):
        return x if isinstance(x, int) else pl.multiple_of(x, mult)

    def run_group(q_at, k_at, n_sub):
        def tiles(specs):
            loaded = []
            for q0, k0, width, bias in specs:
                q0, k0 = aligned(q0, tq), aligned(k0, DIL_HALF)
                loaded.append((q0, q_at(pl.ds(q0, tq)), k_at(pl.ds(k0, width)),
                               vext_s[pl.ds(k0, width), :], bias))
            scores = [_dot_nt(qt, kt) for _, qt, kt, _, _ in loaded]
            maxes, probs = [], []
            for s, (_, _, _, _, bias) in zip(scores, loaded):
                s = s * scale2 + bias
                m = jnp.max(s, axis=-1, keepdims=True)
                maxes.append(m)
                probs.append(jnp.exp2(s - m).astype(BF16))
            accs = [_dot(p, vt) for p, (_, _, _, vt, _) in zip(probs, loaded)]
            for acc, m, (q0, _, _, _, _) in zip(accs, maxes, loaded):
                den = acc[:, HEAD_DIM:]
                od_s[pl.ds(q0, tq), :] = acc[:, :HEAD_DIM] / den
                ld_s[pl.ds(q0, tq), :] = m + jnp.log2(den)

        def first(seg0):
            return (seg0, seg0, kw, bias_s[0])

        def inner(q0):
            return (q0, q0 - DIL_HALF, kw, bias_s[1])

        def last(seg0):
            return (seg0 + n_sub - tq, seg0 + n_sub - kw, kw, bias_s[2])

        tiles_per_seg, n_seg, nb = n_sub // tq, seq // n_sub, TILE_BATCH
        if tiles_per_seg == 1:
            wide = SEGMENT_TILE_BATCH
            def body1(bi, carry):
                tiles([((bi * wide + u) * tq, (bi * wide + u) * tq, tq, bias1_s[...]) for u in range(wide)])
                return carry
            lax.fori_loop(0, seq // (tq * wide), body1, 0)
        elif SEGMENT_TILE_BATCH % tiles_per_seg == 0:
            segs = SEGMENT_TILE_BATCH // tiles_per_seg
            def seg_body(si, carry):
                specs = []
                for u in range(segs):
                    seg0 = (si * segs + u) * n_sub
                    specs += ([first(seg0)] + [inner(seg0 + ti * tq) for ti in range(1, tiles_per_seg - 1)]
                              + [last(seg0)])
                tiles(specs)
                return carry
            lax.fori_loop(0, n_seg // segs, seg_body, 0)
        else:
            assert n_seg == 1
            loops = (tiles_per_seg - 2) // nb
            tiles([first(0)])
            def inner_body(bi, carry):
                tiles([inner((1 + bi * nb + u) * tq) for u in range(nb)])
                return carry
            lax.fori_loop(0, loops, inner_body, 0)
            tiles([inner(ti * tq) for ti in range(1 + loops * nb, tiles_per_seg - 1)] + [last(0)])

    for gi, (_, dil) in enumerate(DIL_PATTERNS):
        n_sub = seq // dil
        if dil == 1:
            vext_s[:, 0:HEAD_DIM] = qkv_ref[0, :, cols(2, gi)]
            run_group(lambda sl: qkv_ref[0, sl, cols(0, gi)], lambda sl: qkv_ref[0, sl, cols(1, gi)], n_sub)
            og_s[gi] = od_s[...]
            lg_s[gi] = ld_s[...]
        else:
            deinterleave(cols(0, gi), qd_s, dil)
            deinterleave(cols(1, gi), kd_s, dil)
            deinterleave(cols(2, gi), vext_s, dil)
            run_group(lambda sl: qd_s[sl, :], lambda sl: kd_s[sl, :], n_sub)
            for r in range(dil):
                og_s[gi, pl.ds(r, n_sub, stride=dil), :] = od_s[r * n_sub:(r + 1) * n_sub, :]
                lg_s[gi, pl.ds(r, n_sub, stride=dil), :] = ld_s[r * n_sub:(r + 1) * n_sub, :]

    m = jnp.maximum(jnp.maximum(lg_s[0], lg_s[1]), lg_s[2])
    num = jnp.zeros((seq, HEAD_DIM), F32)
    den = jnp.zeros((seq, HEAD_DIM), F32)
    for gi in range(DIL_GROUPS):
        w = jnp.exp2(lg_s[gi] - m)
        num = num + w * og_s[gi]
        den = den + w
    o_ref[0] = (num / den).astype(o_ref.dtype)


def _dil_attn(proj3):
    b, seq, _ = proj3.shape
    tq = 2 * DIL_HALF
    for _, dil in DIL_PATTERNS:
        assert (seq // dil) % tq == 0
    kern = functools.partial(_dil_kernel, seq=seq)
    return pl.pallas_call(
        kern,
        grid=(b, DIL_HEADS),
        in_specs=[pl.BlockSpec((1, seq, DIL_SLAB), lambda i, h: (i, 0, OFF_DIL // DIL_SLAB + h))],
        out_specs=pl.BlockSpec((1, seq, HEAD_DIM), lambda i, h: (i, 0, h)),
        out_shape=jax.ShapeDtypeStruct((b, seq, DIL_WIDTH), BF16),
        scratch_shapes=[
            pltpu.VMEM((seq, HEAD_DIM), F32),
            pltpu.VMEM((seq, HEAD_DIM), BF16),
            pltpu.VMEM((seq, HEAD_DIM), BF16),
            pltpu.VMEM((seq, 2 * HEAD_DIM), BF16),
            pltpu.VMEM((seq, HEAD_DIM), F32),
            pltpu.VMEM((seq, HEAD_DIM), F32),
            pltpu.VMEM((DIL_GROUPS, seq, HEAD_DIM), F32),
            pltpu.VMEM((DIL_GROUPS, seq, HEAD_DIM), F32),
            pltpu.VMEM((3, tq, tq + 2 * DIL_HALF), F32),
            pltpu.VMEM((tq, tq), F32),
        ],
        compiler_params=pltpu.CompilerParams(
            dimension_semantics=("parallel", "parallel"), vmem_limit_bytes=VMEM_LIMIT),
        name="dil_attn",
    )(proj3)


def _rope_tables(seq):
    inv = ROPE_THETA ** (-jnp.arange(0, ROPE_DIM, 2, dtype=F32) / ROPE_DIM)
    ang = jnp.arange(seq, dtype=F32)[:, None] * inv[None, :]
    cos, sin = jnp.cos(ang), jnp.sin(ang)
    gap = jnp.zeros((seq, ROPE_PARTNER - ROPE_HALF), F32)
    ct = jnp.concatenate([cos, gap + 1.0, cos, gap + 1.0], axis=1)
    st = jnp.concatenate([-sin, gap, sin, gap], axis=1)
    return ct, st


def _group_major(v):
    return v.reshape(2, SSD_GROUPS, HEADS_PER_GROUP).transpose(1, 0, 2).reshape(DT_COLS)


def _ssd_slabs(a):
    rows = a.shape[0]
    x = a[:, :SSD_INNER].reshape(rows, SSD_GROUPS, GROUP_WIDTH)
    bm = a[:, SSD_INNER:SSD_INNER + SSD_GROUPS * SSD_STATE].reshape(rows, SSD_GROUPS, SSD_STATE)
    cm = a[:, SSD_INNER + SSD_GROUPS * SSD_STATE:].reshape(rows, SSD_GROUPS, SSD_STATE)
    return jnp.concatenate([x, bm, cm], axis=2).reshape(rows, XBC_WIDTH)


def _rope_lanes(a, axis):
    idx = lambda lo, hi: lax.slice_in_dim(a, lo, hi, axis=axis)
    return jnp.concatenate([idx(0, ROPE_HALF), idx(ROPE_DIM, ROPE_DIM + ROPE_PARTNER - ROPE_HALF),
                            idx(ROPE_HALF, ROPE_DIM), idx(ROPE_DIM + ROPE_PARTNER - ROPE_HALF, HEAD_DIM)],
                           axis=axis)


def _main_weight(w_in):
    d = w_in.shape[0]
    o = DT_IN_OFF + DT_COLS
    n_dil = 3 * DIL_GROUPS * DIL_WIDTH
    n_q, n_kv = WIN_Q_HEADS * HEAD_DIM, WIN_KV_HEADS * HEAD_DIM
    w_in = w_in.astype(BF16)
    lane_perm = _rope_lanes(jnp.eye(HEAD_DIM, dtype=BF16), 1)

    def rope_lanes(heads):
        return jnp.einsum('dhk,kj->dhj', heads, lane_perm, preferred_element_type=BF16)

    z = w_in[:, :SSD_INNER]
    ssd = _ssd_slabs(w_in[:, SSD_INNER:DT_IN_OFF])
    dil = rope_lanes(w_in[:, o:o + n_dil].reshape(d, 3 * DIL_GROUPS * DIL_HEADS, HEAD_DIM))
    dil = dil.reshape(d, 3 * DIL_GROUPS, DIL_HEADS, HEAD_DIM)
    dil = dil.transpose(0, 2, 1, 3).reshape(d, n_dil)
    qk = rope_lanes(w_in[:, o + n_dil:o + n_dil + n_q + n_kv].reshape(d, WIN_Q_HEADS + WIN_KV_HEADS, HEAD_DIM))
    q = qk[:, :WIN_Q_HEADS].reshape(d, WIN_KV_HEADS, WIN_REP * HEAD_DIM)
    k = qk[:, WIN_Q_HEADS:]
    v = w_in[:, o + n_dil + n_q + n_kv:o + n_dil + n_q + 2 * n_kv].reshape(d, WIN_KV_HEADS, HEAD_DIM)
    win = jnp.concatenate([q, k, v], axis=2).reshape(d, WIN_KV_HEADS * WIN_SLAB)
    gates = w_in[:, o + n_dil + n_q + 2 * n_kv:]
    return jnp.concatenate([z, ssd, win, dil, gates], axis=1)


def _layer_params(w_in, dt_bias, a_log, d_skip):
    w_main = _main_weight(w_in)
    w_dt = w_in[:, DT_IN_OFF:DT_IN_OFF + DT_COLS]
    w_dt = w_dt.reshape(D_MODEL, 2, SSD_GROUPS, HEADS_PER_GROUP).transpose(0, 2, 1, 3).reshape(D_MODEL, DT_COLS)
    w_dtt = jnp.pad(w_dt, ((0, 0), (0, DT_LANES - DT_COLS))).astype(BF16).T
    b_dt = jnp.pad(_group_major(dt_bias.astype(F32)), (0, DT_LANES - DT_COLS))
    alog = jnp.pad(_group_major(a_log.astype(F32)), (0, DT_LANES - DT_COLS))
    p64 = jnp.repeat(d_skip.astype(F32).reshape(SSD_GROUPS, 1, HEADS_PER_GROUP), SSD_HEAD_DIM, axis=2)
    p64 = jnp.pad(p64, ((0, 0), (0, SUBLANE - 1), (0, 0)))
    return w_main, w_dtt, b_dt.reshape(DT_LANES, 1), alog.reshape(DT_LANES, 1), p64


def kernel(x, g_mix, w_in, conv_w, conv_b, dt_bias, a_log, d_skip, ssd_norm, w_a, w_b, w_c, sink,
           w_out, g_mlp, w_up, w_down, g_final):
    b, seq, _ = x.shape
    m = b * seq
    tm = min(1024, seq)
    ct, st = _rope_tables(seq)
    xf = x.reshape(m, D_MODEL).astype(F32)
    for i in range(DEPTH):
        w_main, w_dtt, b_dtt, alog_col, p64 = _layer_params(w_in[i], dt_bias[i], a_log[i], d_skip[i])
        proj, dtt = _proj(xf, g_mix[i].reshape(1, D_MODEL), w_main, w_dtt, b_dtt, ct, st, seq,
                          tm=tm, tn=1024)
        proj3 = proj.reshape(b, seq, N_MAIN)
        y_ssd = _ssd(proj3, _ssd_slabs(conv_w[i].astype(F32)),
                     _ssd_slabs(conv_b[i].astype(F32).reshape(1, XBC_WIDTH)), dtt, alog_col, p64)
        y_b = _dil_attn(proj3)
        w_b_heads = _rope_lanes(w_b[i].reshape(DIL_HEADS, HEAD_DIM, D_MODEL), 1)
        y_c = _win_attn(proj3, sink[i].astype(F32))
        merged = _merge(y_ssd.reshape(m, SSD_INNER), proj, ssd_norm[i].reshape(1, SSD_INNER),
                        y_b.reshape(m, DIL_WIDTH), y_c.reshape(m, WIN_Q_HEADS * HEAD_DIM),
                        w_a[i].astype(BF16), w_b_heads.reshape(DIL_WIDTH, D_MODEL).astype(BF16),
                        w_c[i].astype(BF16), tm=tm, tn=256)
        xf = _matmul_res(merged, w_out[i].astype(BF16), xf, tm=min(512, seq), tn=D_MODEL, tk=D_MODEL)
        u = _mlp_up(xf, g_mlp[i].reshape(1, D_MODEL), w_up[i].astype(BF16), tm=tm, tn=1024)
        xf = _matmul_res(u, w_down[i].astype(BF16), xf, tm=tm, tn=1024, tk=2048)
    out = _final_norm(xf, g_final.reshape(1, D_MODEL), tm=min(512, seq))
    return out.reshape(b, seq, D_MODEL).astype(x.dtype)
```

```python
import functools

import jax
import jax.numpy as jnp
from jax import lax
from jax.experimental import pallas as pl
from jax.experimental.pallas import tpu as pltpu

F32 = jnp.float32
BF16 = jnp.bfloat16

D_MODEL = 2048
DEPTH = 2
SSD_HEADS = 32
SSD_HEAD_DIM = 64
SSD_INNER = SSD_HEADS * SSD_HEAD_DIM
SSD_GROUPS = 8
SSD_STATE = 128
SSD_CHUNK = 128
HEADS_PER_GROUP = SSD_HEADS // SSD_GROUPS
GROUP_WIDTH = HEADS_PER_GROUP * SSD_HEAD_DIM
CONV_WIDTH = 5
CONV_PAD = (CONV_WIDTH - 1) // 2
XBC_WIDTH = SSD_INNER + 2 * SSD_GROUPS * SSD_STATE
HEAD_DIM = 128
ROPE_DIM = HEAD_DIM // 4
ROPE_HALF = ROPE_DIM // 2
ROPE_PARTNER = HEAD_DIM // 2
ROPE_THETA = 500000.0
DIL_PATTERNS = ((128, 1), (512, 4), (2048, 16))
DIL_GROUPS = len(DIL_PATTERNS)
DIL_HEADS = 8
DIL_WIDTH = DIL_HEADS * HEAD_DIM
DIL_HALF = 64
TILE_BATCH = 7
SEGMENT_TILE_BATCH = 8
WIN_Q_HEADS = 16
WIN_KV_HEADS = 4
WIN_REP = WIN_Q_HEADS // WIN_KV_HEADS
WIN_HALF = 128
D_FF = 4 * D_MODEL
N_BRANCH = 3
EPS = 1e-6
NEG_INF = -1e30
LOG2E = 1.4426950408889634

SSD_SLAB = GROUP_WIDTH + 2 * SSD_STATE
WIN_SLAB = (WIN_REP + 2) * HEAD_DIM
DIL_SLAB = 3 * DIL_GROUPS * HEAD_DIM
OFF_Z = 0
OFF_SSD = OFF_Z + SSD_INNER
OFF_WIN = OFF_SSD + SSD_GROUPS * SSD_SLAB
OFF_DIL = OFF_WIN + WIN_KV_HEADS * WIN_SLAB
OFF_GATE = OFF_DIL + DIL_HEADS * DIL_SLAB
N_MAIN = OFF_GATE + N_BRANCH * D_MODEL
assert OFF_SSD % SSD_SLAB == 0 and OFF_WIN % WIN_SLAB == 0 and OFF_DIL % DIL_SLAB == 0
DT_COLS = 2 * SSD_HEADS
DT_IN_OFF = SSD_INNER + XBC_WIDTH
DT_LANES = 128

LANE = 128
SUBLANE = 8
VMEM_LIMIT = 56 * 1024 * 1024


def _split3(x):
    hi = x.astype(BF16)
    r1 = x - hi.astype(F32)
    mid = r1.astype(BF16)
    r2 = r1 - mid.astype(F32)
    return hi, mid, r2.astype(BF16)


def _dot(a, b):
    return jnp.dot(a, b, preferred_element_type=F32)


def _dot_nt(a, b):
    return lax.dot_general(a, b, (((1,), (1,)), ((), ())), preferred_element_type=F32)


def _exact_right(x, mat):
    hi, mid, lo = _split3(x)
    return _dot(hi, mat) + _dot(mid, mat) + _dot(lo, mat)


def _silu(x):
    h = 0.5 * x
    return h * jnp.tanh(h) + h


def _softplus(x):
    return jnp.maximum(x, 0.0) + jnp.log1p(jnp.exp(-jnp.abs(x)))


def _norm_rows(x, g):
    ms = jnp.mean(x * x, axis=-1, keepdims=True)
    return x * lax.rsqrt(ms + EPS) * g


def _rope_tile(a, ct, st):
    return a * ct + pltpu.roll(a, ROPE_PARTNER, 1) * st


def _proj_kernel(x_ref, g_ref, w_ref, wdtt_ref, bdtt_ref, ct_ref, st_ref,
                 o_ref, dtt_ref, h_ref, *, tn, tile_kinds):
    j = pl.program_id(1)

    @pl.when(j == 0)
    def _():
        hb = _norm_rows(x_ref[...], g_ref[...]).astype(BF16)
        h_ref[...] = hb
        dtt_ref[...] = _softplus(_dot_nt(wdtt_ref[...], hb) + bdtt_ref[...])

    def tile_body(kind):
        acc = _dot(h_ref[...], w_ref[...])
        if 'r' in kind:
            ct, st = ct_ref[...], st_ref[...]
        for c, k in enumerate(kind):
            sl = slice(c * HEAD_DIM, (c + 1) * HEAD_DIM)
            a = acc[:, sl]
            if k == 'r':
                a = _rope_tile(a, ct, st)
            elif k == 'g':
                a = 0.5 * jnp.tanh(0.5 * a) + 0.5
            o_ref[:, sl] = a.astype(o_ref.dtype)

    for kind in sorted(set(tile_kinds)):
        idx = [t for t, k in enumerate(tile_kinds) if k == kind]
        runs, start = [], idx[0]
        for a, b in zip(idx, idx[1:] + [None]):
            if b != a + 1:
                runs.append((start, a))
                start = b
        cond = None
        for lo, hi in runs:
            c = jnp.logical_and(j >= lo, j <= hi)
            cond = c if cond is None else jnp.logical_or(cond, c)
        pl.when(cond)(functools.partial(tile_body, kind))


def _proj_tile_kinds(tn):
    def head_kind(col):
        if col >= OFF_GATE:
            return 'g'
        if col >= OFF_DIL:
            return 'r'
        if col >= OFF_WIN:
            return 'r' if (col - OFF_WIN) % WIN_SLAB < (WIN_REP + 1) * HEAD_DIM else 'p'
        return 'p'
    return tuple(''.join(head_kind(t * tn + c * HEAD_DIM) for c in range(tn // HEAD_DIM))
                 for t in range(N_MAIN // tn))


def _proj(x2d, g, w_main, w_dtt, b_dtt, ct, st, seq, *, tm, tn):
    m = x2d.shape[0]
    assert seq % tm == 0 and m % tm == 0 and N_MAIN % tn == 0 and tn % HEAD_DIM == 0
    seq_tiles = seq // tm
    kern = functools.partial(_proj_kernel, tn=tn, tile_kinds=_proj_tile_kinds(tn))
    return pl.pallas_call(
        kern,
        grid=(m // tm, N_MAIN // tn),
        in_specs=[
            pl.BlockSpec((tm, D_MODEL), lambda i, j: (i, 0)),
            pl.BlockSpec((1, D_MODEL), lambda i, j: (0, 0)),
            pl.BlockSpec((D_MODEL, tn), lambda i, j: (0, j)),
            pl.BlockSpec((DT_LANES, D_MODEL), lambda i, j: (0, 0)),
            pl.BlockSpec((DT_LANES, 1), lambda i, j: (0, 0)),
            pl.BlockSpec((tm, HEAD_DIM), lambda i, j: (i % seq_tiles, 0)),
            pl.BlockSpec((tm, HEAD_DIM), lambda i, j: (i % seq_tiles, 0)),
        ],
        out_specs=[
            pl.BlockSpec((tm, tn), lambda i, j: (i, j)),
            pl.BlockSpec((DT_LANES, tm), lambda i, j: (0, i)),
        ],
        out_shape=[
            jax.ShapeDtypeStruct((m, N_MAIN), BF16),
            jax.ShapeDtypeStruct((DT_LANES, m), F32),
        ],
        scratch_shapes=[pltpu.VMEM((tm, D_MODEL), BF16)],
        compiler_params=pltpu.CompilerParams(
            dimension_semantics=("parallel", "arbitrary"), vmem_limit_bytes=VMEM_LIMIT),
        name="proj",
    )(x2d, g, w_main, w_dtt, b_dtt, ct, st)


def _up_kernel(x_ref, g_ref, w_ref, o_ref, h_ref):
    @pl.when(pl.program_id(1) == 0)
    def _():
        h_ref[...] = _norm_rows(x_ref[...], g_ref[...]).astype(BF16)

    acc = jnp.maximum(_dot(h_ref[...], w_ref[...]), 0.0)
    o_ref[...] = (acc * acc).astype(o_ref.dtype)


def _mlp_up(x2d, g, w, *, tm, tn):
    m, n = x2d.shape[0], w.shape[1]
    return pl.pallas_call(
        _up_kernel,
        grid=(m // tm, n // tn),
        in_specs=[
            pl.BlockSpec((tm, D_MODEL), lambda i, j: (i, 0)),
            pl.BlockSpec((1, D_MODEL), lambda i, j: (0, 0)),
            pl.BlockSpec((D_MODEL, tn), lambda i, j: (0, j)),
        ],
        out_specs=pl.BlockSpec((tm, tn), lambda i, j: (i, j)),
        out_shape=jax.ShapeDtypeStruct((m, n), BF16),
        scratch_shapes=[pltpu.VMEM((tm, D_MODEL), BF16)],
        compiler_params=pltpu.CompilerParams(
            dimension_semantics=("parallel", "arbitrary"), vmem_limit_bytes=VMEM_LIMIT),
        name="mlp_up",
    )(x2d, g, w)


def _matmul_res_kernel(a_ref, w_ref, r_ref, o_ref, acc_ref):
    k = pl.program_id(2)

    @pl.when(k == 0)
    def _():
        acc_ref[...] = jnp.zeros_like(acc_ref)

    acc_ref[...] += _dot(a_ref[...], w_ref[...])

    @pl.when(k == pl.num_programs(2) - 1)
    def _():
        o_ref[...] = r_ref[...] + acc_ref[...]


def _matmul_res(a, w, res, *, tm, tn, tk):
    m, kdim = a.shape
    n = w.shape[1]
    return pl.pallas_call(
        _matmul_res_kernel,
        grid=(m // tm, n // tn, kdim // tk),
        in_specs=[
            pl.BlockSpec((tm, tk), lambda i, j, k: (i, k)),
            pl.BlockSpec((tk, tn), lambda i, j, k: (k, j)),
            pl.BlockSpec((tm, tn), lambda i, j, k: (i, j)),
        ],
        out_specs=pl.BlockSpec((tm, tn), lambda i, j, k: (i, j)),
        out_shape=jax.ShapeDtypeStruct((m, n), F32),
        scratch_shapes=[pltpu.VMEM((tm, tn), F32)],
        compiler_params=pltpu.CompilerParams(
            dimension_semantics=("parallel", "parallel", "arbitrary"), vmem_limit_bytes=VMEM_LIMIT),
        name="matmul_res",
    )(a, w, res)


def _gated_norm(ys, z, nw):
    z = z.astype(F32)
    y = ys.astype(F32) * _silu(z)
    return _norm_rows(y, nw).astype(BF16)


def _merge_kernel(ys_ref, z_ref, nw_ref, yb_ref, yc_ref, g0_ref, g1_ref, g2_ref,
                  wa_ref, wb_ref, wc_ref, o_ref, ya_ref):
    @pl.when(pl.program_id(1) == 0)
    def _():
        ya_ref[...] = _gated_norm(ys_ref[...], z_ref[...], nw_ref[...])

    acc = g0_ref[...].astype(F32) * _dot(ya_ref[...], wa_ref[...])
    acc += g1_ref[...].astype(F32) * _dot(yb_ref[...], wb_ref[...])
    acc += g2_ref[...].astype(F32) * _dot(yc_ref[...], wc_ref[...])
    o_ref[...] = acc.astype(o_ref.dtype)


def _merge(y_ssd, proj, norm_w, y_b, y_c, w_a, w_b, w_c, *, tm, tn):
    m = y_ssd.shape[0]
    gate_blk = OFF_GATE // tn
    per_gate = D_MODEL // tn
    return pl.pallas_call(
        _merge_kernel,
        grid=(m // tm, D_MODEL // tn),
        in_specs=[
            pl.BlockSpec((tm, SSD_INNER), lambda i, j: (i, 0)),
            pl.BlockSpec((tm, SSD_INNER), lambda i, j: (i, OFF_Z // SSD_INNER)),
            pl.BlockSpec((1, SSD_INNER), lambda i, j: (0, 0)),
            pl.BlockSpec((tm, DIL_WIDTH), lambda i, j: (i, 0)),
            pl.BlockSpec((tm, WIN_Q_HEADS * HEAD_DIM), lambda i, j: (i, 0)),
            pl.BlockSpec((tm, tn), lambda i, j: (i, gate_blk + j)),
            pl.BlockSpec((tm, tn), lambda i, j: (i, gate_blk + per_gate + j)),
            pl.BlockSpec((tm, tn), lambda i, j: (i, gate_blk + 2 * per_gate + j)),
            pl.BlockSpec((SSD_INNER, tn), lambda i, j: (0, j)),
            pl.BlockSpec((DIL_WIDTH, tn), lambda i, j: (0, j)),
            pl.BlockSpec((WIN_Q_HEADS * HEAD_DIM, tn), lambda i, j: (0, j)),
        ],
        out_specs=pl.BlockSpec((tm, tn), lambda i, j: (i, j)),
        out_shape=jax.ShapeDtypeStruct((m, D_MODEL), BF16),
        scratch_shapes=[pltpu.VMEM((tm, SSD_INNER), BF16)],
        compiler_params=pltpu.CompilerParams(
            dimension_semantics=("parallel", "arbitrary"), vmem_limit_bytes=VMEM_LIMIT),
        name="merge",
    )(y_ssd, proj, norm_w, y_b, y_c, proj, proj, proj, w_a, w_b, w_c)


def _final_norm_kernel(x_ref, g_ref, o_ref):
    o_ref[...] = _norm_rows(x_ref[...], g_ref[...])


def _final_norm(x2d, g, *, tm):
    m = x2d.shape[0]
    return pl.pallas_call(
        _final_norm_kernel,
        grid=(m // tm,),
        in_specs=[pl.BlockSpec((tm, D_MODEL), lambda i: (i, 0)),
                  pl.BlockSpec((1, D_MODEL), lambda i: (0, 0))],
        out_specs=pl.BlockSpec((tm, D_MODEL), lambda i: (i, 0)),
        out_shape=jax.ShapeDtypeStruct((m, D_MODEL), F32),
        compiler_params=pltpu.CompilerParams(dimension_semantics=("parallel",)),
        name="final_norm",
    )(x2d, g)


def _ssd_kernel(xbc_ref, cw_ref, cb_ref, dtt_ref, alt_ref, p64_ref,
                y_ref,
                pad_s, xs_s, bm_s, cm_s, bmt_s, dtr_s, y_s, st_s, tri_s, u_s, e_s, scl_s, *, seq):
    t = SSD_CHUNK
    n_chunks = seq // t
    for c in range(n_chunks):
        dtr_s[c] = dtt_ref[:, c * t:(c + 1) * t]

    halo = SUBLANE
    for lane_tile in range(SSD_SLAB // LANE):
        lanes = slice(lane_tile * LANE, (lane_tile + 1) * LANE)
        pad_s[lane_tile, 0:halo, :] = jnp.zeros((halo, LANE), F32)
        pad_s[lane_tile, halo + seq:2 * halo + seq, :] = jnp.zeros((halo, LANE), F32)
        pad_s[lane_tile, halo:halo + seq, :] = xbc_ref[0, :, lanes].astype(F32)

    def conv_silu(r0, lane_tile):
        lanes = slice(lane_tile * LANE, (lane_tile + 1) * LANE)
        acc = cb_ref[:, lanes]
        for k in range(CONV_WIDTH):
            lo = halo - CONV_PAD + k + r0
            acc = acc + cw_ref[k:k + 1, lanes] * pad_s[lane_tile, lo:lo + t, :]
        return _silu(acc)

    x_tiles = GROUP_WIDTH // LANE
    for c in range(n_chunks):
        r0 = c * t
        rows = slice(r0, r0 + t)
        for lane_tile in range(x_tiles):
            lanes = slice(lane_tile * LANE, (lane_tile + 1) * LANE)
            xv = conv_silu(r0, lane_tile)
            xs_s[rows, lanes] = xv.astype(BF16)
            y_s[rows, lanes] = xv * p64_ref[0, 0:1, lanes]
        bv = conv_silu(r0, x_tiles)
        bm_s[rows, :] = bv.astype(BF16)
        bmt_s[c] = bv.T.astype(BF16)
        cm_s[rows, :] = conv_silu(r0, x_tiles + 1).astype(BF16)

    ri = lax.broadcasted_iota(jnp.int32, (t, t), 0)
    ci = lax.broadcasted_iota(jnp.int32, (t, t), 1)
    lower, upper = ri >= ci, ri <= ci
    tri_s[:, 0:t] = upper.astype(BF16)
    tri_s[:, t:2 * t] = lower.astype(BF16)
    st_s[...] = jnp.zeros_like(st_s)

    nh = HEADS_PER_GROUP
    a_rows = -jnp.exp(alt_ref[...])
    head_of_lane = lax.broadcasted_iota(jnp.int32, (1, GROUP_WIDTH), 1) // SSD_HEAD_DIM
    low_half = lax.broadcasted_iota(jnp.int32, (1, LANE), 1) < SSD_HEAD_DIM

    def lanes_from_row(row):
        return jnp.broadcast_to(row, (t, t)).T

    def expand64(cols):
        return jnp.concatenate([jnp.where(low_half, cols[0], cols[1]),
                                jnp.where(low_half, cols[2], cols[3])], axis=1)

    def prepare(chunks):
        loaded = []
        for c in chunks:
            rows = pl.ds(pl.multiple_of(c * t, t), t)
            loaded.append((c, rows, dtr_s[c], xs_s[rows, :], cm_s[rows, :], bm_s[rows, :], bmt_s[c]))
        cums = [_exact_right(dt_all, tri_s[...]) for _, _, dt_all, _, _, _, _ in loaded]
        cbs = [_dot_nt(cm_c, bm_c) for _, _, _, _, cm_c, bm_c, _ in loaded]
        staged = []
        for (c, rows, dt_all, xs_c, _, _, bmt_c), cum, cb in zip(loaded, cums, cbs):
            cs = (cum[0:nh, 0:t] * a_rows[0:nh], cum[nh:2 * nh, t:2 * t] * a_rows[nh:2 * nh])
            dts = (dt_all[0:nh, :], dt_all[nh:2 * nh, :])
            tots = (cs[0][:, t - 1:t], cs[1][:, 0:1])
            w_rows = [dts[d] * jnp.exp(tots[d] - cs[d]) for d in range(2)]
            grow = [jnp.exp(tots[d]) for d in range(2)]
            bmt_f = bmt_c.astype(F32)
            l_parts, bw, seen, scale = [], ([], []), ([], []), [0.0, 0.0]
            for h in range(nh):
                mix = jnp.zeros((t, t), F32)
                for d, valid in ((0, lower), (1, upper)):
                    row = cs[d][h:h + 1, :]
                    col = lanes_from_row(row)
                    mix = mix + jnp.exp(jnp.where(valid, col - row, NEG_INF)) * dts[d][h:h + 1, :]
                    seen[d].append(jnp.exp(col))
                    bw[d].append((bmt_f * w_rows[d][h:h + 1, :]).astype(BF16))
                    scale[d] = scale[d] + jnp.where(head_of_lane == h, grow[d][h:h + 1, :], 0.0)
                l_parts.append((cb * mix).astype(BF16))
            lhs = jnp.concatenate([jnp.concatenate(l_parts, axis=1), jnp.concatenate(bw[0], axis=1),
                                   jnp.concatenate(bw[1], axis=1)], axis=0)
            xs_stack = jnp.concatenate(
                [jnp.where(head_of_lane == h, xs_c, jnp.zeros_like(xs_c)) for h in range(nh)], axis=0)
            staged.append((c, rows, lhs, xs_stack, seen, scale))
        outs = [_dot(lhs, xs_stack) for _, _, lhs, xs_stack, _, _ in staged]
        for (c, rows, _, _, seen, scale), out in zip(staged, outs):
            y_s[rows, :] += out[0:t, :]
            for d in range(2):
                u_s[d, c] = out[(d + 1) * t:(d + 2) * t, :]
                e_s[d, c] = expand64(seen[d])
                scl_s[d, c] = jnp.broadcast_to(scale[d], (SUBLANE, GROUP_WIDTH))

    batch = 16
    def prep_body(i, carry):
        prepare([i * batch + u for u in range(batch)])
        return carry

    lax.fori_loop(0, n_chunks // batch, prep_body, 0)

    def scan_body(i, carry):
        steps = []
        for d, c in ((0, i), (1, n_chunks - 1 - i)):
            rows = pl.ds(pl.multiple_of(c * t, t), t)
            steps.append((d, c, rows, st_s[d], cm_s[rows, :]))
        reads = [_dot(cm_c, st.astype(BF16)) for _, _, _, st, cm_c in steps]
        for (d, c, rows, st, _), read in zip(steps, reads):
            y_s[rows, :] += read * e_s[d, c]
            st_s[d] = st * scl_s[d, c][0:1, :] + u_s[d, c]
        return carry

    lax.fori_loop(0, n_chunks, scan_body, 0, unroll=8)
    y_ref[0] = y_s[...].astype(y_ref.dtype)


def _ssd(proj3, conv_w, conv_b, dtt, alog_col, p64):
    b, seq, _ = proj3.shape
    gw, ns = GROUP_WIDTH, SSD_STATE
    slab0 = OFF_SSD // SSD_SLAB
    kern = functools.partial(_ssd_kernel, seq=seq)
    return pl.pallas_call(
        kern,
        grid=(b, SSD_GROUPS),
        in_specs=[
            pl.BlockSpec((1, seq, SSD_SLAB), lambda i, g: (i, 0, slab0 + g)),
            pl.BlockSpec((CONV_WIDTH, SSD_SLAB), lambda i, g: (0, g)),
            pl.BlockSpec((1, SSD_SLAB), lambda i, g: (0, g)),
            pl.BlockSpec((2 * HEADS_PER_GROUP, seq), lambda i, g: (g, i)),
            pl.BlockSpec((2 * HEADS_PER_GROUP, 1), lambda i, g: (g, 0)),
            pl.BlockSpec((1, SUBLANE, gw), lambda i, g: (g, 0, 0)),
        ],
        out_specs=pl.BlockSpec((1, seq, gw), lambda i, g: (i, 0, g)),
        out_shape=jax.ShapeDtypeStruct((b, seq, SSD_INNER), BF16),
        scratch_shapes=[
            pltpu.VMEM((SSD_SLAB // LANE, seq + 2 * SUBLANE, LANE), F32),
            pltpu.VMEM((seq, gw), BF16),
            pltpu.VMEM((seq, ns), BF16),
            pltpu.VMEM((seq, ns), BF16),
            pltpu.VMEM((seq // SSD_CHUNK, ns, SSD_CHUNK), BF16),
            pltpu.VMEM((seq // SSD_CHUNK, 2 * HEADS_PER_GROUP, SSD_CHUNK), F32),
            pltpu.VMEM((seq, gw), F32),
            pltpu.VMEM((2, ns, gw), F32),
            pltpu.VMEM((SSD_CHUNK, 2 * SSD_CHUNK), BF16),
            pltpu.VMEM((2, seq // SSD_CHUNK, ns, gw), F32),
            pltpu.VMEM((2, seq // SSD_CHUNK, SSD_CHUNK, gw), F32),
            pltpu.VMEM((2, seq // SSD_CHUNK, SUBLANE, gw), F32),
        ],
        compiler_params=pltpu.CompilerParams(
            dimension_semantics=("parallel", "arbitrary"), vmem_limit_bytes=VMEM_LIMIT),
        name="ssd",
    )(proj3, conv_w, conv_b, dtt, alog_col, p64)


def _win_kernel(sink_ref, qkv_ref, o_ref, bias_s, vext_s, *, seq):
    q_cols = slice(0, WIN_REP * HEAD_DIM)
    k_cols = slice(WIN_REP * HEAD_DIM, (WIN_REP + 1) * HEAD_DIM)
    v_cols = slice((WIN_REP + 1) * HEAD_DIM, (WIN_REP + 2) * HEAD_DIM)
    blk = WIN_HALF
    kwin = 3 * blk
    n_blk = seq // blk
    kv = pl.program_id(1)
    scale2 = (HEAD_DIM ** -0.5) * LOG2E

    ri = lax.broadcasted_iota(jnp.int32, (blk, kwin), 0)
    ci = lax.broadcasted_iota(jnp.int32, (blk, kwin), 1)
    for w, shift in enumerate((0, -blk, -2 * blk)):
        bias_s[w] = jnp.where(jnp.abs(ci + shift - ri) <= WIN_HALF, 0.0, NEG_INF).astype(F32)
    vext_s[:, 0:HEAD_DIM] = qkv_ref[0, :, v_cols]
    vext_s[:, HEAD_DIM:2 * HEAD_DIM] = jnp.ones((seq, HEAD_DIM), BF16)
    sinks = [sink_ref[kv * WIN_REP + r] * LOG2E for r in range(WIN_REP)]

    def blocks(specs):
        loaded = []
        for qi, which in specs:
            q0 = qi * blk if isinstance(qi, int) else pl.multiple_of(qi * blk, blk)
            k0 = (0, q0 - blk, seq - kwin)[which]
            if not isinstance(k0, int):
                k0 = pl.multiple_of(k0, blk)
            loaded.append((q0, qkv_ref[0, pl.ds(q0, blk), q_cols], qkv_ref[0, pl.ds(k0, kwin), k_cols],
                           vext_s[pl.ds(k0, kwin), :], bias_s[which]))
        scores = [[_dot_nt(qt[:, r * HEAD_DIM:(r + 1) * HEAD_DIM], kt) for r in range(WIN_REP)]
                  for _, qt, kt, _, _ in loaded]
        maxes, probs = [], []
        for heads, (_, _, _, _, bias) in zip(scores, loaded):
            for r, s in enumerate(heads):
                s = s * scale2 + bias
                m = jnp.maximum(jnp.max(s, axis=-1, keepdims=True), sinks[r])
                maxes.append(m)
                probs.append(jnp.exp2(s - m).astype(BF16))
        accs = [_dot(probs[i * WIN_REP + r], vt)
                for i, (_, _, _, vt, _) in enumerate(loaded) for r in range(WIN_REP)]
        for i, (q0, _, _, _, _) in enumerate(loaded):
            for r in range(WIN_REP):
                acc, m = accs[i * WIN_REP + r], maxes[i * WIN_REP + r]
                den = acc[:, HEAD_DIM:] + jnp.exp2(sinks[r] - m)
                o_ref[0, pl.ds(q0, blk), r * HEAD_DIM:(r + 1) * HEAD_DIM] = (
                    acc[:, :HEAD_DIM] / den).astype(o_ref.dtype)

    blocks([(0, 0), (n_blk - 1, 2)])
    per_trip = 7
    trips = (n_blk - 2) // per_trip

    def body(ti, carry):
        blocks([(1 + per_trip * ti + u, 1) for u in range(per_trip)])
        return carry

    lax.fori_loop(0, trips, body, 0)
    if (n_blk - 2) % per_trip:
        blocks([(qi, 1) for qi in range(1 + trips * per_trip, n_blk - 1)])


def _win_attn(proj3, sink):
    b, seq, _ = proj3.shape
    assert seq >= 3 * WIN_HALF and seq % WIN_HALF == 0
    qw = WIN_REP * HEAD_DIM
    kern = functools.partial(_win_kernel, seq=seq)
    return pl.pallas_call(
        kern,
        grid=(b, WIN_KV_HEADS),
        in_specs=[
            pl.BlockSpec(memory_space=pltpu.SMEM),
            pl.BlockSpec((1, seq, WIN_SLAB), lambda i, g: (i, 0, OFF_WIN // WIN_SLAB + g)),
        ],
        out_specs=pl.BlockSpec((1, seq, qw), lambda i, g: (i, 0, g)),
        out_shape=jax.ShapeDtypeStruct((b, seq, WIN_Q_HEADS * HEAD_DIM), BF16),
        scratch_shapes=[pltpu.VMEM((3, WIN_HALF, 3 * WIN_HALF), F32),
                        pltpu.VMEM((seq, 2 * HEAD_DIM), BF16)],
        compiler_params=pltpu.CompilerParams(
            dimension_semantics=("parallel", "parallel"), vmem_limit_bytes=VMEM_LIMIT),
        name="win_attn",
    )(sink, proj3)


def _dil_kernel(qkv_ref, o_ref, stage_s, qd_s, kd_s, vext_s, od_s, ld_s, og_s, lg_s, bias_s, bias1_s,
                *, seq):
    def cols(part, gi):
        c0 = (part * DIL_GROUPS + gi) * HEAD_DIM
        return slice(c0, c0 + HEAD_DIM)

    tq = 2 * DIL_HALF
    kw = tq + 2 * DIL_HALF
    scale2 = (HEAD_DIM ** -0.5) * LOG2E

    ri = lax.broadcasted_iota(jnp.int32, (tq, kw), 0)
    ci = lax.broadcasted_iota(jnp.int32, (tq, kw), 1)
    for w, shift in enumerate((0, -DIL_HALF, -2 * DIL_HALF)):
        bias_s[w] = jnp.where(jnp.abs(ci + shift - ri) <= DIL_HALF, 0.0, NEG_INF).astype(F32)
    ri1 = lax.broadcasted_iota(jnp.int32, (tq, tq), 0)
    ci1 = lax.broadcasted_iota(jnp.int32, (tq, tq), 1)
    bias1_s[...] = jnp.where(jnp.abs(ci1 - ri1) <= DIL_HALF, 0.0, NEG_INF).astype(F32)
    vext_s[:, HEAD_DIM:2 * HEAD_DIM] = jnp.ones((seq, HEAD_DIM), BF16)

    def deinterleave(src_cols, dst_ref, dil):
        n_sub = seq // dil
        stage_s[...] = qkv_ref[0, :, src_cols].astype(F32)
        for r in range(dil):
            dst_ref[r * n_sub:(r + 1) * n_sub, 0:HEAD_DIM] = (
                stage_s[pl.ds(r, n_sub, stride=dil), :].astype(BF16))

    def aligned(x, mult):
        return x if isinstance(x, int) else pl.multiple_of(x, mult)

    def run_group(q_at, k_at, n_sub):
        def tiles(specs):
            loaded = []
            for q0, k0, width, bias in specs:
                q0, k0 = aligned(q0, tq), aligned(k0, DIL_HALF)
                loaded.append((q0, q_at(pl.ds(q0, tq)), k_at(pl.ds(k0, width)),
                               vext_s[pl.ds(k0, width), :], bias))
            scores = [_dot_nt(qt, kt) for _, qt, kt, _, _ in loaded]
            maxes, probs = [], []
            for s, (_, _, _, _, bias) in zip(scores, loaded):
                s = s * scale2 + bias
                m = jnp.max(s, axis=-1, keepdims=True)
                maxes.append(m)
                probs.append(jnp.exp2(s - m).astype(BF16))
            accs = [_dot(p, vt) for p, (_, _, _, vt, _) in zip(probs, loaded)]
            for acc, m, (q0, _, _, _, _) in zip(accs, maxes, loaded):
                den = acc[:, HEAD_DIM:]
                od_s[pl.ds(q0, tq), :] = acc[:, :HEAD_DIM] / den
                ld_s[pl.ds(q0, tq), :] = m + jnp.log2(den)

        def first(seg0):
            return (seg0, seg0, kw, bias_s[0])

        def inner(q0):
            return (q0, q0 - DIL_HALF, kw, bias_s[1])

        def last(seg0):
            return (seg0 + n_sub - tq, seg0 + n_sub - kw, kw, bias_s[2])

        tiles_per_seg, n_seg, nb = n_sub // tq, seq // n_sub, TILE_BATCH
        if tiles_per_seg == 1:
            wide = SEGMENT_TILE_BATCH
            def body1(bi, carry):
                tiles([((bi * wide + u) * tq, (bi * wide + u) * tq, tq, bias1_s[...]) for u in range(wide)])
                return carry
            lax.fori_loop(0, seq // (tq * wide), body1, 0)
        elif SEGMENT_TILE_BATCH % tiles_per_seg == 0:
            segs = SEGMENT_TILE_BATCH // tiles_per_seg
            def seg_body(si, carry):
                specs = []
                for u in range(segs):
                    seg0 = (si * segs + u) * n_sub
                    specs += ([first(seg0)] + [inner(seg0 + ti * tq) for ti in range(1, tiles_per_seg - 1)]
                              + [last(seg0)])
                tiles(specs)
                return carry
            lax.fori_loop(0, n_seg // segs, seg_body, 0)
        else:
            assert n_seg == 1
            loops = (tiles_per_seg - 2) // nb
            tiles([first(0)])
            def inner_body(bi, carry):
                tiles([inner((1 + bi * nb + u) * tq) for u in range(nb)])
                return carry
            lax.fori_loop(0, loops, inner_body, 0)
            tiles([inner(ti * tq) for ti in range(1 + loops * nb, tiles_per_seg - 1)] + [last(0)])

    for gi, (_, dil) in enumerate(DIL_PATTERNS):
        n_sub = seq // dil
        if dil == 1:
            vext_s[:, 0:HEAD_DIM] = qkv_ref[0, :, cols(2, gi)]
            run_group(lambda sl: qkv_ref[0, sl, cols(0, gi)], lambda sl: qkv_ref[0, sl, cols(1, gi)], n_sub)
            og_s[gi] = od_s[...]
            lg_s[gi] = ld_s[...]
        else:
            deinterleave(cols(0, gi), qd_s, dil)
            deinterleave(cols(1, gi), kd_s, dil)
            deinterleave(cols(2, gi), vext_s, dil)
            run_group(lambda sl: qd_s[sl, :], lambda sl: kd_s[sl, :], n_sub)
            for r in range(dil):
                og_s[gi, pl.ds(r, n_sub, stride=dil), :] = od_s[r * n_sub:(r + 1) * n_sub, :]
                lg_s[gi, pl.ds(r, n_sub, stride=dil), :] = ld_s[r * n_sub:(r + 1) * n_sub, :]

    m = jnp.maximum(jnp.maximum(lg_s[0], lg_s[1]), lg_s[2])
    num = jnp.zeros((seq, HEAD_DIM), F32)
    den = jnp.zeros((seq, HEAD_DIM), F32)
    for gi in range(DIL_GROUPS):
        w = jnp.exp2(lg_s[gi] - m)
        num = num + w * og_s[gi]
        den = den + w
    o_ref[0] = (num / den).astype(o_ref.dtype)


def _dil_attn(proj3):
    b, seq, _ = proj3.shape
    tq = 2 * DIL_HALF
    for _, dil in DIL_PATTERNS:
        assert (seq // dil) % tq == 0
    kern = functools.partial(_dil_kernel, seq=seq)
    return pl.pallas_call(
        kern,
        grid=(b, DIL_HEADS),
        in_specs=[pl.BlockSpec((1, seq, DIL_SLAB), lambda i, h: (i, 0, OFF_DIL // DIL_SLAB + h))],
        out_specs=pl.BlockSpec((1, seq, HEAD_DIM), lambda i, h: (i, 0, h)),
        out_shape=jax.ShapeDtypeStruct((b, seq, DIL_WIDTH), BF16),
        scratch_shapes=[
            pltpu.VMEM((seq, HEAD_DIM), F32),
            pltpu.VMEM((seq, HEAD_DIM), BF16),
            pltpu.VMEM((seq, HEAD_DIM), BF16),
            pltpu.VMEM((seq, 2 * HEAD_DIM), BF16),
            pltpu.VMEM((seq, HEAD_DIM), F32),
            pltpu.VMEM((seq, HEAD_DIM), F32),
            pltpu.VMEM((DIL_GROUPS, seq, HEAD_DIM), F32),
            pltpu.VMEM((DIL_GROUPS, seq, HEAD_DIM), F32),
            pltpu.VMEM((3, tq, tq + 2 * DIL_HALF), F32),
            pltpu.VMEM((tq, tq), F32),
        ],
        compiler_params=pltpu.CompilerParams(
            dimension_semantics=("parallel", "parallel"), vmem_limit_bytes=VMEM_LIMIT),
        name="dil_attn",
    )(proj3)


def _rope_tables(seq):
    inv = ROPE_THETA ** (-jnp.arange(0, ROPE_DIM, 2, dtype=F32) / ROPE_DIM)
    ang = jnp.arange(seq, dtype=F32)[:, None] * inv[None, :]
    cos, sin = jnp.cos(ang), jnp.sin(ang)
    gap = jnp.zeros((seq, ROPE_PARTNER - ROPE_HALF), F32)
    ct = jnp.concatenate([cos, gap + 1.0, cos, gap + 1.0], axis=1)
    st = jnp.concatenate([-sin, gap, sin, gap], axis=1)
    return ct, st


def _group_major(v):
    return v.reshape(2, SSD_GROUPS, HEADS_PER_GROUP).transpose(1, 0, 2).reshape(DT_COLS)


def _ssd_slabs(a):
    rows = a.shape[0]
    x = a[:, :SSD_INNER].reshape(rows, SSD_GROUPS, GROUP_WIDTH)
    bm = a[:, SSD_INNER:SSD_INNER + SSD_GROUPS * SSD_STATE].reshape(rows, SSD_GROUPS, SSD_STATE)
    cm = a[:, SSD_INNER + SSD_GROUPS * SSD_STATE:].reshape(rows, SSD_GROUPS, SSD_STATE)
    return jnp.concatenate([x, bm, cm], axis=2).reshape(rows, XBC_WIDTH)


def _rope_lanes(a, axis):
    idx = lambda lo, hi: lax.slice_in_dim(a, lo, hi, axis=axis)
    return jnp.concatenate([idx(0, ROPE_HALF), idx(ROPE_DIM, ROPE_DIM + ROPE_PARTNER - ROPE_HALF),
                            idx(ROPE_HALF, ROPE_DIM), idx(ROPE_DIM + ROPE_PARTNER - ROPE_HALF, HEAD_DIM)],
                           axis=axis)


def _main_weight(w_in):
    d = w_in.shape[0]
    o = DT_IN_OFF + DT_COLS
    n_dil = 3 * DIL_GROUPS * DIL_WIDTH
    n_q, n_kv = WIN_Q_HEADS * HEAD_DIM, WIN_KV_HEADS * HEAD_DIM
    w_in = w_in.astype(BF16)
    lane_perm = _rope_lanes(jnp.eye(HEAD_DIM, dtype=BF16), 1)

    def rope_lanes(heads):
        return jnp.einsum('dhk,kj->dhj', heads, lane_perm, preferred_element_type=BF16)

    z = w_in[:, :SSD_INNER]
    ssd = _ssd_slabs(w_in[:, SSD_INNER:DT_IN_OFF])
    dil = rope_lanes(w_in[:, o:o + n_dil].reshape(d, 3 * DIL_GROUPS * DIL_HEADS, HEAD_DIM))
    dil = dil.reshape(d, 3 * DIL_GROUPS, DIL_HEADS, HEAD_DIM)
    dil = dil.transpose(0, 2, 1, 3).reshape(d, n_dil)
    qk = rope_lanes(w_in[:, o + n_dil:o + n_dil + n_q + n_kv].reshape(d, WIN_Q_HEADS + WIN_KV_HEADS, HEAD_DIM))
    q = qk[:, :WIN_Q_HEADS].reshape(d, WIN_KV_HEADS, WIN_REP * HEAD_DIM)
    k = qk[:, WIN_Q_HEADS:]
    v = w_in[:, o + n_dil + n_q + n_kv:o + n_dil + n_q + 2 * n_kv].reshape(d, WIN_KV_HEADS, HEAD_DIM)
    win = jnp.concatenate([q, k, v], axis=2).reshape(d, WIN_KV_HEADS * WIN_SLAB)
    gates = w_in[:, o + n_dil + n_q + 2 * n_kv:]
    return jnp.concatenate([z, ssd, win, dil, gates], axis=1)


def _layer_params(w_in, dt_bias, a_log, d_skip):
    w_main = _main_weight(w_in)
    w_dt = w_in[:, DT_IN_OFF:DT_IN_OFF + DT_COLS]
    w_dt = w_dt.reshape(D_MODEL, 2, SSD_GROUPS, HEADS_PER_GROUP).transpose(0, 2, 1, 3).reshape(D_MODEL, DT_COLS)
    w_dtt = jnp.pad(w_dt, ((0, 0), (0, DT_LANES - DT_COLS))).astype(BF16).T
    b_dt = jnp.pad(_group_major(dt_bias.astype(F32)), (0, DT_LANES - DT_COLS))
    alog = jnp.pad(_group_major(a_log.astype(F32)), (0, DT_LANES - DT_COLS))
    p64 = jnp.repeat(d_skip.astype(F32).reshape(SSD_GROUPS, 1, HEADS_PER_GROUP), SSD_HEAD_DIM, axis=2)
    p64 = jnp.pad(p64, ((0, 0), (0, SUBLANE - 1), (0, 0)))
    return w_main, w_dtt, b_dt.reshape(DT_LANES, 1), alog.reshape(DT_LANES, 1), p64


def kernel(x, g_mix, w_in, conv_w, conv_b, dt_bias, a_log, d_skip, ssd_norm, w_a, w_b, w_c, sink,
           w_out, g_mlp, w_up, w_down, g_final):
    b, seq, _ = x.shape
    m = b * seq
    tm = min(1024, seq)
    ct, st = _rope_tables(seq)
    xf = x.reshape(m, D_MODEL).astype(F32)
    for i in range(DEPTH):
        w_main, w_dtt, b_dtt, alog_col, p64 = _layer_params(w_in[i], dt_bias[i], a_log[i], d_skip[i])
        proj, dtt = _proj(xf, g_mix[i].reshape(1, D_MODEL), w_main, w_dtt, b_dtt, ct, st, seq,
                          tm=tm, tn=1024)
        proj3 = proj.reshape(b, seq, N_MAIN)
        y_ssd = _ssd(proj3, _ssd_slabs(conv_w[i].astype(F32)),
                     _ssd_slabs(conv_b[i].astype(F32).reshape(1, XBC_WIDTH)), dtt, alog_col, p64)
        y_b = _dil_attn(proj3)
        w_b_heads = _rope_lanes(w_b[i].reshape(DIL_HEADS, HEAD_DIM, D_MODEL), 1)
        y_c = _win_attn(proj3, sink[i].astype(F32))
        merged = _merge(y_ssd.reshape(m, SSD_INNER), proj, ssd_norm[i].reshape(1, SSD_INNER),
                        y_b.reshape(m, DIL_WIDTH), y_c.reshape(m, WIN_Q_HEADS * HEAD_DIM),
                        w_a[i].astype(BF16), w_b_heads.reshape(DIL_WIDTH, D_MODEL).astype(BF16),
                        w_c[i].astype(BF16), tm=tm, tn=256)
        xf = _matmul_res(merged, w_out[i].astype(BF16), xf, tm=min(512, seq), tn=D_MODEL, tk=D_MODEL)
        u = _mlp_up(xf, g_mlp[i].reshape(1, D_MODEL), w_up[i].astype(BF16), tm=tm, tn=1024)
        xf = _matmul_res(u, w_down[i].astype(BF16), xf, tm=tm, tn=1024, tk=2048)
    out = _final_norm(xf, g_final.reshape(1, D_MODEL), tm=min(512, seq))
    return out.reshape(b, seq, D_MODEL).astype(x.dtype)
```
